```python
import jax, jax.numpy as jnp
from jax import lax
import numpy as np

D_MODEL = 1024
BATCH = 8
SEQ = 4096
DEPTH = 2

N_MIXERS = 2
MLA_HEADS = 16
Q_LORA = 384
KV_LORA = 256
QK_NOPE = 64
QK_ROPE = 32
V_HEAD = 64
ROPE_THETA = 10000.0
Q_BLOCK = 128
MLA_IN = Q_LORA + KV_LORA + QK_ROPE
POOL_WINDOWS = (2, 4, 8, 16)
POOL_GROUPS = len(POOL_WINDOWS)
POOL_GROUP_DIM = D_MODEL // POOL_GROUPS
MOE_GROUPS = 4
EXPERTS_PER_GROUP = 8
N_EXPERTS = MOE_GROUPS * EXPERTS_PER_GROUP
MOE_TOP_K = 2
EXPERT_FF = D_MODEL // 4
DN_ALPHA = (2.0 * DEPTH) ** 0.25
DN_BETA = (8.0 * DEPTH) ** -0.25
LN_EPS = 1e-5
RMS_EPS = 1e-6
N_MLA_LAYERS = (DEPTH + N_MIXERS - 1) // N_MIXERS
N_POOL_LAYERS = DEPTH // N_MIXERS

kernel_name = 'hybrid_mla_pool_hmoe_deepnorm_adaln'


def _layer_norm(x, g, b):
    xf = x.astype(jnp.float32)
    mu = jnp.mean(xf, axis=-1, keepdims=True)
    var = jnp.mean(jnp.square(xf - mu), axis=-1, keepdims=True)
    return ((xf - mu) * lax.rsqrt(var + LN_EPS) * g + b).astype(x.dtype)


def _rms_norm(x, g):
    xf = x.astype(jnp.float32)
    return (xf * lax.rsqrt(jnp.mean(jnp.square(xf), axis=-1, keepdims=True) + RMS_EPS) * g).astype(x.dtype)


def _rope_angles(positions):
    inv = 1.0 / (ROPE_THETA ** (jnp.arange(0, QK_ROPE, 2, dtype=jnp.float32) / QK_ROPE))
    ang = positions.astype(jnp.float32)[..., None] * inv
    return jnp.cos(ang), jnp.sin(ang)


def _apply_rope(x, cos, sin):
    x1, x2 = jnp.split(x.astype(jnp.float32), 2, axis=-1)
    return jnp.concatenate([x1 * cos - x2 * sin, x1 * sin + x2 * cos], axis=-1).astype(x.dtype)


def _mla(h, positions, w_in, q_norm_g, kv_norm_g, w_uq, w_ukv, w_o):
    B, S, _ = h.shape
    z = h @ w_in
    q_lat = _rms_norm(z[..., :Q_LORA], q_norm_g)
    kv_lat = _rms_norm(z[..., Q_LORA:Q_LORA + KV_LORA], kv_norm_g)
    cos, sin = _rope_angles(positions)
    k_rope = _apply_rope(z[..., Q_LORA + KV_LORA:], cos, sin)
    q = (q_lat @ w_uq).reshape(B, S, MLA_HEADS, QK_NOPE + QK_ROPE)
    q_nope = q[..., :QK_NOPE]
    q_rope = _apply_rope(q[..., QK_NOPE:], cos[:, :, None, :], sin[:, :, None, :])
    kv = (kv_lat @ w_ukv).reshape(B, S, MLA_HEADS, QK_NOPE + V_HEAD)
    k_nope, v = kv[..., :QK_NOPE], kv[..., QK_NOPE:]
    nb = S // Q_BLOCK
    qn_b = q_nope.reshape(B, nb, Q_BLOCK, MLA_HEADS, QK_NOPE).transpose(1, 0, 2, 3, 4)
    qr_b = q_rope.reshape(B, nb, Q_BLOCK, MLA_HEADS, QK_ROPE).transpose(1, 0, 2, 3, 4)
    starts = jnp.arange(nb, dtype=jnp.int32) * Q_BLOCK
    key_pos = jnp.arange(S, dtype=jnp.int32)
    scale = (QK_NOPE + QK_ROPE) ** -0.5

    def block(args):
        qn, qr, start = args
        s = jnp.einsum('bqhd,bkhd->bhqk', qn, k_nope) + jnp.einsum('bqhr,bkr->bhqk', qr, k_rope)
        s = s.astype(jnp.float32) * scale
        q_pos = start + jnp.arange(Q_BLOCK, dtype=jnp.int32)
        s = jnp.where(key_pos[None, :] <= q_pos[:, None], s, -jnp.inf)
        p = jax.nn.softmax(s, axis=-1).astype(v.dtype)
        return jnp.einsum('bhqk,bkhd->bqhd', p, v)

    o = lax.map(block, (qn_b, qr_b, starts))
    o = o.transpose(1, 0, 2, 3, 4).reshape(B, S, MLA_HEADS * V_HEAD)
    return o @ w_o


def _pool_mixer(h, w_pool, pool_scale):
    B, S, D = h.shape
    hf = h.astype(jnp.float32)
    cs = jnp.cumsum(hf, axis=1)
    count = jnp.arange(1, S + 1, dtype=jnp.float32)[None, :, None]
    outs = []
    for gi, w in enumerate(POOL_WINDOWS):
        sl = slice(gi * POOL_GROUP_DIM, (gi + 1) * POOL_GROUP_DIM)
        cg = cs[..., sl]
        shifted = jnp.pad(cg, ((0, 0), (w, 0), (0, 0)))[:, :S]
        mean = (cg - shifted) / jnp.minimum(count, w)
        outs.append(mean - hf[..., sl])
    mixed = jnp.stack(outs, axis=2).astype(h.dtype)
    y = jnp.einsum('bsgc,gcd->bsgd', mixed, w_pool).reshape(B, S, D)
    return y * pool_scale


def _hier_moe(h, w_gr, b_gr, w_er, b_er, w_gate, w_up, w_down):
    B, S, D = h.shape
    N = B * S
    t = h.reshape(N, D)
    g_prob = jax.nn.softmax((t @ w_gr + b_gr).astype(jnp.float32), axis=-1)
    g_p, g_idx = lax.top_k(g_prob, 1)
    e_logits = (t @ w_er + b_er).astype(jnp.float32).reshape(N, MOE_GROUPS, EXPERTS_PER_GROUP)
    sel = jnp.broadcast_to(g_idx[:, :, None], (N, 1, EXPERTS_PER_GROUP))
    e_in_group = jnp.take_along_axis(e_logits, sel, axis=1)[:, 0]
    e_top, e_idx = lax.top_k(e_in_group, MOE_TOP_K)
    e_w = jax.nn.softmax(e_top, axis=-1) * g_p
    expert_id = g_idx * EXPERTS_PER_GROUP + e_idx
    gates = jnp.sum(jax.nn.one_hot(expert_id, N_EXPERTS, dtype=jnp.float32) * e_w[..., None], axis=1)
    out = jnp.zeros((N, D), jnp.float32)
    for e in range(N_EXPERTS):
        he = jax.nn.silu(t @ w_gate[e]) * (t @ w_up[e])
        out = out + gates[:, e:e + 1] * (he @ w_down[e]).astype(jnp.float32)
    return out.astype(h.dtype).reshape(B, S, D)


def setup_inputs(seed: int = 0) -> dict:
    key = jax.random.key(seed)
    ks = jax.random.split(key, 24)
    f32 = jnp.float32

    def nrm(k, shape, scale):
        return jax.random.normal(k, shape, f32) * scale

    offsets = jax.random.randint(ks[2], (BATCH, 1), 0, 1024, dtype=jnp.int32)
    positions = (offsets + jnp.arange(SEQ, dtype=jnp.int32)[None, :]).astype(jnp.int32)
    return {
        'x': nrm(ks[0], (BATCH, SEQ, D_MODEL), 1.0),
        'c': nrm(ks[1], (BATCH, D_MODEL), 1.0),
        'positions': positions,
        'w_mod': nrm(ks[3], (DEPTH, D_MODEL, 6 * D_MODEL), 0.5 * D_MODEL ** -0.5),
        'b_mod': nrm(ks[4], (DEPTH, 6 * D_MODEL), 0.01),
        'ln_g': 1.0 + nrm(ks[5], (DEPTH, 2, D_MODEL), 0.02),
        'ln_b': nrm(ks[6], (DEPTH, 2, D_MODEL), 0.01),
        'w_in': nrm(ks[7], (N_MLA_LAYERS, D_MODEL, MLA_IN), D_MODEL ** -0.5),
        'q_norm_g': 1.0 + nrm(ks[8], (N_MLA_LAYERS, Q_LORA), 0.02),
        'kv_norm_g': 1.0 + nrm(ks[9], (N_MLA_LAYERS, KV_LORA), 0.02),
        'w_uq': nrm(ks[10], (N_MLA_LAYERS, Q_LORA, MLA_HEADS * (QK_NOPE + QK_ROPE)), Q_LORA ** -0.5),
        'w_ukv': nrm(ks[11], (N_MLA_LAYERS, KV_LORA, MLA_HEADS * (QK_NOPE + V_HEAD)), KV_LORA ** -0.5),
        'w_o': nrm(ks[12], (N_MLA_LAYERS, MLA_HEADS * V_HEAD, D_MODEL), DN_BETA * (MLA_HEADS * V_HEAD) ** -0.5),
        'w_pool': nrm(ks[13], (N_POOL_LAYERS, POOL_GROUPS, POOL_GROUP_DIM, POOL_GROUP_DIM), DN_BETA * POOL_GROUP_DIM ** -0.5),
        'pool_scale': 1.0 + nrm(ks[14], (N_POOL_LAYERS, D_MODEL), 0.1),
        'w_group_router': nrm(ks[15], (DEPTH, D_MODEL, MOE_GROUPS), D_MODEL ** -0.5),
        'b_group_router': nrm(ks[16], (DEPTH, MOE_GROUPS), 0.01),
        'w_expert_router': nrm(ks[17], (DEPTH, D_MODEL, N_EXPERTS), D_MODEL ** -0.5),
        'b_expert_router': nrm(ks[18], (DEPTH, N_EXPERTS), 0.01),
        'w_gate': nrm(ks[19], (DEPTH, N_EXPERTS, D_MODEL, EXPERT_FF), D_MODEL ** -0.5),
        'w_up': nrm(ks[20], (DEPTH, N_EXPERTS, D_MODEL, EXPERT_FF), D_MODEL ** -0.5),
        'w_down': nrm(ks[21], (DEPTH, N_EXPERTS, EXPERT_FF, D_MODEL), DN_BETA * EXPERT_FF ** -0.5),
    }


def reference(x, c, positions, w_mod, b_mod, ln_g, ln_b, w_in, q_norm_g, kv_norm_g, w_uq, w_ukv, w_o,
              w_pool, pool_scale, w_group_router, b_group_router, w_expert_router, b_expert_router,
              w_gate, w_up, w_down):
    c_act = jax.nn.silu(c)
    for i in range(DEPTH):
        mod = (c_act @ w_mod[i] + b_mod[i])[:, None, :]
        sh1, sc1, g1, sh2, sc2, g2 = jnp.split(mod, 6, axis=-1)
        j = i // N_MIXERS
        h = x * (1.0 + sc1) + sh1
        if i % N_MIXERS == 0:
            y = _mla(h, positions, w_in[j], q_norm_g[j], kv_norm_g[j], w_uq[j], w_ukv[j], w_o[j])
        else:
            y = _pool_mixer(h, w_pool[j], pool_scale[j])
        x = _layer_norm(DN_ALPHA * x + (1.0 + g1) * y, ln_g[i, 0], ln_b[i, 0])
        h = x * (1.0 + sc2) + sh2
        y = _hier_moe(h, w_group_router[i], b_group_router[i], w_expert_router[i], b_expert_router[i],
                      w_gate[i], w_up[i], w_down[i])
        x = _layer_norm(DN_ALPHA * x + (1.0 + g2) * y, ln_g[i, 1], ln_b[i, 1])
    return x
```

```python
import functools

import numpy as np
import jax
import jax.numpy as jnp
from jax import lax
from jax.experimental import pallas as pl
from jax.experimental.pallas import tpu as pltpu

MLA_HEADS = 16
Q_LORA = 384
KV_LORA = 256
QK_NOPE = 64
QK_ROPE = 32
V_HEAD = 64
ROPE_THETA = 10000.0
POOL_WINDOWS = (2, 4, 8, 16)
MOE_GROUPS = 4
EXPERTS_PER_GROUP = 8
DEPTH = 2
DN_ALPHA = (2.0 * DEPTH) ** 0.25
LN_EPS = 1e-5
RMS_EPS = 1e-6

LANES = 128
SUBLANES = 8
VMEM_LIMIT = 56 * 1024 * 1024

HEAD_W = 128
PAIRS_PER_GROUP = EXPERTS_PER_GROUP * (EXPERTS_PER_GROUP - 1) // 2
N_CLASSES = MOE_GROUPS * PAIRS_PER_GROUP
ROW_TILE = 128
META_ROWS = 16
HALO = 16

f32 = jnp.float32
bf16 = jnp.bfloat16


def _cparams(sem):
    return pltpu.CompilerParams(dimension_semantics=sem, vmem_limit_bytes=VMEM_LIMIT)


def _split_bf16(a):
    hi = a.astype(bf16)
    lo = (a - hi.astype(f32)).astype(bf16)
    return hi, lo


def _dot(a, b):
    return jnp.dot(a, b, preferred_element_type=f32)


def _dot3(a_hi, a_lo, b_hi, b_lo):
    return _dot(a_hi, b_hi) + (_dot(a_lo, b_hi) + _dot(a_hi, b_lo))


def _layer_norm(x, g, b):
    mu = jnp.mean(x, axis=-1, keepdims=True)
    xc = x - mu
    var = jnp.mean(xc * xc, axis=-1, keepdims=True)
    return xc * lax.rsqrt(var + LN_EPS) * g + b


def _mod_kernel(c_ref, w_ref, b_ref, o_ref):
    c = c_ref[...]
    ca = c * (1.0 / (1.0 + jnp.exp(-c)))
    a_hi, a_lo = _split_bf16(ca)
    w_hi, w_lo = _split_bf16(w_ref[...])
    o_ref[...] = _dot3(a_hi, a_lo, w_hi, w_lo) + b_ref[...]


def _modulation(c, w_mod, b_mod):
    B, D = c.shape
    depth = w_mod.shape[0]
    out = pl.pallas_call(
        _mod_kernel,
        grid=(depth, 6),
        in_specs=[
            pl.BlockSpec((B, D), lambda i, j: (0, 0)),
            pl.BlockSpec((None, D, D), lambda i, j: (i, 0, j)),
            pl.BlockSpec((None, None, 1, D), lambda i, j: (i, j, 0, 0)),
        ],
        out_specs=pl.BlockSpec((None, None, B, D), lambda i, j: (i, j, 0, 0)),
        out_shape=jax.ShapeDtypeStruct((depth, 6, B, D), f32),
        compiler_params=_cparams(("arbitrary", "arbitrary")),
        name="adaln_mod",
    )(c, w_mod, b_mod.reshape(depth, 6, 1, D))
    out = jnp.transpose(out, (0, 2, 1, 3))
    return jnp.pad(out, ((0, 0), (0, 0), (0, 2), (0, 0)))


def _mla_proj_kernel(x_ref, pos_ref, mod_ref, win_ref, qg_ref, kg_ref, wuq_ref, wkn_ref, wv_ref,
                     invf_ref, q_ref, k_ref, v_ref):
    x = x_ref[...]
    h = x * (1.0 + mod_ref[1:2, :]) + mod_ref[0:1, :]
    z = _dot(h.astype(bf16), win_ref[...])
    zq = z[:, :Q_LORA]
    zkv = z[:, Q_LORA:Q_LORA + KV_LORA]
    za = z[:, Q_LORA + KV_LORA:Q_LORA + KV_LORA + LANES]
    zb = z[:, Q_LORA + KV_LORA + LANES:]
    q_lat = zq * lax.rsqrt(jnp.mean(zq * zq, axis=-1, keepdims=True) + RMS_EPS) * qg_ref[...]
    kv_lat = zkv * lax.rsqrt(jnp.mean(zkv * zkv, axis=-1, keepdims=True) + RMS_EPS) * kg_ref[...]

    ang = pos_ref[...].astype(f32) * invf_ref[...]
    cs = jnp.cos(ang)
    sn = jnp.sin(ang)
    lane = lax.broadcasted_iota(jnp.int32, cs.shape, 1)
    scale = (QK_NOPE + QK_ROPE) ** -0.5
    tq = scale * jnp.where(lane < QK_NOPE + QK_ROPE, cs, sn)
    kr = za * cs + zb * sn

    q = _dot(q_lat.astype(bf16), wuq_ref[...])
    kv_b = kv_lat.astype(bf16)
    kn = _dot(kv_b, wkn_ref[...])
    for hd in range(MLA_HEADS):
        sl = slice(hd * HEAD_W, (hd + 1) * HEAD_W)
        q_ref[hd] = (q[:, sl] * tq).astype(bf16)
        k_ref[hd] = (kn[:, sl] + kr).astype(bf16)
    v_ref[...] = _dot(kv_b, wv_ref[...]).astype(bf16)


def _mla_proj(x2, pos2, mod, win, qg, kg, wuq, wkn, wv, invf, B, S, tm):
    N, D = x2.shape
    spb = S // tm
    const = lambda shape: pl.BlockSpec(shape, lambda i: (0,) * len(shape))
    return pl.pallas_call(
        _mla_proj_kernel,
        grid=(N // tm,),
        in_specs=[
            pl.BlockSpec((tm, D), lambda i: (i, 0)),
            pl.BlockSpec((tm, 1), lambda i: (i, 0)),
            pl.BlockSpec((None, 8, D), lambda i: (i // spb, 0, 0)),
            const(win.shape), const(qg.shape), const(kg.shape), const(wuq.shape), const(wkn.shape),
            const(wv.shape), const(invf.shape),
        ],
        out_specs=[
            pl.BlockSpec((None, MLA_HEADS, tm, HEAD_W), lambda i: (i // spb, 0, i % spb, 0)),
            pl.BlockSpec((None, MLA_HEADS, tm, HEAD_W), lambda i: (i // spb, 0, i % spb, 0)),
            pl.BlockSpec((tm, MLA_HEADS * V_HEAD), lambda i: (i, 0)),
        ],
        out_shape=[
            jax.ShapeDtypeStruct((B, MLA_HEADS, S, HEAD_W), bf16),
            jax.ShapeDtypeStruct((B, MLA_HEADS, S, HEAD_W), bf16),
            jax.ShapeDtypeStruct((N, MLA_HEADS * V_HEAD), bf16),
        ],
        compiler_params=_cparams(("arbitrary",)),
        name="mla_proj",
    )(x2, pos2, mod, win, qg, kg, wuq, wkn, wv, invf)


def _attn_kernel(q_ref, k_ref, v_ref, o_ref, vx_ref, m_ref, acc_ref, *, tq):
    S = v_ref.shape[0]
    nq = S // tq
    vx_ref[:, :LANES] = v_ref[...]
    vx_ref[:, LANES:] = jnp.ones((S, LANES), bf16)
    row = lax.broadcasted_iota(jnp.int32, (tq, tq), 0)
    col = lax.broadcasted_iota(jnp.int32, (tq, tq), 1)
    reps = tq // LANES

    for hh in range(2):
        def kv_step(q, k0, masked):
            k = k_ref[hh, pl.ds(k0, tq), :]
            s = lax.dot_general(q, k, (((1,), (1,)), ((), ())), preferred_element_type=f32)
            if masked:
                s = jnp.where(col <= row, s, -jnp.inf)
            m_old = m_ref[...]
            m_new = jnp.maximum(m_old, jnp.max(s, axis=-1, keepdims=True))
            alpha = jnp.exp(m_old - m_new)
            p = jnp.exp(s - jnp.concatenate([m_new] * reps, axis=1))
            pv = _dot(p.astype(bf16), vx_ref[pl.ds(k0, tq), :])
            acc_ref[...] = acc_ref[...] * jnp.concatenate([alpha, alpha], axis=1) + pv
            m_ref[...] = m_new

        def q_body(qi, carry):
            q0 = pl.multiple_of(qi * tq, tq)
            q = q_ref[hh, pl.ds(q0, tq), :]
            m_ref[...] = jnp.full(m_ref.shape, -jnp.inf, f32)
            acc_ref[...] = jnp.zeros(acc_ref.shape, f32)

            def full_body(j, c):
                kv_step(q, pl.multiple_of(j * tq, tq), False)
                return c

            lax.fori_loop(0, qi, full_body, 0)
            kv_step(q, q0, True)
            acc = acc_ref[...]
            o = acc[:, :LANES] / acc[:, LANES:]
            lo, hi = hh * V_HEAD, (hh + 1) * V_HEAD
            o_ref[pl.ds(q0, tq), lo:hi] = o[:, lo:hi].astype(bf16)
            return carry

        lax.fori_loop(0, nq, q_body, 0)


def _attention(q, k, v3, tq):
    B, H, S, W = q.shape
    return pl.pallas_call(
        functools.partial(_attn_kernel, tq=tq),
        grid=(B, H // 2),
        in_specs=[
            pl.BlockSpec((None, 2, S, W), lambda b, p: (b, p, 0, 0)),
            pl.BlockSpec((None, 2, S, W), lambda b, p: (b, p, 0, 0)),
            pl.BlockSpec((None, S, LANES), lambda b, p: (b, 0, p)),
        ],
        out_specs=pl.BlockSpec((None, S, LANES), lambda b, p: (b, 0, p)),
        out_shape=jax.ShapeDtypeStruct((B, S, H * V_HEAD), bf16),
        scratch_shapes=[
            pltpu.VMEM((S, 2 * LANES), bf16),
            pltpu.VMEM((tq, LANES), f32),
            pltpu.VMEM((tq, 2 * LANES), f32),
        ],
        compiler_params=_cparams(("arbitrary", "arbitrary")),
        name="mla_attention",
    )(q, k, v3)


def _post_mixer(x, y, mod_ref, ln_ref, wr_hi_ref, wr_lo_ref, br_ref, xo_ref, hm_ref, cls_ref):
    tm, D = x.shape
    x1 = _layer_norm(DN_ALPHA * x + (1.0 + mod_ref[2:3, :]) * y, ln_ref[0:1, :], ln_ref[1:2, :])
    xo_ref[...] = x1
    h2 = x1 * (1.0 + mod_ref[4:5, :]) + mod_ref[3:4, :]
    h_hi, h_lo = _split_bf16(h2)
    logits = _dot3(h_hi, h_lo, wr_hi_ref[...], wr_lo_ref[...]) + br_ref[...]

    lane = lax.broadcasted_iota(jnp.int32, logits.shape, 1).astype(f32)
    neg = -jnp.inf
    far = float(LANES)
    is_g = lane < MOE_GROUPS
    gl = jnp.where(is_g, logits, neg)
    gmax = jnp.max(gl, axis=-1, keepdims=True)
    gidx = jnp.min(jnp.where(gl == gmax, lane, far), axis=-1, keepdims=True)
    g_p = 1.0 / jnp.sum(jnp.where(is_g, jnp.exp(logits - gmax), 0.0), axis=-1, keepdims=True)
    base = MOE_GROUPS + EXPERTS_PER_GROUP * gidx
    el = jnp.where((lane >= base) & (lane < base + EXPERTS_PER_GROUP), logits, neg)
    t1 = jnp.max(el, axis=-1, keepdims=True)
    i1 = jnp.min(jnp.where(el == t1, lane, far), axis=-1, keepdims=True)
    el2 = jnp.where(lane == i1, neg, el)
    t2 = jnp.max(el2, axis=-1, keepdims=True)
    i2 = jnp.min(jnp.where(el2 == t2, lane, far), axis=-1, keepdims=True)
    e = jnp.exp(t2 - t1)
    w1 = g_p / (1.0 + e)
    w2 = g_p * e / (1.0 + e)
    a = i1 - base
    b = i2 - base
    a_first = a < b
    lo = jnp.where(a_first, a, b)
    hi = jnp.where(a_first, b, a)
    w_lo = jnp.where(a_first, w1, w2)
    w_hi = jnp.where(a_first, w2, w1)
    pair = lo * (2 * EXPERTS_PER_GROUP - 1 - lo) * 0.5 + (hi - lo - 1.0)
    cls_ref[...] = (gidx * PAIRS_PER_GROUP + pair).astype(jnp.int32)

    for s in range(D // LANES):
        hm_ref[:, s, :] = h2[:, s * LANES:(s + 1) * LANES]
    hm_ref[:, SUBLANES, :] = jnp.broadcast_to(w_lo, (tm, LANES))
    hm_ref[:, SUBLANES + 1, :] = jnp.broadcast_to(w_hi, (tm, LANES))
    for s in range(SUBLANES + 2, META_ROWS):
        hm_ref[:, s, :] = jnp.zeros((tm, LANES), f32)


def _attn_out_kernel(o_ref, x_ref, mod_ref, wo_ref, ln_ref, wr_hi_ref, wr_lo_ref, br_ref,
                     xo_ref, hm_ref, cls_ref):
    y = _dot(o_ref[...], wo_ref[...])
    _post_mixer(x_ref[...], y, mod_ref, ln_ref, wr_hi_ref, wr_lo_ref, br_ref, xo_ref, hm_ref, cls_ref)


def _post_out_specs(N, D, tm):
    specs = [
        pl.BlockSpec((tm, D), lambda i: (i, 0)),
        pl.BlockSpec((tm, META_ROWS, LANES), lambda i: (i, 0, 0)),
        pl.BlockSpec((tm, 1), lambda i: (i, 0)),
    ]
    shapes = [
        jax.ShapeDtypeStruct((N, D), f32),
        jax.ShapeDtypeStruct((N, META_ROWS, LANES), f32),
        jax.ShapeDtypeStruct((N, 1), jnp.int32),
    ]
    return specs, shapes


def _attn_out(o2, x2, mod, wo, ln, wr_hi, wr_lo, br, S, tm):
    N, D = x2.shape
    spb = S // tm
    const = lambda shape: pl.BlockSpec(shape, lambda i: (0,) * len(shape))
    out_specs, out_shapes = _post_out_specs(N, D, tm)
    return pl.pallas_call(
        _attn_out_kernel,
        grid=(N // tm,),
        in_specs=[
            pl.BlockSpec((tm, D), lambda i: (i, 0)),
            pl.BlockSpec((tm, D), lambda i: (i, 0)),
            pl.BlockSpec((None, 8, D), lambda i: (i // spb, 0, 0)),
            const(wo.shape), const(ln.shape), const(wr_hi.shape), const(wr_lo.shape), const(br.shape),
        ],
        out_specs=out_specs,
        out_shape=out_shapes,
        compiler_params=_cparams(("arbitrary",)),
        name="attn_out_router",
    )(o2, x2, mod, wo, ln, wr_hi, wr_lo, br)


def _pool_kernel(x_ref, y_ref, modp_ref, lnp_ref, mod_ref, wp_ref, ps_ref, ln_ref,
                 wr_hi_ref, wr_lo_ref, br_ref, xo_ref, hm_ref, cls_ref, hb_ref, *, spb):
    tm, D = x_ref.shape
    i = pl.program_id(0)
    t_blk = i % spb
    yprev = jnp.concatenate([y_ref[:, s, :] for s in range(D // LANES)], axis=1)
    x2 = _layer_norm(DN_ALPHA * x_ref[...] + (1.0 + modp_ref[5:6, :]) * yprev,
                     lnp_ref[0:1, :], lnp_ref[1:2, :])
    h = x2 * (1.0 + mod_ref[1:2, :]) + mod_ref[0:1, :]

    @pl.when(t_blk == 0)
    def _():
        hb_ref[0:HALO, :] = jnp.zeros((HALO, D), f32)

    hb_ref[HALO:, :] = h
    t_seq = t_blk * tm + lax.broadcasted_iota(jnp.int32, (tm, 1), 0)
    gd = D // len(POOL_WINDOWS)
    ys = []
    for gi, w in enumerate(POOL_WINDOWS):
        c0, c1 = gi * gd, (gi + 1) * gd
        a = hb_ref[:, c0:c1]
        span = 1
        while span < w:
            a = a[span:, :] + a[:-span, :]
            span *= 2
        tsum = a[HALO - (w - 1):, :]
        cnt = jnp.minimum(t_seq + 1, w).astype(f32)
        mixed = tsum / cnt - h[:, c0:c1]
        ys.append(_dot(mixed.astype(bf16), wp_ref[gi]))
    hb_ref[0:HALO, :] = h[tm - HALO:, :]
    y = jnp.concatenate(ys, axis=1) * ps_ref[...]
    _post_mixer(x2, y, mod_ref, ln_ref, wr_hi_ref, wr_lo_ref, br_ref, xo_ref, hm_ref, cls_ref)


def _pool_layer(x2, yprev, modp, lnp, mod, wp, ps, ln, wr_hi, wr_lo, br, S, tm):
    N, D = x2.shape
    spb = S // tm
    const = lambda shape: pl.BlockSpec(shape, lambda i: (0,) * len(shape))
    out_specs, out_shapes = _post_out_specs(N, D, tm)
    return pl.pallas_call(
        functools.partial(_pool_kernel, spb=spb),
        grid=(N // tm,),
        in_specs=[
            pl.BlockSpec((tm, D), lambda i: (i, 0)),
            pl.BlockSpec((tm, SUBLANES, LANES), lambda i: (i, 0, 0)),
            pl.BlockSpec((None, 8, D), lambda i: (i // spb, 0, 0)),
            const(lnp.shape),
            pl.BlockSpec((None, 8, D), lambda i: (i // spb, 0, 0)),
            const(wp.shape), const(ps.shape), const(ln.shape),
            const(wr_hi.shape), const(wr_lo.shape), const(br.shape),
        ],
        out_specs=out_specs,
        out_shape=out_shapes,
        scratch_shapes=[pltpu.VMEM((HALO + tm, D), f32)],
        compiler_params=_cparams(("arbitrary",)),
        name="pool_mixer_router",
    )(x2, yprev, modp, lnp, mod, wp, ps, ln, wr_hi, wr_lo, br)


def _final_kernel(x_ref, y_ref, mod_ref, ln_ref, o_ref):
    D = x_ref.shape[1]
    y = jnp.concatenate([y_ref[:, s, :] for s in range(D // LANES)], axis=1)
    o_ref[...] = _layer_norm(DN_ALPHA * x_ref[...] + (1.0 + mod_ref[5:6, :]) * y,
                             ln_ref[0:1, :], ln_ref[1:2, :])


def _final_merge(x2, y, mod, ln, S, tm):
    N, D = x2.shape
    spb = S // tm
    return pl.pallas_call(
        _final_kernel,
        grid=(N // tm,),
        in_specs=[
            pl.BlockSpec((tm, D), lambda i: (i, 0)),
            pl.BlockSpec((tm, SUBLANES, LANES), lambda i: (i, 0, 0)),
            pl.BlockSpec((None, 8, D), lambda i: (i // spb, 0, 0)),
            pl.BlockSpec(ln.shape, lambda i: (0, 0)),
        ],
        out_specs=pl.BlockSpec((tm, D), lambda i: (i, 0)),
        out_shape=jax.ShapeDtypeStruct((N, D), f32),
        compiler_params=_cparams(("arbitrary",)),
        name="final_merge",
    )(x2, y, mod, ln)


def _rank_kernel(cls_ref, rank_ref, cnt_ref, carry_ref):
    tr = cls_ref.shape[0]

    @pl.when(pl.program_id(0) == 0)
    def _():
        carry_ref[...] = jnp.zeros(carry_ref.shape, f32)

    lane = lax.broadcasted_iota(jnp.int32, (tr, LANES), 1)
    onehot = (cls_ref[...] == lane)
    oh = onehot.astype(bf16)
    r = lax.broadcasted_iota(jnp.int32, (tr, tr), 0)
    c = lax.broadcasted_iota(jnp.int32, (tr, tr), 1)
    earlier = (c < r).astype(bf16)
    before = _dot(earlier, oh) + carry_ref[0:1, :]
    rank_ref[...] = jnp.sum(jnp.where(onehot, before, 0.0), axis=-1, keepdims=True).astype(jnp.int32)
    total = carry_ref[0:1, :] + jnp.sum(oh.astype(f32), axis=0, keepdims=True)
    carry_ref[...] = jnp.broadcast_to(total, carry_ref.shape)
    cnt_ref[...] = jnp.broadcast_to(total, cnt_ref.shape).astype(jnp.int32)


def _class_ranks(cls, tr):
    N = cls.shape[0]
    return pl.pallas_call(
        _rank_kernel,
        grid=(N // tr,),
        in_specs=[pl.BlockSpec((tr, 1), lambda i: (i, 0))],
        out_specs=[
            pl.BlockSpec((tr, 1), lambda i: (i, 0)),
            pl.BlockSpec((SUBLANES, LANES), lambda i: (0, 0)),
        ],
        out_shape=[
            jax.ShapeDtypeStruct((N, 1), jnp.int32),
            jax.ShapeDtypeStruct((SUBLANES, LANES), jnp.int32),
        ],
        scratch_shapes=[pltpu.VMEM((SUBLANES, LANES), f32)],
        compiler_params=_cparams(("arbitrary",)),
        name="class_ranks",
    )(cls)


def _permute_kernel(off_ref, cls_ref, rank_ref, src_ref, dst_ref, sem, *, chunk, to_sorted):
    base = pl.program_id(0) * chunk

    def row_copy(t):
        slot = off_ref[cls_ref[t]] + rank_ref[t]
        if to_sorted:
            return pltpu.make_async_copy(src_ref.at[base + t], dst_ref.at[slot], sem)
        return pltpu.make_async_copy(src_ref.at[slot], dst_ref.at[base + t], sem)

    def issue(t, c):
        row_copy(t).start()
        return c

    lax.fori_loop(0, chunk, issue, 0)
    pltpu.make_async_copy(src_ref.at[pl.ds(0, chunk)], dst_ref.at[pl.ds(0, chunk)], sem).wait()


def _permute_rows(off, cls1, rank1, src, n_dst, chunk, to_sorted):
    N = cls1.shape[0]
    rows, lanes = src.shape[1], src.shape[2]
    return pl.pallas_call(
        functools.partial(_permute_kernel, chunk=chunk, to_sorted=to_sorted),
        grid_spec=pltpu.PrefetchScalarGridSpec(
            num_scalar_prefetch=1,
            grid=(N // chunk,),
            in_specs=[
                pl.BlockSpec((chunk,), lambda i, off: (i,), memory_space=pltpu.SMEM),
                pl.BlockSpec((chunk,), lambda i, off: (i,), memory_space=pltpu.SMEM),
                pl.BlockSpec(memory_space=pl.ANY),
            ],
            out_specs=pl.BlockSpec(memory_space=pl.ANY),
            scratch_shapes=[pltpu.SemaphoreType.DMA(())],
        ),
        out_shape=jax.ShapeDtypeStruct((n_dst, rows, lanes), src.dtype),
        compiler_params=_cparams(("arbitrary",)),
        name="dispatch_rows" if to_sorted else "collect_rows",
    )(off, cls1, rank1, src)


def _moe_kernel(te1_ref, te2_ref, nused_ref, hs_ref, wgu1_ref, wd1_ref, wgu2_ref, wd2_ref, ys_ref):
    D = wd1_ref.shape[1]
    ff = wd1_ref.shape[0]

    @pl.when(pl.program_id(0) < nused_ref[0])
    def _():
        h = jnp.concatenate([hs_ref[:, s, :] for s in range(D // LANES)], axis=1).astype(bf16)

        def expert(wgu_ref, wd_ref, w):
            gu = _dot(h, wgu_ref[...])
            g, u = gu[:, :ff], gu[:, ff:]
            he = g * (1.0 / (1.0 + jnp.exp(-g))) * u
            he = he * jnp.concatenate([w] * (ff // LANES), axis=1)
            return _dot(he.astype(bf16), wd_ref[...])

        y = expert(wgu1_ref, wd1_ref, hs_ref[:, SUBLANES, :]) + \
            expert(wgu2_ref, wd2_ref, hs_ref[:, SUBLANES + 1, :])
        for s in range(D // LANES):
            ys_ref[:, s, :] = y[:, s * LANES:(s + 1) * LANES]


def _moe(te1, te2, nused, hs, wgu, wd):
    n_slots = hs.shape[0]
    E, D, ff2 = wgu.shape
    ff = ff2 // 2
    n_tiles = n_slots // ROW_TILE

    def row_map(i, te1, te2, nused):
        return (jnp.minimum(i, nused[0] - 1), 0, 0)

    return pl.pallas_call(
        _moe_kernel,
        grid_spec=pltpu.PrefetchScalarGridSpec(
            num_scalar_prefetch=3,
            grid=(n_tiles,),
            in_specs=[
                pl.BlockSpec((ROW_TILE, META_ROWS, LANES), row_map),
                pl.BlockSpec((None, D, ff2), lambda i, te1, te2, nused: (te1[i], 0, 0)),
                pl.BlockSpec((None, ff, D), lambda i, te1, te2, nused: (te1[i], 0, 0)),
                pl.BlockSpec((None, D, ff2), lambda i, te1, te2, nused: (te2[i], 0, 0)),
                pl.BlockSpec((None, ff, D), lambda i, te1, te2, nused: (te2[i], 0, 0)),
            ],
            out_specs=pl.BlockSpec((ROW_TILE, SUBLANES, LANES), row_map),
        ),
        out_shape=jax.ShapeDtypeStruct((n_slots, SUBLANES, LANES), f32),
        compiler_params=_cparams(("arbitrary",)),
        name="pair_moe",
    )(te1, te2, nused, hs, wgu, wd, wgu, wd)


def _pair_tables():
    lo = np.zeros((N_CLASSES,), np.int32)
    hi = np.zeros((N_CLASSES,), np.int32)
    for g in range(MOE_GROUPS):
        p = 0
        for a in range(EXPERTS_PER_GROUP):
            for b in range(a + 1, EXPERTS_PER_GROUP):
                lo[g * PAIRS_PER_GROUP + p] = g * EXPERTS_PER_GROUP + a
                hi[g * PAIRS_PER_GROUP + p] = g * EXPERTS_PER_GROUP + b
                p += 1
    return lo, hi


def _moe_layer(hm, cls, wgu, wd, chunk, tr):
    N = hm.shape[0]
    n_tiles = N // ROW_TILE + N_CLASSES
    rank, cnt = _class_ranks(cls, tr)
    counts = cnt[0, :N_CLASSES]
    tiles = (counts + ROW_TILE - 1) // ROW_TILE
    tile_end = jnp.cumsum(tiles)
    off = jnp.zeros((LANES,), jnp.int32).at[:N_CLASSES].set((tile_end - tiles) * ROW_TILE)
    nused = tile_end[-1:]
    tile_cls = jnp.searchsorted(tile_end, jnp.minimum(jnp.arange(n_tiles), nused[0] - 1), side="right")
    tile_cls = jnp.minimum(tile_cls, N_CLASSES - 1).astype(jnp.int32)
    lo_tab, hi_tab = _pair_tables()
    te1 = jnp.asarray(lo_tab)[tile_cls]
    te2 = jnp.asarray(hi_tab)[tile_cls]

    cls1 = cls.reshape(N)
    rank1 = rank.reshape(N)
    hs = _permute_rows(off, cls1, rank1, hm, n_tiles * ROW_TILE, chunk, True)
    ys = _moe(te1, te2, nused.astype(jnp.int32), hs, wgu, wd)
    return _permute_rows(off, cls1, rank1, ys, N, chunk, False)


def _swap_rope(w):
    half = QK_ROPE // 2
    return jnp.concatenate([-w[..., half:], w[..., :half]], axis=-1)


def _prep_mla_weights(w_in, w_uq, w_ukv):
    D = w_in.shape[0]
    w_kr = w_in[:, Q_LORA + KV_LORA:]
    w_ks = _swap_rope(w_kr)
    zeros = jnp.zeros((D, QK_NOPE), w_in.dtype)
    win = jnp.concatenate([w_in[:, :Q_LORA + KV_LORA], zeros, w_kr, w_kr, zeros, w_ks, w_ks], axis=1)
    uq = w_uq.reshape(Q_LORA, MLA_HEADS, QK_NOPE + QK_ROPE)
    rope = uq[..., QK_NOPE:]
    wuq = jnp.concatenate([uq[..., :QK_NOPE], rope, _swap_rope(rope)], axis=-1).reshape(Q_LORA, -1)
    ukv = w_ukv.reshape(KV_LORA, MLA_HEADS, QK_NOPE + V_HEAD)
    wkn = jnp.concatenate([ukv[..., :QK_NOPE], jnp.zeros((KV_LORA, MLA_HEADS, HEAD_W - QK_NOPE), w_ukv.dtype)],
                          axis=-1).reshape(KV_LORA, -1)
    wv = ukv[..., QK_NOPE:].reshape(KV_LORA, -1)
    return win.astype(bf16), wuq.astype(bf16), wkn.astype(bf16), wv.astype(bf16)


def _rope_inv_freq():
    inv = 1.0 / (ROPE_THETA ** (np.arange(0, QK_ROPE, 2, dtype=np.float32) / QK_ROPE))
    inv2 = np.concatenate([inv, inv]).astype(np.float32)
    return np.concatenate([np.zeros((QK_NOPE,), np.float32), inv2, inv2]).reshape(1, LANES)


def _prep_router(w_gr, b_gr, w_er, b_er):
    D = w_gr.shape[0]
    n = MOE_GROUPS + MOE_GROUPS * EXPERTS_PER_GROUP
    w = jnp.concatenate([w_gr, w_er, jnp.zeros((D, LANES - n), f32)], axis=1)
    b = jnp.concatenate([b_gr, b_er, jnp.zeros((LANES - n,), f32)]).reshape(1, LANES)
    hi, lo = _split_bf16(w)
    return hi, lo, b


def kernel(x, c, positions, w_mod, b_mod, ln_g, ln_b, w_in, q_norm_g, kv_norm_g, w_uq, w_ukv, w_o, w_pool,
           pool_scale, w_group_router, b_group_router, w_expert_router, b_expert_router, w_gate, w_up, w_down):
    B, S, D = x.shape
    N = B * S
    tm = min(256, S)
    tq = min(512, S)
    tr = min(512, N)
    chunk = min(2048, N)

    mod = _modulation(c, w_mod, b_mod)
    ln = jnp.stack([ln_g, ln_b], axis=2)
    x2 = x.reshape(N, D)

    win, wuq, wkn, wv = _prep_mla_weights(w_in[0], w_uq[0], w_ukv[0])
    q, k, v = _mla_proj(x2, positions.reshape(N, 1), mod[0], win, q_norm_g[0].reshape(1, -1),
                        kv_norm_g[0].reshape(1, -1), wuq, wkn, wv, jnp.asarray(_rope_inv_freq()), B, S, tm)
    o = _attention(q, k, v.reshape(B, S, MLA_HEADS * V_HEAD), tq)
    wr_hi, wr_lo, br = _prep_router(w_group_router[0], b_group_router[0], w_expert_router[0], b_expert_router[0])
    x1, hm, cls = _attn_out(o.reshape(N, D), x2, mod[0], w_o[0].astype(bf16), ln[0, 0], wr_hi, wr_lo, br, S, tm)
    wgu = jnp.concatenate([w_gate[0], w_up[0]], axis=-1).astype(bf16)
    y0 = _moe_layer(hm, cls, wgu, w_down[0].astype(bf16), chunk, tr)

    wr_hi, wr_lo, br = _prep_router(w_group_router[1], b_group_router[1], w_expert_router[1], b_expert_router[1])
    x3, hm, cls = _pool_layer(x1, y0, mod[0], ln[0, 1], mod[1], w_pool[0].astype(bf16),
                              pool_scale[0].reshape(1, D), ln[1, 0], wr_hi, wr_lo, br, S, tm)
    wgu = jnp.concatenate([w_gate[1], w_up[1]], axis=-1).astype(bf16)
    y1 = _moe_layer(hm, cls, wgu, w_down[1].astype(bf16), chunk, tr)

    out = _final_merge(x3, y1, mod[1], ln[1, 1], S, tm)
    return out.reshape(B, S, D)
```

```python
import functools

import numpy as np
import jax
import jax.numpy as jnp
from jax import lax
from jax.experimental import pallas as pl
from jax.experimental.pallas import tpu as pltpu

MLA_HEADS = 16
Q_LORA = 384
KV_LORA = 256
QK_NOPE = 64
QK_ROPE = 32
V_HEAD = 64
ROPE_THETA = 10000.0
POOL_WINDOWS = (2, 4, 8, 16)
MOE_GROUPS = 4
EXPERTS_PER_GROUP = 8
DEPTH = 2
DN_ALPHA = (2.0 * DEPTH) ** 0.25
LN_EPS = 1e-5
RMS_EPS = 1e-6

LANES = 128
SUBLANES = 8
VMEM_LIMIT = 56 * 1024 * 1024

HEAD_W = 128
PAIRS_PER_GROUP = EXPERTS_PER_GROUP * (EXPERTS_PER_GROUP - 1) // 2
N_CLASSES = MOE_GROUPS * PAIRS_PER_GROUP
ROW_TILE = 128
META_ROWS = 16
HALO = 16

f32 = jnp.float32
bf16 = jnp.bfloat16


def _cparams(sem):
    return pltpu.CompilerParams(dimension_semantics=sem, vmem_limit_bytes=VMEM_LIMIT)


def _split_bf16(a):
    hi = a.astype(bf16)
    lo = (a - hi.astype(f32)).astype(bf16)
    return hi, lo


def _dot(a, b):
    return jnp.dot(a, b, preferred_element_type=f32)


def _dot3(a_hi, a_lo, b_hi, b_lo):
    return _dot(a_hi, b_hi) + (_dot(a_lo, b_hi) + _dot(a_hi, b_lo))


def _layer_norm(x, g, b):
    mu = jnp.mean(x, axis=-1, keepdims=True)
    xc = x - mu
    var = jnp.mean(xc * xc, axis=-1, keepdims=True)
    return xc * lax.rsqrt(var + LN_EPS) * g + b


def _mod_kernel(c_ref, w_ref, b_ref, o_ref):
    c = c_ref[...]
    ca = c * (1.0 / (1.0 + jnp.exp(-c)))
    a_hi, a_lo = _split_bf16(ca)
    w_hi, w_lo = _split_bf16(w_ref[...])
    o_ref[...] = _dot3(a_hi, a_lo, w_hi, w_lo) + b_ref[...]


def _modulation(c, w_mod, b_mod):
    B, D = c.shape
    depth = w_mod.shape[0]
    out = pl.pallas_call(
        _mod_kernel,
        grid=(depth, 6),
        in_specs=[
            pl.BlockSpec((B, D), lambda i, j: (0, 0)),
            pl.BlockSpec((None, D, D), lambda i, j: (i, 0, j)),
            pl.BlockSpec((None, None, 1, D), lambda i, j: (i, j, 0, 0)),
        ],
        out_specs=pl.BlockSpec((None, None, B, D), lambda i, j: (i, j, 0, 0)),
        out_shape=jax.ShapeDtypeStruct((depth, 6, B, D), f32),
        compiler_params=_cparams(("arbitrary", "arbitrary")),
        name="adaln_mod",
    )(c, w_mod, b_mod.reshape(depth, 6, 1, D))
    out = jnp.transpose(out, (0, 2, 1, 3))
    return jnp.pad(out, ((0, 0), (0, 0), (0, 2), (0, 0)))


def _mla_proj_kernel(x_ref, pos_ref, mod_ref, win_ref, qg_ref, kg_ref, wuq_ref, wkn_ref, wv_ref,
                     invf_ref, q_ref, k_ref, v_ref):
    x = x_ref[...]
    h = x * (1.0 + mod_ref[1:2, :]) + mod_ref[0:1, :]
    z = _dot(h.astype(bf16), win_ref[...])
    zq = z[:, :Q_LORA]
    zkv = z[:, Q_LORA:Q_LORA + KV_LORA]
    za = z[:, Q_LORA + KV_LORA:Q_LORA + KV_LORA + LANES]
    zb = z[:, Q_LORA + KV_LORA + LANES:]
    q_lat = zq * lax.rsqrt(jnp.mean(zq * zq, axis=-1, keepdims=True) + RMS_EPS) * qg_ref[...]
    kv_lat = zkv * lax.rsqrt(jnp.mean(zkv * zkv, axis=-1, keepdims=True) + RMS_EPS) * kg_ref[...]

    ang = pos_ref[...].astype(f32) * invf_ref[...]
    cs = jnp.cos(ang)
    sn = jnp.sin(ang)
    lane = lax.broadcasted_iota(jnp.int32, cs.shape, 1)
    scale = (QK_NOPE + QK_ROPE) ** -0.5
    tq = scale * jnp.where(lane < QK_NOPE + QK_ROPE, cs, sn)
    kr = za * cs + zb * sn

    q = _dot(q_lat.astype(bf16), wuq_ref[...])
    kv_b = kv_lat.astype(bf16)
    kn = _dot(kv_b, wkn_ref[...])
    for hd in range(MLA_HEADS):
        sl = slice(hd * HEAD_W, (hd + 1) * HEAD_W)
        q_ref[hd] = (q[:, sl] * tq).astype(bf16)
        k_ref[hd] = (kn[:, sl] + kr).astype(bf16)
    v_ref[...] = _dot(kv_b, wv_ref[...]).astype(bf16)


def _mla_proj(x2, pos2, mod, win, qg, kg, wuq, wkn, wv, invf, B, S, tm):
    N, D = x2.shape
    spb = S // tm
    const = lambda shape: pl.BlockSpec(shape, lambda i: (0,) * len(shape))
    return pl.pallas_call(
        _mla_proj_kernel,
        grid=(N // tm,),
        in_specs=[
            pl.BlockSpec((tm, D), lambda i: (i, 0)),
            pl.BlockSpec((tm, 1), lambda i: (i, 0)),
            pl.BlockSpec((None, 8, D), lambda i: (i // spb, 0, 0)),
            const(win.shape), const(qg.shape), const(kg.shape), const(wuq.shape), const(wkn.shape),
            const(wv.shape), const(invf.shape),
        ],
        out_specs=[
            pl.BlockSpec((None, MLA_HEADS, tm, HEAD_W), lambda i: (i // spb, 0, i % spb, 0)),
            pl.BlockSpec((None, MLA_HEADS, tm, HEAD_W), lambda i: (i // spb, 0, i % spb, 0)),
            pl.BlockSpec((tm, MLA_HEADS * V_HEAD), lambda i: (i, 0)),
        ],
        out_shape=[
            jax.ShapeDtypeStruct((B, MLA_HEADS, S, HEAD_W), bf16),
            jax.ShapeDtypeStruct((B, MLA_HEADS, S, HEAD_W), bf16),
            jax.ShapeDtypeStruct((N, MLA_HEADS * V_HEAD), bf16),
        ],
        compiler_params=_cparams(("arbitrary",)),
        name="mla_proj",
    )(x2, pos2, mod, win, qg, kg, wuq, wkn, wv, invf)


def _attn_kernel(q_ref, k_ref, v_ref, o_ref, vx_ref, m_ref, acc_ref, *, tq):
    S = v_ref.shape[0]
    nq = S // tq
    vx_ref[:, :LANES] = v_ref[...]
    vx_ref[:, LANES:] = jnp.ones((S, LANES), bf16)
    row = lax.broadcasted_iota(jnp.int32, (tq, tq), 0)
    col = lax.broadcasted_iota(jnp.int32, (tq, tq), 1)
    reps = tq // LANES

    for hh in range(2):
        def kv_step(q, k0, masked):
            k = k_ref[hh, pl.ds(k0, tq), :]
            s = lax.dot_general(q, k, (((1,), (1,)), ((), ())), preferred_element_type=f32)
            if masked:
                s = jnp.where(col <= row, s, -jnp.inf)
            m_old = m_ref[...]
            m_new = jnp.maximum(m_old, jnp.max(s, axis=-1, keepdims=True))
            alpha = jnp.exp(m_old - m_new)
            p = jnp.exp(s - jnp.concatenate([m_new] * reps, axis=1))
            pv = _dot(p.astype(bf16), vx_ref[pl.ds(k0, tq), :])
            acc_ref[...] = acc_ref[...] * jnp.concatenate([alpha, alpha], axis=1) + pv
            m_ref[...] = m_new

        def q_body(qi, carry):
            q0 = pl.multiple_of(qi * tq, tq)
            q = q_ref[hh, pl.ds(q0, tq), :]
            m_ref[...] = jnp.full(m_ref.shape, -jnp.inf, f32)
            acc_ref[...] = jnp.zeros(acc_ref.shape, f32)

            def full_body(j, c):
                kv_step(q, pl.multiple_of(j * tq, tq), False)
                return c

            lax.fori_loop(0, qi, full_body, 0)
            kv_step(q, q0, True)
            acc = acc_ref[...]
            o = acc[:, :LANES] / acc[:, LANES:]
            lo, hi = hh * V_HEAD, (hh + 1) * V_HEAD
            o_ref[pl.ds(q0, tq), lo:hi] = o[:, lo:hi].astype(bf16)
            return carry

        lax.fori_loop(0, nq, q_body, 0)


def _attention(q, k, v3, tq):
    B, H, S, W = q.shape
    return pl.pallas_call(
        functools.partial(_attn_kernel, tq=tq),
        grid=(B, H // 2),
        in_specs=[
            pl.BlockSpec((None, 2, S, W), lambda b, p: (b, p, 0, 0)),
            pl.BlockSpec((None, 2, S, W), lambda b, p: (b, p, 0, 0)),
            pl.BlockSpec((None, S, LANES), lambda b, p: (b, 0, p)),
        ],
        out_specs=pl.BlockSpec((None, S, LANES), lambda b, p: (b, 0, p)),
        out_shape=jax.ShapeDtypeStruct((B, S, H * V_HEAD), bf16),
        scratch_shapes=[
            pltpu.VMEM((S, 2 * LANES), bf16),
            pltpu.VMEM((tq, LANES), f32),
            pltpu.VMEM((tq, 2 * LANES), f32),
        ],
        compiler_params=_cparams(("arbitrary", "arbitrary")),
        name="mla_attention",
    )(q, k, v3)


def _post_mixer(x, y, mod_ref, ln_ref, wr_hi_ref, wr_lo_ref, br_ref, xo_ref, hm_ref, cls_ref):
    tm, D = x.shape
    x1 = _layer_norm(DN_ALPHA * x + (1.0 + mod_ref[2:3, :]) * y, ln_ref[0:1, :], ln_ref[1:2, :])
    xo_ref[...] = x1
    h2 = x1 * (1.0 + mod_ref[4:5, :]) + mod_ref[3:4, :]
    h_hi, h_lo = _split_bf16(h2)
    logits = _dot3(h_hi, h_lo, wr_hi_ref[...], wr_lo_ref[...]) + br_ref[...]

    lane = lax.broadcasted_iota(jnp.int32, logits.shape, 1).astype(f32)
    neg = -jnp.inf
    far = float(LANES)
    is_g = lane < MOE_GROUPS
    gl = jnp.where(is_g, logits, neg)
    gmax = jnp.max(gl, axis=-1, keepdims=True)
    gidx = jnp.min(jnp.where(gl == gmax, lane, far), axis=-1, keepdims=True)
    g_p = 1.0 / jnp.sum(jnp.where(is_g, jnp.exp(logits - gmax), 0.0), axis=-1, keepdims=True)
    base = MOE_GROUPS + EXPERTS_PER_GROUP * gidx
    el = jnp.where((lane >= base) & (lane < base + EXPERTS_PER_GROUP), logits, neg)
    t1 = jnp.max(el, axis=-1, keepdims=True)
    i1 = jnp.min(jnp.where(el == t1, lane, far), axis=-1, keepdims=True)
    el2 = jnp.where(lane == i1, neg, el)
    t2 = jnp.max(el2, axis=-1, keepdims=True)
    i2 = jnp.min(jnp.where(el2 == t2, lane, far), axis=-1, keepdims=True)
    e = jnp.exp(t2 - t1)
    w1 = g_p / (1.0 + e)
    w2 = g_p * e / (1.0 + e)
    a = i1 - base
    b = i2 - base
    a_first = a < b
    lo = jnp.where(a_first, a, b)
    hi = jnp.where(a_first, b, a)
    w_lo = jnp.where(a_first, w1, w2)
    w_hi = jnp.where(a_first, w2, w1)
    pair = lo * (2 * EXPERTS_PER_GROUP - 1 - lo) * 0.5 + (hi - lo - 1.0)
    cls_ref[...] = (gidx * PAIRS_PER_GROUP + pair).astype(jnp.int32)

    for s in range(D // LANES):
        hm_ref[:, s, :] = h2[:, s * LANES:(s + 1) * LANES]
    hm_ref[:, SUBLANES, :] = jnp.broadcast_to(w_lo, (tm, LANES))
    hm_ref[:, SUBLANES + 1, :] = jnp.broadcast_to(w_hi, (tm, LANES))
    for s in range(SUBLANES + 2, META_ROWS):
        hm_ref[:, s, :] = jnp.zeros((tm, LANES), f32)


def _attn_out_kernel(o_ref, x_ref, mod_ref, wo_ref, ln_ref, wr_hi_ref, wr_lo_ref, br_ref,
                     xo_ref, hm_ref, cls_ref):
    y = _dot(o_ref[...], wo_ref[...])
    _post_mixer(x_ref[...], y, mod_ref, ln_ref, wr_hi_ref, wr_lo_ref, br_ref, xo_ref, hm_ref, cls_ref)


def _post_out_specs(N, D, tm):
    specs = [
        pl.BlockSpec((tm, D), lambda i: (i, 0)),
        pl.BlockSpec((tm, META_ROWS, LANES), lambda i: (i, 0, 0)),
        pl.BlockSpec((tm, 1), lambda i: (i, 0)),
    ]
    shapes = [
        jax.ShapeDtypeStruct((N, D), f32),
        jax.ShapeDtypeStruct((N, META_ROWS, LANES), f32),
        jax.ShapeDtypeStruct((N, 1), jnp.int32),
    ]
    return specs, shapes


def _attn_out(o2, x2, mod, wo, ln, wr_hi, wr_lo, br, S, tm):
    N, D = x2.shape
    spb = S // tm
    const = lambda shape: pl.BlockSpec(shape, lambda i: (0,) * len(shape))
    out_specs, out_shapes = _post_out_specs(N, D, tm)
    return pl.pallas_call(
        _attn_out_kernel,
        grid=(N // tm,),
        in_specs=[
            pl.BlockSpec((tm, D), lambda i: (i, 0)),
            pl.BlockSpec((tm, D), lambda i: (i, 0)),
            pl.BlockSpec((None, 8, D), lambda i: (i // spb, 0, 0)),
            const(wo.shape), const(ln.shape), const(wr_hi.shape), const(wr_lo.shape), const(br.shape),
        ],
        out_specs=out_specs,
        out_shape=out_shapes,
        compiler_params=_cparams(("arbitrary",)),
        name="attn_out_router",
    )(o2, x2, mod, wo, ln, wr_hi, wr_lo, br)


def _pool_kernel(x_ref, y_ref, modp_ref, lnp_ref, mod_ref, wp_ref, ps_ref, ln_ref,
                 wr_hi_ref, wr_lo_ref, br_ref, xo_ref, hm_ref, cls_ref, hb_ref, *, spb):
    tm, D = x_ref.shape
    i = pl.program_id(0)
    t_blk = i % spb
    yprev = jnp.concatenate([y_ref[:, s, :] for s in range(D // LANES)], axis=1)
    x2 = _layer_norm(DN_ALPHA * x_ref[...] + (1.0 + modp_ref[5:6, :]) * yprev,
                     lnp_ref[0:1, :], lnp_ref[1:2, :])
    h = x2 * (1.0 + mod_ref[1:2, :]) + mod_ref[0:1, :]

    @pl.when(t_blk == 0)
    def _():
        hb_ref[0:HALO, :] = jnp.zeros((HALO, D), f32)

    hb_ref[HALO:, :] = h
    t_seq = t_blk * tm + lax.broadcasted_iota(jnp.int32, (tm, 1), 0)
    gd = D // len(POOL_WINDOWS)
    ys = []
    for gi, w in enumerate(POOL_WINDOWS):
        c0, c1 = gi * gd, (gi + 1) * gd
        a = hb_ref[:, c0:c1]
        span = 1
        while span < w:
            a = a[span:, :] + a[:-span, :]
            span *= 2
        tsum = a[HALO - (w - 1):, :]
        cnt = jnp.minimum(t_seq + 1, w).astype(f32)
        mixed = tsum / cnt - h[:, c0:c1]
        ys.append(_dot(mixed.astype(bf16), wp_ref[gi]))
    hb_ref[0:HALO, :] = h[tm - HALO:, :]
    y = jnp.concatenate(ys, axis=1) * ps_ref[...]
    _post_mixer(x2, y, mod_ref, ln_ref, wr_hi_ref, wr_lo_ref, br_ref, xo_ref, hm_ref, cls_ref)


def _pool_layer(x2, yprev, modp, lnp, mod, wp, ps, ln, wr_hi, wr_lo, br, S, tm):
    N, D = x2.shape
    spb = S // tm
    const = lambda shape: pl.BlockSpec(shape, lambda i: (0,) * len(shape))
    out_specs, out_shapes = _post_out_specs(N, D, tm)
    return pl.pallas_call(
        functools.partial(_pool_kernel, spb=spb),
        grid=(N // tm,),
        in_specs=[
            pl.BlockSpec((tm, D), lambda i: (i, 0)),
            pl.BlockSpec((tm, SUBLANES, LANES), lambda i: (i, 0, 0)),
            pl.BlockSpec((None, 8, D), lambda i: (i // spb, 0, 0)),
            const(lnp.shape),
            pl.BlockSpec((None, 8, D), lambda i: (i // spb, 0, 0)),
            const(wp.shape), const(ps.shape), const(ln.shape),
            const(wr_hi.shape), const(wr_lo.shape), const(br.shape),
        ],
        out_specs=out_specs,
        out_shape=out_shapes,
        scratch_shapes=[pltpu.VMEM((HALO + tm, D), f32)],
        compiler_params=_cparams(("arbitrary",)),
        name="pool_mixer_router",
    )(x2, yprev, modp, lnp, mod, wp, ps, ln, wr_hi, wr_lo, br)


def _final_kernel(x_ref, y_ref, mod_ref, ln_ref, o_ref):
    D = x_ref.shape[1]
    y = jnp.concatenate([y_ref[:, s, :] for s in range(D // LANES)], axis=1)
    o_ref[...] = _layer_norm(DN_ALPHA * x_ref[...] + (1.0 + mod_ref[5:6, :]) * y,
                             ln_ref[0:1, :], ln_ref[1:2, :])


def _final_merge(x2, y, mod, ln, S, tm):
    N, D = x2.shape
    spb = S // tm
    return pl.pallas_call(
        _final_kernel,
        grid=(N // tm,),
        in_specs=[
            pl.BlockSpec((tm, D), lambda i: (i, 0)),
            pl.BlockSpec((tm, SUBLANES, LANES), lambda i: (i, 0, 0)),
            pl.BlockSpec((None, 8, D), lambda i: (i // spb, 0, 0)),
            pl.BlockSpec(ln.shape, lambda i: (0, 0)),
        ],
        out_specs=pl.BlockSpec((tm, D), lambda i: (i, 0)),
        out_shape=jax.ShapeDtypeStruct((N, D), f32),
        compiler_params=_cparams(("arbitrary",)),
        name="final_merge",
    )(x2, y, mod, ln)


def _rank_kernel(cls_ref, rank_ref, cnt_ref, carry_ref):
    tr = cls_ref.shape[0]

    @pl.when(pl.program_id(0) == 0)
    def _():
        carry_ref[...] = jnp.zeros(carry_ref.shape, f32)

    lane = lax.broadcasted_iota(jnp.int32, (tr, LANES), 1)
    onehot = (cls_ref[...] == lane)
    oh = onehot.astype(bf16)
    r = lax.broadcasted_iota(jnp.int32, (tr, tr), 0)
    c = lax.broadcasted_iota(jnp.int32, (tr, tr), 1)
    earlier = (c < r).astype(bf16)
    before = _dot(earlier, oh) + carry_ref[0:1, :]
    rank_ref[...] = jnp.sum(jnp.where(onehot, before, 0.0), axis=-1, keepdims=True).astype(jnp.int32)
    total = carry_ref[0:1, :] + jnp.sum(oh.astype(f32), axis=0, keepdims=True)
    carry_ref[...] = jnp.broadcast_to(total, carry_ref.shape)
    cnt_ref[...] = jnp.broadcast_to(total, cnt_ref.shape).astype(jnp.int32)


def _class_ranks(cls, tr):
    N = cls.shape[0]
    return pl.pallas_call(
        _rank_kernel,
        grid=(N // tr,),
        in_specs=[pl.BlockSpec((tr, 1), lambda i: (i, 0))],
        out_specs=[
            pl.BlockSpec((tr, 1), lambda i: (i, 0)),
            pl.BlockSpec((SUBLANES, LANES), lambda i: (0, 0)),
        ],
        out_shape=[
            jax.ShapeDtypeStruct((N, 1), jnp.int32),
            jax.ShapeDtypeStruct((SUBLANES, LANES), jnp.int32),
        ],
        scratch_shapes=[pltpu.VMEM((SUBLANES, LANES), f32)],
        compiler_params=_cparams(("arbitrary",)),
        name="class_ranks",
    )(cls)


def _slot_token_kernel(off_ref, cls_ref, rank_ref, inv_ref, *, chunk):
    base = pl.program_id(0) * chunk

    @pl.when(pl.program_id(0) == 0)
    def _():
        def clear(s, c):
            inv_ref[s] = 0
            return c
        lax.fori_loop(0, inv_ref.shape[0], clear, 0, unroll=8)

    def body(t, c):
        inv_ref[off_ref[cls_ref[t]] + rank_ref[t]] = base + t
        return c

    lax.fori_loop(0, chunk, body, 0, unroll=8)


def _slot_tokens(off, cls1, rank1, n_slots, chunk):
    N = cls1.shape[0]
    return pl.pallas_call(
        functools.partial(_slot_token_kernel, chunk=chunk),
        grid_spec=pltpu.PrefetchScalarGridSpec(
            num_scalar_prefetch=1,
            grid=(N // chunk,),
            in_specs=[
                pl.BlockSpec((chunk,), lambda i, off: (i,), memory_space=pltpu.SMEM),
                pl.BlockSpec((chunk,), lambda i, off: (i,), memory_space=pltpu.SMEM),
            ],
            out_specs=pl.BlockSpec((n_slots,), lambda i, off: (0,), memory_space=pltpu.SMEM),
        ),
        out_shape=jax.ShapeDtypeStruct((n_slots,), jnp.int32),
        compiler_params=_cparams(("arbitrary",)),
        name="slot_tokens",
    )(off, cls1, rank1)


def _moe_kernel(te1_ref, te2_ref, cnt_ref, inv_ref, hm_ref, wgu1_ref, wd1_ref, wgu2_ref, wd2_ref,
                y_ref, hbuf, ybuf, gsem, ssem):
    D = wd1_ref.shape[1]
    ff = wd1_ref.shape[0]
    i = pl.program_id(0)
    b = i % 2

    def gather(tile, buf):
        def body(r, c):
            tok = inv_ref[tile * ROW_TILE + r]
            pltpu.make_async_copy(hm_ref.at[tok], hbuf.at[buf, r], gsem.at[buf]).start()
            return c
        lax.fori_loop(0, cnt_ref[tile], body, 0)

    def wait_rows(src, dst, sem, n):
        @pl.when(n > 0)
        def _():
            pltpu.make_async_copy(src.at[pl.ds(0, n)], dst.at[pl.ds(0, n)], sem).wait()

    @pl.when(i == 0)
    def _():
        gather(0, 0)

    gather(i + 1, 1 - b)
    wait_rows(hm_ref, hbuf.at[b], gsem.at[b], cnt_ref[i])

    @pl.when(i >= 2)
    def _():
        wait_rows(ybuf.at[b], y_ref, ssem.at[b], cnt_ref[jnp.maximum(i - 2, 0)])

    @pl.when(cnt_ref[i] > 0)
    def _():
        hb = hbuf.at[b]
        h = jnp.concatenate([hb[:, s, :] for s in range(D // LANES)], axis=1).astype(bf16)

        def expert(wgu_ref, wd_ref, w):
            gu = _dot(h, wgu_ref[...])
            g, u = gu[:, :ff], gu[:, ff:]
            he = g * (1.0 / (1.0 + jnp.exp(-g))) * u
            he = he * jnp.concatenate([w] * (ff // LANES), axis=1)
            return _dot(he.astype(bf16), wd_ref[...])

        y = expert(wgu1_ref, wd1_ref, hb[:, SUBLANES, :]) + expert(wgu2_ref, wd2_ref, hb[:, SUBLANES + 1, :])
        yb = ybuf.at[b]
        for s in range(D // LANES):
            yb[:, s, :] = y[:, s * LANES:(s + 1) * LANES]

        def scatter(r, c):
            tok = inv_ref[i * ROW_TILE + r]
            pltpu.make_async_copy(ybuf.at[b, r], y_ref.at[tok], ssem.at[b]).start()
            return c
        lax.fori_loop(0, cnt_ref[i], scatter, 0)


def _moe(te1, te2, cnt, inv, hm, wgu, wd):
    N = hm.shape[0]
    E, D, ff2 = wgu.shape
    ff = ff2 // 2
    n_steps = te1.shape[0] - 1

    def wmap(which):
        def f(i, te1, te2, cnt, inv):
            return ((te1, te2)[which][i], 0, 0)
        return f

    return pl.pallas_call(
        _moe_kernel,
        grid_spec=pltpu.PrefetchScalarGridSpec(
            num_scalar_prefetch=4,
            grid=(n_steps,),
            in_specs=[
                pl.BlockSpec(memory_space=pl.ANY),
                pl.BlockSpec((None, D, ff2), wmap(0)),
                pl.BlockSpec((None, ff, D), wmap(0)),
                pl.BlockSpec((None, D, ff2), wmap(1)),
                pl.BlockSpec((None, ff, D), wmap(1)),
            ],
            out_specs=pl.BlockSpec(memory_space=pl.ANY),
            scratch_shapes=[
                pltpu.VMEM((2, ROW_TILE, META_ROWS, LANES), f32),
                pltpu.VMEM((2, ROW_TILE, SUBLANES, LANES), f32),
                pltpu.SemaphoreType.DMA((2,)),
                pltpu.SemaphoreType.DMA((2,)),
            ],
        ),
        out_shape=jax.ShapeDtypeStruct((N, SUBLANES, LANES), f32),
        compiler_params=_cparams(("arbitrary",)),
        name="pair_moe",
    )(te1, te2, cnt, inv, hm, wgu, wd, wgu, wd)


def _pair_tables():
    lo = np.zeros((N_CLASSES,), np.int32)
    hi = np.zeros((N_CLASSES,), np.int32)
    for g in range(MOE_GROUPS):
        p = 0
        for a in range(EXPERTS_PER_GROUP):
            for b in range(a + 1, EXPERTS_PER_GROUP):
                lo[g * PAIRS_PER_GROUP + p] = g * EXPERTS_PER_GROUP + a
                hi[g * PAIRS_PER_GROUP + p] = g * EXPERTS_PER_GROUP + b
                p += 1
    return lo, hi


def _moe_layer(hm, cls, wgu, wd, chunk, tr):
    N = hm.shape[0]
    n_tiles = N // ROW_TILE + N_CLASSES
    rank, cnt = _class_ranks(cls, tr)
    counts = cnt[0, :N_CLASSES]
    tiles = (counts + ROW_TILE - 1) // ROW_TILE
    tile_end = jnp.cumsum(tiles)
    tile_start = tile_end - tiles
    off = jnp.zeros((LANES,), jnp.int32).at[:N_CLASSES].set(tile_start * ROW_TILE)
    nused = tile_end[-1]
    step = jnp.arange(n_tiles + 3, dtype=jnp.int32)
    tile_cls = jnp.sum(tile_end[None, :] <= jnp.minimum(step, nused - 1)[:, None], axis=1)
    tile_cls = jnp.minimum(tile_cls, N_CLASSES - 1).astype(jnp.int32)
    lo_tab, hi_tab = _pair_tables()
    te1 = jnp.asarray(lo_tab)[tile_cls]
    te2 = jnp.asarray(hi_tab)[tile_cls]
    in_cls = (step - tile_start[tile_cls]) * ROW_TILE
    tile_cnt = jnp.where(step < nused, jnp.clip(counts[tile_cls] - in_cls, 0, ROW_TILE), 0).astype(jnp.int32)

    inv = _slot_tokens(off, cls.reshape(N), rank.reshape(N), n_tiles * ROW_TILE, chunk)
    return _moe(te1, te2, tile_cnt, inv, hm, wgu, wd)


def _swap_rope(w):
    half = QK_ROPE // 2
    return jnp.concatenate([-w[..., half:], w[..., :half]], axis=-1)


def _prep_mla_weights(w_in, w_uq, w_ukv):
    D = w_in.shape[0]
    w_kr = w_in[:, Q_LORA + KV_LORA:]
    w_ks = _swap_rope(w_kr)
    zeros = jnp.zeros((D, QK_NOPE), w_in.dtype)
    win = jnp.concatenate([w_in[:, :Q_LORA + KV_LORA], zeros, w_kr, w_kr, zeros, w_ks, w_ks], axis=1)
    uq = w_uq.reshape(Q_LORA, MLA_HEADS, QK_NOPE + QK_ROPE)
    rope = uq[..., QK_NOPE:]
    wuq = jnp.concatenate([uq[..., :QK_NOPE], rope, _swap_rope(rope)], axis=-1).reshape(Q_LORA, -1)
    ukv = w_ukv.reshape(KV_LORA, MLA_HEADS, QK_NOPE + V_HEAD)
    wkn = jnp.concatenate([ukv[..., :QK_NOPE], jnp.zeros((KV_LORA, MLA_HEADS, HEAD_W - QK_NOPE), w_ukv.dtype)],
                          axis=-1).reshape(KV_LORA, -1)
    wv = ukv[..., QK_NOPE:].reshape(KV_LORA, -1)
    return win.astype(bf16), wuq.astype(bf16), wkn.astype(bf16), wv.astype(bf16)


def _rope_inv_freq():
    inv = 1.0 / (ROPE_THETA ** (np.arange(0, QK_ROPE, 2, dtype=np.float32) / QK_ROPE))
    inv2 = np.concatenate([inv, inv]).astype(np.float32)
    return np.concatenate([np.zeros((QK_NOPE,), np.float32), inv2, inv2]).reshape(1, LANES)


def _prep_router(w_gr, b_gr, w_er, b_er):
    D = w_gr.shape[0]
    n = MOE_GROUPS + MOE_GROUPS * EXPERTS_PER_GROUP
    w = jnp.concatenate([w_gr, w_er, jnp.zeros((D, LANES - n), f32)], axis=1)
    b = jnp.concatenate([b_gr, b_er, jnp.zeros((LANES - n,), f32)]).reshape(1, LANES)
    hi, lo = _split_bf16(w)
    return hi, lo, b


def kernel(x, c, positions, w_mod, b_mod, ln_g, ln_b, w_in, q_norm_g, kv_norm_g, w_uq, w_ukv, w_o, w_pool,
           pool_scale, w_group_router, b_group_router, w_expert_router, b_expert_router, w_gate, w_up, w_down):
    B, S, D = x.shape
    N = B * S
    tm = min(256, S)
    tq = min(512, S)
    tr = min(512, N)
    chunk = min(2048, N)

    mod = _modulation(c, w_mod, b_mod)
    ln = jnp.stack([ln_g, ln_b], axis=2)
    x2 = x.reshape(N, D)

    win, wuq, wkn, wv = _prep_mla_weights(w_in[0], w_uq[0], w_ukv[0])
    q, k, v = _mla_proj(x2, positions.reshape(N, 1), mod[0], win, q_norm_g[0].reshape(1, -1),
                        kv_norm_g[0].reshape(1, -1), wuq, wkn, wv, jnp.asarray(_rope_inv_freq()), B, S, tm)
    o = _attention(q, k, v.reshape(B, S, MLA_HEADS * V_HEAD), tq)
    wr_hi, wr_lo, br = _prep_router(w_group_router[0], b_group_router[0], w_expert_router[0], b_expert_router[0])
    x1, hm, cls = _attn_out(o.reshape(N, D), x2, mod[0], w_o[0].astype(bf16), ln[0, 0], wr_hi, wr_lo, br, S, tm)
    wgu = jnp.concatenate([w_gate[0], w_up[0]], axis=-1).astype(bf16)
    y0 = _moe_layer(hm, cls, wgu, w_down[0].astype(bf16), chunk, tr)

    wr_hi, wr_lo, br = _prep_router(w_group_router[1], b_group_router[1], w_expert_router[1], b_expert_router[1])
    x3, hm, cls = _pool_layer(x1, y0, mod[0], ln[0, 1], mod[1], w_pool[0].astype(bf16),
                              pool_scale[0].reshape(1, D), ln[1, 0], wr_hi, wr_lo, br, S, tm)
    wgu = jnp.concatenate([w_gate[1], w_up[1]], axis=-1).astype(bf16)
    y1 = _moe_layer(hm, cls, wgu, w_down[1].astype(bf16), chunk, tr)

    out = _final_merge(x3, y1, mod[1], ln[1, 1], S, tm)
    return out.reshape(B, S, D)
```

```python
import functools

import numpy as np
import jax
import jax.numpy as jnp
from jax import lax
from jax.experimental import pallas as pl
from jax.experimental.pallas import tpu as pltpu

MLA_HEADS = 16
Q_LORA = 384
KV_LORA = 256
QK_NOPE = 64
QK_ROPE = 32
V_HEAD = 64
ROPE_THETA = 10000.0
POOL_WINDOWS = (2, 4, 8, 16)
MOE_GROUPS = 4
EXPERTS_PER_GROUP = 8
DEPTH = 2
DN_ALPHA = (2.0 * DEPTH) ** 0.25
LN_EPS = 1e-5
RMS_EPS = 1e-6
LOG2E = 1.4426950408889634

LANES = 128
SUBLANES = 8
VMEM_LIMIT = 56 * 1024 * 1024

HEAD_W = 128
PAIRS_PER_GROUP = EXPERTS_PER_GROUP * (EXPERTS_PER_GROUP - 1) // 2
N_CLASSES = MOE_GROUPS * PAIRS_PER_GROUP
ROW_TILE = 128
HM_ROWS = 16
Y_ROWS = 8
HALO = 16

f32 = jnp.float32
bf16 = jnp.bfloat16


def _cparams(sem):
    return pltpu.CompilerParams(dimension_semantics=sem, vmem_limit_bytes=VMEM_LIMIT)


def _split_bf16(a):
    hi = a.astype(bf16)
    lo = (a - hi.astype(f32)).astype(bf16)
    return hi, lo


def _dot(a, b):
    return jnp.dot(a, b, preferred_element_type=f32)


def _dot3(a_hi, a_lo, b_hi, b_lo):
    return _dot(a_hi, b_hi) + (_dot(a_lo, b_hi) + _dot(a_hi, b_lo))


def _rows_load(ref, n, pitch, width):
    return jnp.concatenate([ref[pl.ds(s, n, stride=pitch), :] for s in range(width // LANES)], axis=1)


def _rows_store(ref, val, pitch):
    n, width = val.shape
    for s in range(width // LANES):
        ref[pl.ds(s, n, stride=pitch), :] = val[:, s * LANES:(s + 1) * LANES]


def _layer_norm(x, g, b):
    mu = jnp.mean(x, axis=-1, keepdims=True)
    xc = x - mu
    var = jnp.mean(xc * xc, axis=-1, keepdims=True)
    return xc * lax.rsqrt(var + LN_EPS) * g + b


def _mod_kernel(c_ref, w_ref, b_ref, o_ref):
    c = c_ref[...]
    ca = c * (1.0 / (1.0 + jnp.exp(-c)))
    a_hi, a_lo = _split_bf16(ca)
    w_hi, w_lo = _split_bf16(w_ref[...])
    o_ref[...] = _dot3(a_hi, a_lo, w_hi, w_lo) + b_ref[...]


def _modulation(c, w_mod, b_mod):
    B, D = c.shape
    depth = w_mod.shape[0]
    out = pl.pallas_call(
        _mod_kernel,
        grid=(depth, 6),
        in_specs=[
            pl.BlockSpec((B, D), lambda i, j: (0, 0)),
            pl.BlockSpec((None, D, D), lambda i, j: (i, 0, j)),
            pl.BlockSpec((None, None, 1, D), lambda i, j: (i, j, 0, 0)),
        ],
        out_specs=pl.BlockSpec((None, None, B, D), lambda i, j: (i, j, 0, 0)),
        out_shape=jax.ShapeDtypeStruct((depth, 6, B, D), f32),
        compiler_params=_cparams(("arbitrary", "arbitrary")),
        name="adaln_mod",
    )(c, w_mod, b_mod.reshape(depth, 6, 1, D))
    out = jnp.transpose(out, (0, 2, 1, 3))
    return jnp.pad(out, ((0, 0), (0, 0), (0, 2), (0, 0)))


def _mla_proj_kernel(x_ref, pos_ref, mod_ref, win_ref, qg_ref, kg_ref, wuqt_ref, wkn_ref, wvt_ref,
                     invf_ref, qt_ref, k_ref, vt_ref):
    x = x_ref[...]
    h = x * (1.0 + mod_ref[1:2, :]) + mod_ref[0:1, :]
    z = _dot(h.astype(bf16), win_ref[...])
    zq = z[:, :Q_LORA]
    zkv = z[:, Q_LORA:Q_LORA + KV_LORA]
    za = z[:, Q_LORA + KV_LORA:Q_LORA + KV_LORA + LANES]
    zb = z[:, Q_LORA + KV_LORA + LANES:]
    q_lat = zq * lax.rsqrt(jnp.mean(zq * zq, axis=-1, keepdims=True) + RMS_EPS) * qg_ref[...]
    kv_lat = zkv * lax.rsqrt(jnp.mean(zkv * zkv, axis=-1, keepdims=True) + RMS_EPS) * kg_ref[...]

    ang = pos_ref[...].astype(f32) * invf_ref[...]
    cs = jnp.cos(ang)
    sn = jnp.sin(ang)
    lane = lax.broadcasted_iota(jnp.int32, cs.shape, 1)
    scale = (QK_NOPE + QK_ROPE) ** -0.5 * LOG2E
    tqn = scale * jnp.where(lane < QK_NOPE + QK_ROPE, cs, sn)
    kr = za * cs + zb * sn

    kn = _dot(kv_lat.astype(bf16), wkn_ref[...])
    for hd in range(MLA_HEADS):
        k_ref[hd] = (kn[:, hd * HEAD_W:(hd + 1) * HEAD_W] + kr).astype(bf16)

    tqt = tqn.T
    qt = _dot(wuqt_ref[...], q_lat.T.astype(bf16))
    for hd in range(MLA_HEADS):
        qt_ref[hd] = (qt[hd * HEAD_W:(hd + 1) * HEAD_W, :] * tqt).astype(bf16)
    vt_ref[...] = _dot(wvt_ref[...], kv_lat.T.astype(bf16)).astype(bf16)


def _mla_proj(x2, pos2, mod, win, qg, kg, wuqt, wkn, wvt, invf, B, S, tm, tq):
    N, D = x2.shape
    spb = S // tm
    per_q = tq // tm
    const = lambda shape: pl.BlockSpec(shape, lambda i: (0,) * len(shape))
    nq = S // tq
    return pl.pallas_call(
        _mla_proj_kernel,
        grid=(N // tm,),
        in_specs=[
            pl.BlockSpec((tm, D), lambda i: (i, 0)),
            pl.BlockSpec((tm, 1), lambda i: (i, 0)),
            pl.BlockSpec((None, 8, D), lambda i: (i // spb, 0, 0)),
            const(win.shape), const(qg.shape), const(kg.shape), const(wuqt.shape), const(wkn.shape),
            const(wvt.shape), const(invf.shape),
        ],
        out_specs=[
            pl.BlockSpec((None, MLA_HEADS, None, HEAD_W, tm),
                         lambda i: (i // spb, 0, (i % spb) // per_q, 0, i % per_q)),
            pl.BlockSpec((None, MLA_HEADS, tm, HEAD_W), lambda i: (i // spb, 0, i % spb, 0)),
            pl.BlockSpec((None, None, MLA_HEADS * V_HEAD, tm),
                         lambda i: (i // spb, (i % spb) // per_q, 0, i % per_q)),
        ],
        out_shape=[
            jax.ShapeDtypeStruct((B, MLA_HEADS, nq, HEAD_W, tq), bf16),
            jax.ShapeDtypeStruct((B, MLA_HEADS, S, HEAD_W), bf16),
            jax.ShapeDtypeStruct((B, nq, MLA_HEADS * V_HEAD, tq), bf16),
        ],
        compiler_params=_cparams(("arbitrary",)),
        name="mla_proj",
    )(x2, pos2, mod, win, qg, kg, wuqt, wkn, wvt, invf)


ONES_ROWS = 16


def _attn_kernel(qt_ref, k_ref, vt_ref, o_ref, vx_ref, m_ref, acc_ref, ot_ref, *, tq):
    nq = qt_ref.shape[1]
    for hh in range(2):
        for j in range(nq):
            vx_ref[hh, j, :V_HEAD, :] = vt_ref[j, hh * V_HEAD:(hh + 1) * V_HEAD, :]
            vx_ref[hh, j, V_HEAD:, :] = jnp.ones((ONES_ROWS, tq), bf16)

    def kv_step(qi, j, masked):
        k0 = pl.multiple_of(j * tq, tq)
        for hh in range(2):
            s = _dot(k_ref[hh, pl.ds(k0, tq), :], qt_ref[hh, qi])
            if masked:
                key = lax.broadcasted_iota(jnp.int32, (tq, tq), 0)
                qry = lax.broadcasted_iota(jnp.int32, (tq, tq), 1)
                s = jnp.where(key <= qry, s, -jnp.inf)
            m_old = m_ref[hh]
            m_new = jnp.maximum(m_old, jnp.max(s, axis=0, keepdims=True))
            alpha = jnp.exp2(m_old - m_new)
            p = jnp.exp2(s - m_new[0:1, :])
            pv = _dot(vx_ref[hh, j], p.astype(bf16))
            acc_ref[hh] = acc_ref[hh] * alpha[0:1, :] + pv
            m_ref[hh] = m_new

    def q_body(qi, carry):
        m_ref[...] = jnp.full(m_ref.shape, -jnp.inf, f32)
        acc_ref[...] = jnp.zeros(acc_ref.shape, f32)

        def full_body(j, c):
            kv_step(qi, j, False)
            return c

        lax.fori_loop(0, qi, full_body, 0)
        kv_step(qi, qi, True)
        for hh in range(2):
            acc = acc_ref[hh]
            ot_ref[hh * V_HEAD:(hh + 1) * V_HEAD, :] = acc[:V_HEAD, :] * (1.0 / acc[V_HEAD:V_HEAD + 1, :])
        o_ref[pl.ds(pl.multiple_of(qi * tq, tq), tq), :] = ot_ref[...].T.astype(bf16)
        return carry

    lax.fori_loop(0, nq, q_body, 0)


def _attention(qt, k, vt, tq):
    B, H, nq, W, _ = qt.shape
    S = nq * tq
    return pl.pallas_call(
        functools.partial(_attn_kernel, tq=tq),
        grid=(B, H // 2),
        in_specs=[
            pl.BlockSpec((None, 2, nq, W, tq), lambda b, p: (b, p, 0, 0, 0)),
            pl.BlockSpec((None, 2, S, W), lambda b, p: (b, p, 0, 0)),
            pl.BlockSpec((None, nq, 2 * V_HEAD, tq), lambda b, p: (b, 0, p, 0)),
        ],
        out_specs=pl.BlockSpec((None, S, LANES), lambda b, p: (b, 0, p)),
        out_shape=jax.ShapeDtypeStruct((B, S, H * V_HEAD), bf16),
        scratch_shapes=[
            pltpu.VMEM((2, nq, V_HEAD + ONES_ROWS, tq), bf16),
            pltpu.VMEM((2, SUBLANES, tq), f32),
            pltpu.VMEM((2, V_HEAD + ONES_ROWS, tq), f32),
            pltpu.VMEM((2 * V_HEAD, tq), f32),
        ],
        compiler_params=_cparams(("arbitrary", "arbitrary")),
        name="mla_attention",
    )(qt, k, vt)


def _post_mixer(x, y, mod_ref, ln_ref, wr_hi_ref, wr_lo_ref, br_ref, xo_ref, hm_ref, cls_ref):
    tm, D = x.shape
    x1 = _layer_norm(DN_ALPHA * x + (1.0 + mod_ref[2:3, :]) * y, ln_ref[0:1, :], ln_ref[1:2, :])
    xo_ref[...] = x1
    h2 = x1 * (1.0 + mod_ref[4:5, :]) + mod_ref[3:4, :]
    h_hi, h_lo = _split_bf16(h2)
    logits = _dot3(h_hi, h_lo, wr_hi_ref[...], wr_lo_ref[...]) + br_ref[...]

    lane = lax.broadcasted_iota(jnp.int32, logits.shape, 1).astype(f32)
    neg = -jnp.inf
    far = float(LANES)
    is_g = lane < MOE_GROUPS
    gl = jnp.where(is_g, logits, neg)
    gmax = jnp.max(gl, axis=-1, keepdims=True)
    gidx = jnp.min(jnp.where(gl == gmax, lane, far), axis=-1, keepdims=True)
    g_p = 1.0 / jnp.sum(jnp.where(is_g, jnp.exp(logits - gmax), 0.0), axis=-1, keepdims=True)
    base = MOE_GROUPS + EXPERTS_PER_GROUP * gidx
    el = jnp.where((lane >= base) & (lane < base + EXPERTS_PER_GROUP), logits, neg)
    t1 = jnp.max(el, axis=-1, keepdims=True)
    i1 = jnp.min(jnp.where(el == t1, lane, far), axis=-1, keepdims=True)
    el2 = jnp.where(lane == i1, neg, el)
    t2 = jnp.max(el2, axis=-1, keepdims=True)
    i2 = jnp.min(jnp.where(el2 == t2, lane, far), axis=-1, keepdims=True)
    e = jnp.exp(t2 - t1)
    w1 = g_p / (1.0 + e)
    w2 = g_p * e / (1.0 + e)
    a = i1 - base
    b = i2 - base
    a_first = a < b
    lo = jnp.where(a_first, a, b)
    hi = jnp.where(a_first, b, a)
    w_lo = jnp.where(a_first, w1, w2)
    w_hi = jnp.where(a_first, w2, w1)
    pair = lo * (2 * EXPERTS_PER_GROUP - 1 - lo) * 0.5 + (hi - lo - 1.0)
    cls_ref[...] = (gidx * PAIRS_PER_GROUP + pair).astype(jnp.int32)

    _rows_store(hm_ref, h2, HM_ROWS)
    hm_ref[pl.ds(D // LANES, tm, stride=HM_ROWS), :] = jnp.broadcast_to(w_lo, (tm, LANES))
    hm_ref[pl.ds(D // LANES + 1, tm, stride=HM_ROWS), :] = jnp.broadcast_to(w_hi, (tm, LANES))
    for s in range(D // LANES + 2, HM_ROWS):
        hm_ref[pl.ds(s, tm, stride=HM_ROWS), :] = jnp.zeros((tm, LANES), f32)


def _attn_out_kernel(o_ref, x_ref, mod_ref, wo_ref, ln_ref, wr_hi_ref, wr_lo_ref, br_ref,
                     xo_ref, hm_ref, cls_ref):
    y = _dot(o_ref[...], wo_ref[...])
    _post_mixer(x_ref[...], y, mod_ref, ln_ref, wr_hi_ref, wr_lo_ref, br_ref, xo_ref, hm_ref, cls_ref)


def _post_out_specs(N, D, tm):
    specs = [
        pl.BlockSpec((tm, D), lambda i: (i, 0)),
        pl.BlockSpec((tm * HM_ROWS, LANES), lambda i: (i, 0)),
        pl.BlockSpec((tm, 1), lambda i: (i, 0)),
    ]
    shapes = [
        jax.ShapeDtypeStruct((N, D), f32),
        jax.ShapeDtypeStruct((N * HM_ROWS, LANES), f32),
        jax.ShapeDtypeStruct((N, 1), jnp.int32),
    ]
    return specs, shapes


def _attn_out(o2, x2, mod, wo, ln, wr_hi, wr_lo, br, S, tm):
    N, D = x2.shape
    spb = S // tm
    const = lambda shape: pl.BlockSpec(shape, lambda i: (0,) * len(shape))
    out_specs, out_shapes = _post_out_specs(N, D, tm)
    return pl.pallas_call(
        _attn_out_kernel,
        grid=(N // tm,),
        in_specs=[
            pl.BlockSpec((tm, D), lambda i: (i, 0)),
            pl.BlockSpec((tm, D), lambda i: (i, 0)),
            pl.BlockSpec((None, 8, D), lambda i: (i // spb, 0, 0)),
            const(wo.shape), const(ln.shape), const(wr_hi.shape), const(wr_lo.shape), const(br.shape),
        ],
        out_specs=out_specs,
        out_shape=out_shapes,
        compiler_params=_cparams(("arbitrary",)),
        name="attn_out_router",
    )(o2, x2, mod, wo, ln, wr_hi, wr_lo, br)


def _pool_kernel(x_ref, y_ref, modp_ref, lnp_ref, mod_ref, wp_ref, ps_ref, ln_ref,
                 wr_hi_ref, wr_lo_ref, br_ref, xo_ref, hm_ref, cls_ref, hb_ref, *, spb):
    tm, D = x_ref.shape
    i = pl.program_id(0)
    t_blk = i % spb
    yprev = _rows_load(y_ref, tm, Y_ROWS, D)
    x2 = _layer_norm(DN_ALPHA * x_ref[...] + (1.0 + modp_ref[5:6, :]) * yprev,
                     lnp_ref[0:1, :], lnp_ref[1:2, :])
    h = x2 * (1.0 + mod_ref[1:2, :]) + mod_ref[0:1, :]

    @pl.when(t_blk == 0)
    def _():
        hb_ref[0:HALO, :] = jnp.zeros((HALO, D), f32)

    hb_ref[HALO:, :] = h
    t_seq = t_blk * tm + lax.broadcasted_iota(jnp.int32, (tm, 1), 0)
    gd = D // len(POOL_WINDOWS)
    ys = []
    for gi, w in enumerate(POOL_WINDOWS):
        c0, c1 = gi * gd, (gi + 1) * gd
        a = hb_ref[:, c0:c1]
        span = 1
        while span < w:
            a = a[span:, :] + a[:-span, :]
            span *= 2
        tsum = a[HALO - (w - 1):, :]
        cnt = jnp.minimum(t_seq + 1, w).astype(f32)
        mixed = tsum / cnt - h[:, c0:c1]
        ys.append(_dot(mixed.astype(bf16), wp_ref[gi]))
    hb_ref[0:HALO, :] = h[tm - HALO:, :]
    y = jnp.concatenate(ys, axis=1) * ps_ref[...]
    _post_mixer(x2, y, mod_ref, ln_ref, wr_hi_ref, wr_lo_ref, br_ref, xo_ref, hm_ref, cls_ref)


def _pool_layer(x2, yprev, modp, lnp, mod, wp, ps, ln, wr_hi, wr_lo, br, S, tm):
    N, D = x2.shape
    spb = S // tm
    const = lambda shape: pl.BlockSpec(shape, lambda i: (0,) * len(shape))
    out_specs, out_shapes = _post_out_specs(N, D, tm)
    return pl.pallas_call(
        functools.partial(_pool_kernel, spb=spb),
        grid=(N // tm,),
        in_specs=[
            pl.BlockSpec((tm, D), lambda i: (i, 0)),
            pl.BlockSpec((tm * Y_ROWS, LANES), lambda i: (i, 0)),
            pl.BlockSpec((None, 8, D), lambda i: (i // spb, 0, 0)),
            const(lnp.shape),
            pl.BlockSpec((None, 8, D), lambda i: (i // spb, 0, 0)),
            const(wp.shape), const(ps.shape), const(ln.shape),
            const(wr_hi.shape), const(wr_lo.shape), const(br.shape),
        ],
        out_specs=out_specs,
        out_shape=out_shapes,
        scratch_shapes=[pltpu.VMEM((HALO + tm, D), f32)],
        compiler_params=_cparams(("arbitrary",)),
        name="pool_mixer_router",
    )(x2, yprev, modp, lnp, mod, wp, ps, ln, wr_hi, wr_lo, br)


def _final_kernel(x_ref, y_ref, mod_ref, ln_ref, o_ref):
    tm, D = x_ref.shape
    y = _rows_load(y_ref, tm, Y_ROWS, D)
    o_ref[...] = _layer_norm(DN_ALPHA * x_ref[...] + (1.0 + mod_ref[5:6, :]) * y,
                             ln_ref[0:1, :], ln_ref[1:2, :])


def _final_merge(x2, y, mod, ln, S, tm):
    N, D = x2.shape
    spb = S // tm
    return pl.pallas_call(
        _final_kernel,
        grid=(N // tm,),
        in_specs=[
            pl.BlockSpec((tm, D), lambda i: (i, 0)),
            pl.BlockSpec((tm * Y_ROWS, LANES), lambda i: (i, 0)),
            pl.BlockSpec((None, 8, D), lambda i: (i // spb, 0, 0)),
            pl.BlockSpec(ln.shape, lambda i: (0, 0)),
        ],
        out_specs=pl.BlockSpec((tm, D), lambda i: (i, 0)),
        out_shape=jax.ShapeDtypeStruct((N, D), f32),
        compiler_params=_cparams(("arbitrary",)),
        name="final_merge",
    )(x2, y, mod, ln)


def _rank_kernel(cls_ref, rank_ref, cnt_ref, carry_ref):
    tr = cls_ref.shape[0]

    @pl.when(pl.program_id(0) == 0)
    def _():
        carry_ref[...] = jnp.zeros(carry_ref.shape, f32)

    lane = lax.broadcasted_iota(jnp.int32, (tr, LANES), 1)
    onehot = (cls_ref[...] == lane)
    oh = onehot.astype(bf16)
    r = lax.broadcasted_iota(jnp.int32, (tr, tr), 0)
    c = lax.broadcasted_iota(jnp.int32, (tr, tr), 1)
    earlier = (c < r).astype(bf16)
    before = _dot(earlier, oh) + carry_ref[0:1, :]
    rank_ref[...] = jnp.sum(jnp.where(onehot, before, 0.0), axis=-1, keepdims=True).astype(jnp.int32)
    total = carry_ref[0:1, :] + jnp.sum(oh.astype(f32), axis=0, keepdims=True)
    carry_ref[...] = jnp.broadcast_to(total, carry_ref.shape)
    cnt_ref[...] = jnp.broadcast_to(total, cnt_ref.shape).astype(jnp.int32)


def _class_ranks(cls, tr):
    N = cls.shape[0]
    return pl.pallas_call(
        _rank_kernel,
        grid=(N // tr,),
        in_specs=[pl.BlockSpec((tr, 1), lambda i: (i, 0))],
        out_specs=[
            pl.BlockSpec((tr, 1), lambda i: (i, 0)),
            pl.BlockSpec((SUBLANES, LANES), lambda i: (0, 0)),
        ],
        out_shape=[
            jax.ShapeDtypeStruct((N, 1), jnp.int32),
            jax.ShapeDtypeStruct((SUBLANES, LANES), jnp.int32),
        ],
        scratch_shapes=[pltpu.VMEM((SUBLANES, LANES), f32)],
        compiler_params=_cparams(("arbitrary",)),
        name="class_ranks",
    )(cls)


def _slot_token_kernel(off_ref, cls_ref, rank_ref, inv_ref, *, chunk):
    base = pl.program_id(0) * chunk

    @pl.when(pl.program_id(0) == 0)
    def _():
        def clear(s, c):
            inv_ref[s] = 0
            return c
        lax.fori_loop(0, inv_ref.shape[0], clear, 0, unroll=32)

    def body(t, c):
        inv_ref[off_ref[cls_ref[t]] + rank_ref[t]] = base + t
        return c

    lax.fori_loop(0, chunk, body, 0, unroll=16)


def _slot_tokens(off, cls1, rank1, n_slots, chunk):
    N = cls1.shape[0]
    return pl.pallas_call(
        functools.partial(_slot_token_kernel, chunk=chunk),
        grid_spec=pltpu.PrefetchScalarGridSpec(
            num_scalar_prefetch=1,
            grid=(N // chunk,),
            in_specs=[
                pl.BlockSpec((chunk,), lambda i, off: (i,), memory_space=pltpu.SMEM),
                pl.BlockSpec((chunk,), lambda i, off: (i,), memory_space=pltpu.SMEM),
            ],
            out_specs=pl.BlockSpec((n_slots,), lambda i, off: (0,), memory_space=pltpu.SMEM),
        ),
        out_shape=jax.ShapeDtypeStruct((n_slots,), jnp.int32),
        compiler_params=_cparams(("arbitrary",)),
        name="slot_tokens",
    )(off, cls1, rank1)


def _moe_kernel(te1_ref, te2_ref, cnt_ref, inv_ref, hm_ref, wgu1_ref, wd1_ref, wgu2_ref, wd2_ref,
                y_ref, hbuf0, hbuf1, ybuf0, ybuf1, gsem, ssem):
    D = wd1_ref.shape[1]
    ff = wd1_ref.shape[0]
    i = pl.program_id(0)
    n_cur = cnt_ref[i]
    n_prev = jnp.where(i >= 1, cnt_ref[jnp.maximum(i - 1, 0)], 0)
    n_next = cnt_ref[i + 1]

    def token_rows(ref, t, pitch):
        return ref.at[pl.ds(pl.multiple_of(t * pitch, pitch), pitch)]

    def start_gather(tile, r, hb, sem):
        tok = inv_ref[tile * ROW_TILE + r]
        pltpu.make_async_copy(token_rows(hm_ref, tok, HM_ROWS), token_rows(hb, r, HM_ROWS), sem).start()

    def start_scatter(tile, r, yb, sem):
        tok = inv_ref[tile * ROW_TILE + r]
        pltpu.make_async_copy(token_rows(yb, r, Y_ROWS), token_rows(y_ref, tok, Y_ROWS), sem).start()

    def wait_tokens(src, dst, sem, n, pitch):
        @pl.when(n > 0)
        def _():
            rows = pl.multiple_of(n * pitch, pitch)
            pltpu.make_async_copy(src.at[pl.ds(0, rows)], dst.at[pl.ds(0, rows)], sem).wait()

    @pl.when(i == 0)
    def _():
        def first(r, c):
            start_gather(0, r, hbuf0, gsem.at[0])
            return c
        lax.fori_loop(0, n_cur, first, 0)

    def step(hb_cur, hb_nxt, yb_cur, yb_prv, cur, nxt):
        wait_tokens(hm_ref, hb_cur, gsem.at[cur], n_cur, HM_ROWS)

        @pl.when(i >= 2)
        def _():
            wait_tokens(yb_cur, y_ref, ssem.at[cur], cnt_ref[jnp.maximum(i - 2, 0)], Y_ROWS)

        @pl.when((n_cur > 0) | (n_prev > 0))
        def _():
            for r in range(ROW_TILE):
                @pl.when(r < n_next)
                def _():
                    start_gather(i + 1, r, hb_nxt, gsem.at[nxt])

                @pl.when(r < n_prev)
                def _():
                    start_scatter(i - 1, r, yb_prv, ssem.at[nxt])

            h = _rows_load(hb_cur, ROW_TILE, HM_ROWS, D).astype(bf16)

            def expert(wgu_ref, wd_ref, gate_row):
                w = hb_cur[pl.ds(gate_row, ROW_TILE, stride=HM_ROWS), :]
                gu = _dot(h, wgu_ref[...])
                g, u = gu[:, :ff], gu[:, ff:]
                he = g * (1.0 / (1.0 + jnp.exp(-g))) * u
                he = he * jnp.concatenate([w] * (ff // LANES), axis=1)
                return _dot(he.astype(bf16), wd_ref[...])

            y = expert(wgu1_ref, wd1_ref, D // LANES) + expert(wgu2_ref, wd2_ref, D // LANES + 1)
            _rows_store(yb_cur, y, Y_ROWS)

    @pl.when(i % 2 == 0)
    def _():
        step(hbuf0, hbuf1, ybuf0, ybuf1, 0, 1)

    @pl.when(i % 2 == 1)
    def _():
        step(hbuf1, hbuf0, ybuf1, ybuf0, 1, 0)


def _moe(te1, te2, cnt, inv, hm, wgu, wd):
    N = hm.shape[0] // HM_ROWS
    E, D, ff2 = wgu.shape
    ff = ff2 // 2
    n_steps = te1.shape[0] - 1

    def wmap(which):
        def f(i, te1, te2, cnt, inv):
            return ((te1, te2)[which][i], 0, 0)
        return f

    return pl.pallas_call(
        _moe_kernel,
        grid_spec=pltpu.PrefetchScalarGridSpec(
            num_scalar_prefetch=4,
            grid=(n_steps,),
            in_specs=[
                pl.BlockSpec(memory_space=pl.ANY),
                pl.BlockSpec((None, D, ff2), wmap(0)),
                pl.BlockSpec((None, ff, D), wmap(0)),
                pl.BlockSpec((None, D, ff2), wmap(1)),
                pl.BlockSpec((None, ff, D), wmap(1)),
            ],
            out_specs=pl.BlockSpec(memory_space=pl.ANY),
            scratch_shapes=[
                pltpu.VMEM((ROW_TILE * HM_ROWS, LANES), f32),
                pltpu.VMEM((ROW_TILE * HM_ROWS, LANES), f32),
                pltpu.VMEM((ROW_TILE * Y_ROWS, LANES), f32),
                pltpu.VMEM((ROW_TILE * Y_ROWS, LANES), f32),
                pltpu.SemaphoreType.DMA((2,)),
                pltpu.SemaphoreType.DMA((2,)),
            ],
        ),
        out_shape=jax.ShapeDtypeStruct((N * Y_ROWS, LANES), f32),
        compiler_params=_cparams(("arbitrary",)),
        name="pair_moe",
    )(te1, te2, cnt, inv, hm, wgu, wd, wgu, wd)


def _pair_tables():
    lo = np.zeros((N_CLASSES,), np.int32)
    hi = np.zeros((N_CLASSES,), np.int32)
    for g in range(MOE_GROUPS):
        p = 0
        for a in range(EXPERTS_PER_GROUP):
            for b in range(a + 1, EXPERTS_PER_GROUP):
                lo[g * PAIRS_PER_GROUP + p] = g * EXPERTS_PER_GROUP + a
                hi[g * PAIRS_PER_GROUP + p] = g * EXPERTS_PER_GROUP + b
                p += 1
    return lo, hi


def _moe_layer(hm, cls, wgu, wd, chunk, tr):
    N = cls.shape[0]
    n_tiles = N // ROW_TILE + N_CLASSES
    rank, cnt = _class_ranks(cls, tr)
    counts = cnt[0, :N_CLASSES]
    tiles = (counts + ROW_TILE - 1) // ROW_TILE
    tile_end = jnp.cumsum(tiles)
    tile_start = tile_end - tiles
    off = jnp.zeros((LANES,), jnp.int32).at[:N_CLASSES].set(tile_start * ROW_TILE)
    nused = tile_end[-1]
    step = jnp.arange(n_tiles + 3, dtype=jnp.int32)
    tile_cls = jnp.sum(tile_end[None, :] <= jnp.minimum(step, nused - 1)[:, None], axis=1)
    tile_cls = jnp.minimum(tile_cls, N_CLASSES - 1).astype(jnp.int32)
    lo_tab, hi_tab = _pair_tables()
    te1 = jnp.asarray(lo_tab)[tile_cls]
    te2 = jnp.asarray(hi_tab)[tile_cls]
    in_cls = (step - tile_start[tile_cls]) * ROW_TILE
    tile_cnt = jnp.where(step < nused, jnp.clip(counts[tile_cls] - in_cls, 0, ROW_TILE), 0).astype(jnp.int32)

    inv = _slot_tokens(off, cls.reshape(N), rank.reshape(N), n_tiles * ROW_TILE, chunk)
    return _moe(te1, te2, tile_cnt, inv, hm, wgu, wd)


def _swap_rope(w):
    half = QK_ROPE // 2
    return jnp.concatenate([-w[..., half:], w[..., :half]], axis=-1)


def _prep_mla_weights(w_in, w_uq, w_ukv):
    D = w_in.shape[0]
    w_kr = w_in[:, Q_LORA + KV_LORA:]
    w_ks = _swap_rope(w_kr)
    zeros = jnp.zeros((D, QK_NOPE), w_in.dtype)
    win = jnp.concatenate([w_in[:, :Q_LORA + KV_LORA], zeros, w_kr, w_kr, zeros, w_ks, w_ks], axis=1)
    uq = w_uq.reshape(Q_LORA, MLA_HEADS, QK_NOPE + QK_ROPE)
    rope = uq[..., QK_NOPE:]
    wuq = jnp.concatenate([uq[..., :QK_NOPE], rope, _swap_rope(rope)], axis=-1).reshape(Q_LORA, -1)
    ukv = w_ukv.reshape(KV_LORA, MLA_HEADS, QK_NOPE + V_HEAD)
    wkn = jnp.concatenate([ukv[..., :QK_NOPE], jnp.zeros((KV_LORA, MLA_HEADS, HEAD_W - QK_NOPE), w_ukv.dtype)],
                          axis=-1).reshape(KV_LORA, -1)
    wv = ukv[..., QK_NOPE:].reshape(KV_LORA, -1)
    return win.astype(bf16), wuq.T.astype(bf16), wkn.astype(bf16), wv.T.astype(bf16)


def _rope_inv_freq():
    inv = 1.0 / (ROPE_THETA ** (np.arange(0, QK_ROPE, 2, dtype=np.float32) / QK_ROPE))
    inv2 = np.concatenate([inv, inv]).astype(np.float32)
    return np.concatenate([np.zeros((QK_NOPE,), np.float32), inv2, inv2]).reshape(1, LANES)


def _prep_router(w_gr, b_gr, w_er, b_er):
    D = w_gr.shape[0]
    n = MOE_GROUPS + MOE_GROUPS * EXPERTS_PER_GROUP
    w = jnp.concatenate([w_gr, w_er, jnp.zeros((D, LANES - n), f32)], axis=1)
    b = jnp.concatenate([b_gr, b_er, jnp.zeros((LANES - n,), f32)]).reshape(1, LANES)
    hi, lo = _split_bf16(w)
    return hi, lo, b


def kernel(x, c, positions, w_mod, b_mod, ln_g, ln_b, w_in, q_norm_g, kv_norm_g, w_uq, w_ukv, w_o, w_pool,
           pool_scale, w_group_router, b_group_router, w_expert_router, b_expert_router, w_gate, w_up, w_down):
    B, S, D = x.shape
    N = B * S
    tm = min(256, S)
    tq = min(1024, S)
    tr = min(512, N)
    chunk = min(2048, N)

    mod = _modulation(c, w_mod, b_mod)
    ln = jnp.stack([ln_g, ln_b], axis=2)
    x2 = x.reshape(N, D)

    win, wuqt, wkn, wvt = _prep_mla_weights(w_in[0], w_uq[0], w_ukv[0])
    qt, k, vt = _mla_proj(x2, positions.reshape(N, 1), mod[0], win, q_norm_g[0].reshape(1, -1),
                          kv_norm_g[0].reshape(1, -1), wuqt, wkn, wvt, jnp.asarray(_rope_inv_freq()),
                          B, S, tm, tq)
    o = _attention(qt, k, vt, tq)
    wr_hi, wr_lo, br = _prep_router(w_group_router[0], b_group_router[0], w_expert_router[0], b_expert_router[0])
    x1, hm, cls = _attn_out(o.reshape(N, D), x2, mod[0], w_o[0].astype(bf16), ln[0, 0], wr_hi, wr_lo, br, S, tm)
    wgu = jnp.concatenate([w_gate[0], w_up[0]], axis=-1).astype(bf16)
    y0 = _moe_layer(hm, cls, wgu, w_down[0].astype(bf16), chunk, tr)

    wr_hi, wr_lo, br = _prep_router(w_group_router[1], b_group_router[1], w_expert_router[1], b_expert_router[1])
    x3, hm, cls = _pool_layer(x1, y0, mod[0], ln[0, 1], mod[1], w_pool[0].astype(bf16),
                              pool_scale[0].reshape(1, D), ln[1, 0], wr_hi, wr_lo, br, S, tm)
    wgu = jnp.concatenate([w_gate[1], w_up[1]], axis=-1).astype(bf16)
    y1 = _moe_layer(hm, cls, wgu, w_down[1].astype(bf16), chunk, tr)

    out = _final_merge(x3, y1, mod[1], ln[1, 1], S, tm)
    return out.reshape(B, S, D)
```

```python
import functools

import numpy as np
import jax
import jax.numpy as jnp
from jax import lax
from jax.experimental import pallas as pl
from jax.experimental.pallas import tpu as pltpu

MLA_HEADS = 16
Q_LORA = 384
KV_LORA = 256
QK_NOPE = 64
QK_ROPE = 32
V_HEAD = 64
ROPE_THETA = 10000.0
POOL_WINDOWS = (2, 4, 8, 16)
MOE_GROUPS = 4
EXPERTS_PER_GROUP = 8
DEPTH = 2
DN_ALPHA = (2.0 * DEPTH) ** 0.25
LN_EPS = 1e-5
RMS_EPS = 1e-6
LOG2E = 1.4426950408889634

LANES = 128
SUBLANES = 8
VMEM_LIMIT = 56 * 1024 * 1024

HEAD_W = 128
PAIRS_PER_GROUP = EXPERTS_PER_GROUP * (EXPERTS_PER_GROUP - 1) // 2
N_CLASSES = MOE_GROUPS * PAIRS_PER_GROUP
ROW_TILE = 128
HM_ROWS = 16
Y_ROWS = 8
MOE_SLOTS = 3
HALO = 16

f32 = jnp.float32
bf16 = jnp.bfloat16


def _cparams(sem):
    return pltpu.CompilerParams(dimension_semantics=sem, vmem_limit_bytes=VMEM_LIMIT)


def _split_bf16(a):
    hi = a.astype(bf16)
    lo = (a - hi.astype(f32)).astype(bf16)
    return hi, lo


def _dot(a, b):
    return jnp.dot(a, b, preferred_element_type=f32)


def _dot3(a_hi, a_lo, b_hi, b_lo):
    return _dot(a_hi, b_hi) + (_dot(a_lo, b_hi) + _dot(a_hi, b_lo))


def _rows_load(ref, n, pitch, width):
    return jnp.concatenate([ref[pl.ds(s, n, stride=pitch), :] for s in range(width // LANES)], axis=1)


def _rows_store(ref, val, pitch):
    n, width = val.shape
    for s in range(width // LANES):
        ref[pl.ds(s, n, stride=pitch), :] = val[:, s * LANES:(s + 1) * LANES]


def _layer_norm(x, g, b):
    mu = jnp.mean(x, axis=-1, keepdims=True)
    xc = x - mu
    var = jnp.mean(xc * xc, axis=-1, keepdims=True)
    return xc * lax.rsqrt(var + LN_EPS) * g + b


def _mod_kernel(c_ref, w_ref, b_ref, o_ref):
    c = c_ref[...]
    ca = c * (1.0 / (1.0 + jnp.exp(-c)))
    a_hi, a_lo = _split_bf16(ca)
    w_hi, w_lo = _split_bf16(w_ref[...])
    o_ref[...] = _dot3(a_hi, a_lo, w_hi, w_lo) + b_ref[...]


def _modulation(c, w_mod, b_mod):
    B, D = c.shape
    depth = w_mod.shape[0]
    out = pl.pallas_call(
        _mod_kernel,
        grid=(depth, 6),
        in_specs=[
            pl.BlockSpec((B, D), lambda i, j: (0, 0)),
            pl.BlockSpec((None, D, D), lambda i, j: (i, 0, j)),
            pl.BlockSpec((None, None, 1, D), lambda i, j: (i, j, 0, 0)),
        ],
        out_specs=pl.BlockSpec((None, None, B, D), lambda i, j: (i, j, 0, 0)),
        out_shape=jax.ShapeDtypeStruct((depth, 6, B, D), f32),
        compiler_params=_cparams(("arbitrary", "arbitrary")),
        name="adaln_mod",
    )(c, w_mod, b_mod.reshape(depth, 6, 1, D))
    out = jnp.transpose(out, (0, 2, 1, 3))
    return jnp.pad(out, ((0, 0), (0, 0), (0, 2), (0, 0)))


def _mla_proj_kernel(x_ref, pos_ref, mod_ref, win_ref, qg_ref, kg_ref, wuqt_ref, wkn_ref, wvt_ref,
                     invf_ref, qt_ref, k_ref, vt_ref):
    x = x_ref[...]
    h = x * (1.0 + mod_ref[1:2, :]) + mod_ref[0:1, :]
    z = _dot(h.astype(bf16), win_ref[...])
    zq = z[:, :Q_LORA]
    zkv = z[:, Q_LORA:Q_LORA + KV_LORA]
    za = z[:, Q_LORA + KV_LORA:Q_LORA + KV_LORA + LANES]
    zb = z[:, Q_LORA + KV_LORA + LANES:]
    q_lat = zq * lax.rsqrt(jnp.mean(zq * zq, axis=-1, keepdims=True) + RMS_EPS) * qg_ref[...]
    kv_lat = zkv * lax.rsqrt(jnp.mean(zkv * zkv, axis=-1, keepdims=True) + RMS_EPS) * kg_ref[...]

    ang = pos_ref[...].astype(f32) * invf_ref[...]
    cs = jnp.cos(ang)
    sn = jnp.sin(ang)
    lane = lax.broadcasted_iota(jnp.int32, cs.shape, 1)
    scale = (QK_NOPE + QK_ROPE) ** -0.5 * LOG2E
    tqn = scale * jnp.where(lane < QK_NOPE + QK_ROPE, cs, sn)
    kr = za * cs + zb * sn

    kn = _dot(kv_lat.astype(bf16), wkn_ref[...])
    for hd in range(MLA_HEADS):
        k_ref[hd] = (kn[:, hd * HEAD_W:(hd + 1) * HEAD_W] + kr).astype(bf16)

    tqt = tqn.T
    qt = _dot(wuqt_ref[...], q_lat.T.astype(bf16))
    for hd in range(MLA_HEADS):
        qt_ref[hd] = (qt[hd * HEAD_W:(hd + 1) * HEAD_W, :] * tqt).astype(bf16)
    vt_ref[...] = _dot(wvt_ref[...], kv_lat.T.astype(bf16)).astype(bf16)


def _mla_proj(x2, pos2, mod, win, qg, kg, wuqt, wkn, wvt, invf, B, S, tm, tq):
    N, D = x2.shape
    spb = S // tm
    per_q = tq // tm
    const = lambda shape: pl.BlockSpec(shape, lambda i: (0,) * len(shape))
    nq = S // tq
    return pl.pallas_call(
        _mla_proj_kernel,
        grid=(N // tm,),
        in_specs=[
            pl.BlockSpec((tm, D), lambda i: (i, 0)),
            pl.BlockSpec((tm, 1), lambda i: (i, 0)),
            pl.BlockSpec((None, 8, D), lambda i: (i // spb, 0, 0)),
            const(win.shape), const(qg.shape), const(kg.shape), const(wuqt.shape), const(wkn.shape),
            const(wvt.shape), const(invf.shape),
        ],
        out_specs=[
            pl.BlockSpec((None, MLA_HEADS, None, HEAD_W, tm),
                         lambda i: (i // spb, 0, (i % spb) // per_q, 0, i % per_q)),
            pl.BlockSpec((None, MLA_HEADS, tm, HEAD_W), lambda i: (i // spb, 0, i % spb, 0)),
            pl.BlockSpec((None, None, MLA_HEADS * V_HEAD, tm),
                         lambda i: (i // spb, (i % spb) // per_q, 0, i % per_q)),
        ],
        out_shape=[
            jax.ShapeDtypeStruct((B, MLA_HEADS, nq, HEAD_W, tq), bf16),
            jax.ShapeDtypeStruct((B, MLA_HEADS, S, HEAD_W), bf16),
            jax.ShapeDtypeStruct((B, nq, MLA_HEADS * V_HEAD, tq), bf16),
        ],
        compiler_params=_cparams(("arbitrary",)),
        name="mla_proj",
    )(x2, pos2, mod, win, qg, kg, wuqt, wkn, wvt, invf)


ONES_ROWS = 16


def _attn_kernel(qt_ref, k_ref, vt_ref, o_ref, vx_ref, m_ref, acc_ref, ot_ref, *, tq):
    nq = qt_ref.shape[1]
    for hh in range(2):
        for j in range(nq):
            vx_ref[hh, j, :V_HEAD, :] = vt_ref[j, hh * V_HEAD:(hh + 1) * V_HEAD, :]
            vx_ref[hh, j, V_HEAD:, :] = jnp.ones((ONES_ROWS, tq), bf16)

    def kv_step(qi, j, masked):
        k0 = pl.multiple_of(j * tq, tq)
        scores = [_dot(k_ref[hh, pl.ds(k0, tq), :], qt_ref[hh, qi]) for hh in range(2)]
        for hh in range(2):
            s = scores[hh]
            if masked:
                key = lax.broadcasted_iota(jnp.int32, (tq, tq), 0)
                qry = lax.broadcasted_iota(jnp.int32, (tq, tq), 1)
                s = jnp.where(key <= qry, s, -jnp.inf)
            m_old = m_ref[hh]
            m_new = jnp.maximum(m_old, jnp.max(s, axis=0, keepdims=True))
            alpha = jnp.exp2(m_old - m_new)
            p = jnp.exp2(s - m_new[0:1, :])
            pv = _dot(vx_ref[hh, j], p.astype(bf16))
            acc_ref[hh] = acc_ref[hh] * alpha[0:1, :] + pv
            m_ref[hh] = m_new

    def q_body(qi, carry):
        m_ref[...] = jnp.full(m_ref.shape, -jnp.inf, f32)
        acc_ref[...] = jnp.zeros(acc_ref.shape, f32)

        def full_body(j, c):
            kv_step(qi, j, False)
            return c

        lax.fori_loop(0, qi, full_body, 0)
        kv_step(qi, qi, True)
        for hh in range(2):
            acc = acc_ref[hh]
            ot_ref[hh * V_HEAD:(hh + 1) * V_HEAD, :] = acc[:V_HEAD, :] * (1.0 / acc[V_HEAD:V_HEAD + 1, :])
        o_ref[pl.ds(pl.multiple_of(qi * tq, tq), tq), :] = ot_ref[...].T.astype(bf16)
        return carry

    lax.fori_loop(0, nq, q_body, 0)


def _attention(qt, k, vt, tq):
    B, H, nq, W, _ = qt.shape
    S = nq * tq
    return pl.pallas_call(
        functools.partial(_attn_kernel, tq=tq),
        grid=(B, H // 2),
        in_specs=[
            pl.BlockSpec((None, 2, nq, W, tq), lambda b, p: (b, p, 0, 0, 0)),
            pl.BlockSpec((None, 2, S, W), lambda b, p: (b, p, 0, 0)),
            pl.BlockSpec((None, nq, 2 * V_HEAD, tq), lambda b, p: (b, 0, p, 0)),
        ],
        out_specs=pl.BlockSpec((None, S, LANES), lambda b, p: (b, 0, p)),
        out_shape=jax.ShapeDtypeStruct((B, S, H * V_HEAD), bf16),
        scratch_shapes=[
            pltpu.VMEM((2, nq, V_HEAD + ONES_ROWS, tq), bf16),
            pltpu.VMEM((2, SUBLANES, tq), f32),
            pltpu.VMEM((2, V_HEAD + ONES_ROWS, tq), f32),
            pltpu.VMEM((2 * V_HEAD, tq), f32),
        ],
        compiler_params=_cparams(("arbitrary", "arbitrary")),
        name="mla_attention",
    )(qt, k, vt)


def _post_mixer(x, y, mod_ref, ln_ref, wr_hi_ref, wr_lo_ref, br_ref, xo_ref, hm_ref, cls_ref):
    tm, D = x.shape
    x1 = _layer_norm(DN_ALPHA * x + (1.0 + mod_ref[2:3, :]) * y, ln_ref[0:1, :], ln_ref[1:2, :])
    xo_ref[...] = x1
    h2 = x1 * (1.0 + mod_ref[4:5, :]) + mod_ref[3:4, :]
    h_hi, h_lo = _split_bf16(h2)
    logits = _dot3(h_hi, h_lo, wr_hi_ref[...], wr_lo_ref[...]) + br_ref[...]

    lane = lax.broadcasted_iota(jnp.int32, logits.shape, 1).astype(f32)
    neg = -jnp.inf
    far = float(LANES)
    is_g = lane < MOE_GROUPS
    gl = jnp.where(is_g, logits, neg)
    gmax = jnp.max(gl, axis=-1, keepdims=True)
    gidx = jnp.min(jnp.where(gl == gmax, lane, far), axis=-1, keepdims=True)
    g_p = 1.0 / jnp.sum(jnp.where(is_g, jnp.exp(logits - gmax), 0.0), axis=-1, keepdims=True)
    base = MOE_GROUPS + EXPERTS_PER_GROUP * gidx
    el = jnp.where((lane >= base) & (lane < base + EXPERTS_PER_GROUP), logits, neg)
    t1 = jnp.max(el, axis=-1, keepdims=True)
    i1 = jnp.min(jnp.where(el == t1, lane, far), axis=-1, keepdims=True)
    el2 = jnp.where(lane == i1, neg, el)
    t2 = jnp.max(el2, axis=-1, keepdims=True)
    i2 = jnp.min(jnp.where(el2 == t2, lane, far), axis=-1, keepdims=True)
    e = jnp.exp(t2 - t1)
    w1 = g_p / (1.0 + e)
    w2 = g_p * e / (1.0 + e)
    a = i1 - base
    b = i2 - base
    a_first = a < b
    lo = jnp.where(a_first, a, b)
    hi = jnp.where(a_first, b, a)
    w_lo = jnp.where(a_first, w1, w2)
    w_hi = jnp.where(a_first, w2, w1)
    pair = lo * (2 * EXPERTS_PER_GROUP - 1 - lo) * 0.5 + (hi - lo - 1.0)
    cls_ref[...] = (gidx * PAIRS_PER_GROUP + pair).astype(jnp.int32)

    _rows_store(hm_ref, h2, HM_ROWS)
    hm_ref[pl.ds(D // LANES, tm, stride=HM_ROWS), :] = jnp.broadcast_to(w_lo, (tm, LANES))
    hm_ref[pl.ds(D // LANES + 1, tm, stride=HM_ROWS), :] = jnp.broadcast_to(w_hi, (tm, LANES))
    for s in range(D // LANES + 2, HM_ROWS):
        hm_ref[pl.ds(s, tm, stride=HM_ROWS), :] = jnp.zeros((tm, LANES), f32)


def _attn_out_kernel(o_ref, x_ref, mod_ref, wo_ref, ln_ref, wr_hi_ref, wr_lo_ref, br_ref,
                     xo_ref, hm_ref, cls_ref):
    y = _dot(o_ref[...], wo_ref[...])
    _post_mixer(x_ref[...], y, mod_ref, ln_ref, wr_hi_ref, wr_lo_ref, br_ref, xo_ref, hm_ref, cls_ref)


def _post_out_specs(N, D, tm):
    specs = [
        pl.BlockSpec((tm, D), lambda i: (i, 0)),
        pl.BlockSpec((tm * HM_ROWS, LANES), lambda i: (i, 0)),
        pl.BlockSpec((tm, 1), lambda i: (i, 0)),
    ]
    shapes = [
        jax.ShapeDtypeStruct((N, D), f32),
        jax.ShapeDtypeStruct((N * HM_ROWS, LANES), f32),
        jax.ShapeDtypeStruct((N, 1), jnp.int32),
    ]
    return specs, shapes


def _attn_out(o2, x2, mod, wo, ln, wr_hi, wr_lo, br, S, tm):
    N, D = x2.shape
    spb = S // tm
    const = lambda shape: pl.BlockSpec(shape, lambda i: (0,) * len(shape))
    out_specs, out_shapes = _post_out_specs(N, D, tm)
    return pl.pallas_call(
        _attn_out_kernel,
        grid=(N // tm,),
        in_specs=[
            pl.BlockSpec((tm, D), lambda i: (i, 0)),
            pl.BlockSpec((tm, D), lambda i: (i, 0)),
            pl.BlockSpec((None, 8, D), lambda i: (i // spb, 0, 0)),
            const(wo.shape), const(ln.shape), const(wr_hi.shape), const(wr_lo.shape), const(br.shape),
        ],
        out_specs=out_specs,
        out_shape=out_shapes,
        compiler_params=_cparams(("arbitrary",)),
        name="attn_out_router",
    )(o2, x2, mod, wo, ln, wr_hi, wr_lo, br)


def _pool_kernel(x_ref, y_ref, modp_ref, lnp_ref, mod_ref, wp_ref, ps_ref, ln_ref,
                 wr_hi_ref, wr_lo_ref, br_ref, xo_ref, hm_ref, cls_ref, hb_ref, *, spb):
    tm, D = x_ref.shape
    i = pl.program_id(0)
    t_blk = i % spb
    yprev = _rows_load(y_ref, tm, Y_ROWS, D)
    x2 = _layer_norm(DN_ALPHA * x_ref[...] + (1.0 + modp_ref[5:6, :]) * yprev,
                     lnp_ref[0:1, :], lnp_ref[1:2, :])
    h = x2 * (1.0 + mod_ref[1:2, :]) + mod_ref[0:1, :]

    @pl.when(t_blk == 0)
    def _():
        hb_ref[0:HALO, :] = jnp.zeros((HALO, D), f32)

    hb_ref[HALO:, :] = h
    t_seq = t_blk * tm + lax.broadcasted_iota(jnp.int32, (tm, 1), 0)
    gd = D // len(POOL_WINDOWS)
    ys = []
    for gi, w in enumerate(POOL_WINDOWS):
        c0, c1 = gi * gd, (gi + 1) * gd
        a = hb_ref[:, c0:c1]
        span = 1
        while span < w:
            a = a[span:, :] + a[:-span, :]
            span *= 2
        tsum = a[HALO - (w - 1):, :]
        cnt = jnp.minimum(t_seq + 1, w).astype(f32)
        mixed = tsum / cnt - h[:, c0:c1]
        ys.append(_dot(mixed.astype(bf16), wp_ref[gi]))
    hb_ref[0:HALO, :] = h[tm - HALO:, :]
    y = jnp.concatenate(ys, axis=1) * ps_ref[...]
    _post_mixer(x2, y, mod_ref, ln_ref, wr_hi_ref, wr_lo_ref, br_ref, xo_ref, hm_ref, cls_ref)


def _pool_layer(x2, yprev, modp, lnp, mod, wp, ps, ln, wr_hi, wr_lo, br, S, tm):
    N, D = x2.shape
    spb = S // tm
    const = lambda shape: pl.BlockSpec(shape, lambda i: (0,) * len(shape))
    out_specs, out_shapes = _post_out_specs(N, D, tm)
    return pl.pallas_call(
        functools.partial(_pool_kernel, spb=spb),
        grid=(N // tm,),
        in_specs=[
            pl.BlockSpec((tm, D), lambda i: (i, 0)),
            pl.BlockSpec((tm * Y_ROWS, LANES), lambda i: (i, 0)),
            pl.BlockSpec((None, 8, D), lambda i: (i // spb, 0, 0)),
            const(lnp.shape),
            pl.BlockSpec((None, 8, D), lambda i: (i // spb, 0, 0)),
            const(wp.shape), const(ps.shape), const(ln.shape),
            const(wr_hi.shape), const(wr_lo.shape), const(br.shape),
        ],
        out_specs=out_specs,
        out_shape=out_shapes,
        scratch_shapes=[pltpu.VMEM((HALO + tm, D), f32)],
        compiler_params=_cparams(("arbitrary",)),
        name="pool_mixer_router",
    )(x2, yprev, modp, lnp, mod, wp, ps, ln, wr_hi, wr_lo, br)


def _final_kernel(x_ref, y_ref, mod_ref, ln_ref, o_ref):
    tm, D = x_ref.shape
    y = _rows_load(y_ref, tm, Y_ROWS, D)
    o_ref[...] = _layer_norm(DN_ALPHA * x_ref[...] + (1.0 + mod_ref[5:6, :]) * y,
                             ln_ref[0:1, :], ln_ref[1:2, :])


def _final_merge(x2, y, mod, ln, S, tm):
    N, D = x2.shape
    spb = S // tm
    return pl.pallas_call(
        _final_kernel,
        grid=(N // tm,),
        in_specs=[
            pl.BlockSpec((tm, D), lambda i: (i, 0)),
            pl.BlockSpec((tm * Y_ROWS, LANES), lambda i: (i, 0)),
            pl.BlockSpec((None, 8, D), lambda i: (i // spb, 0, 0)),
            pl.BlockSpec(ln.shape, lambda i: (0, 0)),
        ],
        out_specs=pl.BlockSpec((tm, D), lambda i: (i, 0)),
        out_shape=jax.ShapeDtypeStruct((N, D), f32),
        compiler_params=_cparams(("arbitrary",)),
        name="final_merge",
    )(x2, y, mod, ln)


def _rank_kernel(cls_ref, rank_ref, cnt_ref, carry_ref):
    tr = cls_ref.shape[0]

    @pl.when(pl.program_id(0) == 0)
    def _():
        carry_ref[...] = jnp.zeros(carry_ref.shape, f32)

    lane = lax.broadcasted_iota(jnp.int32, (tr, LANES), 1)
    onehot = (cls_ref[...] == lane)
    oh = onehot.astype(bf16)
    r = lax.broadcasted_iota(jnp.int32, (tr, tr), 0)
    c = lax.broadcasted_iota(jnp.int32, (tr, tr), 1)
    earlier = (c < r).astype(bf16)
    before = _dot(earlier, oh) + carry_ref[0:1, :]
    rank_ref[...] = jnp.sum(jnp.where(onehot, before, 0.0), axis=-1, keepdims=True).astype(jnp.int32)
    total = carry_ref[0:1, :] + jnp.sum(oh.astype(f32), axis=0, keepdims=True)
    carry_ref[...] = jnp.broadcast_to(total, carry_ref.shape)
    cnt_ref[...] = jnp.broadcast_to(total, cnt_ref.shape).astype(jnp.int32)


def _class_ranks(cls, tr):
    N = cls.shape[0]
    return pl.pallas_call(
        _rank_kernel,
        grid=(N // tr,),
        in_specs=[pl.BlockSpec((tr, 1), lambda i: (i, 0))],
        out_specs=[
            pl.BlockSpec((tr, 1), lambda i: (i, 0)),
            pl.BlockSpec((SUBLANES, LANES), lambda i: (0, 0)),
        ],
        out_shape=[
            jax.ShapeDtypeStruct((N, 1), jnp.int32),
            jax.ShapeDtypeStruct((SUBLANES, LANES), jnp.int32),
        ],
        scratch_shapes=[pltpu.VMEM((SUBLANES, LANES), f32)],
        compiler_params=_cparams(("arbitrary",)),
        name="class_ranks",
    )(cls)


def _slot_token_kernel(off_ref, cls_ref, rank_ref, inv_ref, *, chunk):
    base = pl.program_id(0) * chunk

    @pl.when(pl.program_id(0) == 0)
    def _():
        def clear(s, c):
            inv_ref[s] = 0
            return c
        lax.fori_loop(0, inv_ref.shape[0], clear, 0, unroll=32)

    def body(t, c):
        inv_ref[off_ref[cls_ref[t]] + rank_ref[t]] = base + t
        return c

    lax.fori_loop(0, chunk, body, 0, unroll=16)


def _slot_tokens(off, cls1, rank1, n_slots, chunk):
    N = cls1.shape[0]
    return pl.pallas_call(
        functools.partial(_slot_token_kernel, chunk=chunk),
        grid_spec=pltpu.PrefetchScalarGridSpec(
            num_scalar_prefetch=1,
            grid=(N // chunk,),
            in_specs=[
                pl.BlockSpec((chunk,), lambda i, off: (i,), memory_space=pltpu.SMEM),
                pl.BlockSpec((chunk,), lambda i, off: (i,), memory_space=pltpu.SMEM),
            ],
            out_specs=pl.BlockSpec((n_slots,), lambda i, off: (0,), memory_space=pltpu.SMEM),
        ),
        out_shape=jax.ShapeDtypeStruct((n_slots,), jnp.int32),
        compiler_params=_cparams(("arbitrary",)),
        name="slot_tokens",
    )(off, cls1, rank1)


def _moe_kernel(te1_ref, te2_ref, cnt_ref, inv_ref, hm_ref, wgu1_ref, wd1_ref, wgu2_ref, wd2_ref,
                y_ref, *scratch):
    hbufs = scratch[:MOE_SLOTS]
    ybufs = scratch[MOE_SLOTS:2 * MOE_SLOTS]
    gsem, ssem = scratch[2 * MOE_SLOTS:]
    D = wd1_ref.shape[1]
    ff = wd1_ref.shape[0]
    i = pl.program_id(0)
    n_cur = cnt_ref[i]
    n_prev = jnp.where(i >= 1, cnt_ref[jnp.maximum(i - 1, 0)], 0)
    n_ahead = cnt_ref[i + 2]

    def token_rows(ref, t, pitch):
        return ref.at[pl.ds(pl.multiple_of(t * pitch, pitch), pitch)]

    def start_gather(tile, r, hb, sem):
        tok = inv_ref[tile * ROW_TILE + r]
        pltpu.make_async_copy(token_rows(hm_ref, tok, HM_ROWS), token_rows(hb, r, HM_ROWS), sem).start()

    def start_scatter(tile, r, yb, sem):
        tok = inv_ref[tile * ROW_TILE + r]
        pltpu.make_async_copy(token_rows(yb, r, Y_ROWS), token_rows(y_ref, tok, Y_ROWS), sem).start()

    def wait_tokens(src, dst, sem, n, pitch):
        @pl.when(n > 0)
        def _():
            rows = pl.multiple_of(n * pitch, pitch)
            pltpu.make_async_copy(src.at[pl.ds(0, rows)], dst.at[pl.ds(0, rows)], sem).wait()

    @pl.when(i == 0)
    def _():
        for t in range(MOE_SLOTS - 1):
            def first(r, c):
                start_gather(t, r, hbufs[t], gsem.at[t])
                return c
            lax.fori_loop(0, cnt_ref[t], first, 0)

    def step(cur):
        far = (cur + MOE_SLOTS - 1) % MOE_SLOTS
        hb_cur, yb_cur = hbufs[cur], ybufs[cur]
        wait_tokens(hm_ref, hb_cur, gsem.at[cur], n_cur, HM_ROWS)

        @pl.when(i >= MOE_SLOTS)
        def _():
            wait_tokens(yb_cur, y_ref, ssem.at[cur], cnt_ref[jnp.maximum(i - MOE_SLOTS, 0)], Y_ROWS)

        @pl.when((n_cur > 0) | (n_prev > 0))
        def _():
            for r in range(ROW_TILE):
                @pl.when(r < n_ahead)
                def _():
                    start_gather(i + 2, r, hbufs[far], gsem.at[far])

                @pl.when(r < n_prev)
                def _():
                    start_scatter(i - 1, r, ybufs[far], ssem.at[far])

            h = _rows_load(hb_cur, ROW_TILE, HM_ROWS, D).astype(bf16)

            def expert(wgu_ref, wd_ref, gate_row):
                w = hb_cur[pl.ds(gate_row, ROW_TILE, stride=HM_ROWS), :]
                gu = _dot(h, wgu_ref[...])
                g, u = gu[:, :ff], gu[:, ff:]
                he = g * (1.0 / (1.0 + jnp.exp(-g))) * u
                he = he * jnp.concatenate([w] * (ff // LANES), axis=1)
                return _dot(he.astype(bf16), wd_ref[...])

            y = expert(wgu1_ref, wd1_ref, D // LANES) + expert(wgu2_ref, wd2_ref, D // LANES + 1)
            _rows_store(yb_cur, y, Y_ROWS)

    for cur in range(MOE_SLOTS):
        @pl.when(i % MOE_SLOTS == cur)
        def _():
            step(cur)


def _moe(te1, te2, cnt, inv, hm, wgu, wd):
    N = hm.shape[0] // HM_ROWS
    E, D, ff2 = wgu.shape
    ff = ff2 // 2
    n_steps = te1.shape[0] - 2

    def wmap(which):
        def f(i, te1, te2, cnt, inv):
            return ((te1, te2)[which][i], 0, 0)
        return f

    return pl.pallas_call(
        _moe_kernel,
        grid_spec=pltpu.PrefetchScalarGridSpec(
            num_scalar_prefetch=4,
            grid=(n_steps,),
            in_specs=[
                pl.BlockSpec(memory_space=pl.ANY),
                pl.BlockSpec((None, D, ff2), wmap(0)),
                pl.BlockSpec((None, ff, D), wmap(0)),
                pl.BlockSpec((None, D, ff2), wmap(1)),
                pl.BlockSpec((None, ff, D), wmap(1)),
            ],
            out_specs=pl.BlockSpec(memory_space=pl.ANY),
            scratch_shapes=(
                [pltpu.VMEM((ROW_TILE * HM_ROWS, LANES), f32)] * MOE_SLOTS
                + [pltpu.VMEM((ROW_TILE * Y_ROWS, LANES), f32)] * MOE_SLOTS
                + [pltpu.SemaphoreType.DMA((MOE_SLOTS,)), pltpu.SemaphoreType.DMA((MOE_SLOTS,))]
            ),
        ),
        out_shape=jax.ShapeDtypeStruct((N * Y_ROWS, LANES), f32),
        compiler_params=_cparams(("arbitrary",)),
        name="pair_moe",
    )(te1, te2, cnt, inv, hm, wgu, wd, wgu, wd)


def _pair_tables():
    lo = np.zeros((N_CLASSES,), np.int32)
    hi = np.zeros((N_CLASSES,), np.int32)
    for g in range(MOE_GROUPS):
        p = 0
        for a in range(EXPERTS_PER_GROUP):
            for b in range(a + 1, EXPERTS_PER_GROUP):
                lo[g * PAIRS_PER_GROUP + p] = g * EXPERTS_PER_GROUP + a
                hi[g * PAIRS_PER_GROUP + p] = g * EXPERTS_PER_GROUP + b
                p += 1
    return lo, hi


def _moe_layer(hm, cls, wgu, wd, chunk, tr):
    N = cls.shape[0]
    n_tiles = N // ROW_TILE + N_CLASSES
    rank, cnt = _class_ranks(cls, tr)
    counts = cnt[0, :N_CLASSES]
    tiles = (counts + ROW_TILE - 1) // ROW_TILE
    tile_end = jnp.cumsum(tiles)
    tile_start = tile_end - tiles
    off = jnp.zeros((LANES,), jnp.int32).at[:N_CLASSES].set(tile_start * ROW_TILE)
    nused = tile_end[-1]
    step = jnp.arange(n_tiles + MOE_SLOTS + 2, dtype=jnp.int32)
    tile_cls = jnp.sum(tile_end[None, :] <= jnp.minimum(step, nused - 1)[:, None], axis=1)
    tile_cls = jnp.minimum(tile_cls, N_CLASSES - 1).astype(jnp.int32)
    lo_tab, hi_tab = _pair_tables()
    te1 = jnp.asarray(lo_tab)[tile_cls]
    te2 = jnp.asarray(hi_tab)[tile_cls]
    in_cls = (step - tile_start[tile_cls]) * ROW_TILE
    tile_cnt = jnp.where(step < nused, jnp.clip(counts[tile_cls] - in_cls, 0, ROW_TILE), 0).astype(jnp.int32)

    inv = _slot_tokens(off, cls.reshape(N), rank.reshape(N), n_tiles * ROW_TILE, chunk)
    return _moe(te1, te2, tile_cnt, inv, hm, wgu, wd)


def _swap_rope(w):
    half = QK_ROPE // 2
    return jnp.concatenate([-w[..., half:], w[..., :half]], axis=-1)


def _prep_mla_weights(w_in, w_uq, w_ukv):
    D = w_in.shape[0]
    w_kr = w_in[:, Q_LORA + KV_LORA:]
    w_ks = _swap_rope(w_kr)
    zeros = jnp.zeros((D, QK_NOPE), w_in.dtype)
    win = jnp.concatenate([w_in[:, :Q_LORA + KV_LORA], zeros, w_kr, w_kr, zeros, w_ks, w_ks], axis=1)
    uq = w_uq.reshape(Q_LORA, MLA_HEADS, QK_NOPE + QK_ROPE)
    rope = uq[..., QK_NOPE:]
    wuq = jnp.concatenate([uq[..., :QK_NOPE], rope, _swap_rope(rope)], axis=-1).reshape(Q_LORA, -1)
    ukv = w_ukv.reshape(KV_LORA, MLA_HEADS, QK_NOPE + V_HEAD)
    wkn = jnp.concatenate([ukv[..., :QK_NOPE], jnp.zeros((KV_LORA, MLA_HEADS, HEAD_W - QK_NOPE), w_ukv.dtype)],
                          axis=-1).reshape(KV_LORA, -1)
    wv = ukv[..., QK_NOPE:].reshape(KV_LORA, -1)
    return win.astype(bf16), wuq.T.astype(bf16), wkn.astype(bf16), wv.T.astype(bf16)


def _rope_inv_freq():
    inv = 1.0 / (ROPE_THETA ** (np.arange(0, QK_ROPE, 2, dtype=np.float32) / QK_ROPE))
    inv2 = np.concatenate([inv, inv]).astype(np.float32)
    return np.concatenate([np.zeros((QK_NOPE,), np.float32), inv2, inv2]).reshape(1, LANES)


def _prep_router(w_gr, b_gr, w_er, b_er):
    D = w_gr.shape[0]
    n = MOE_GROUPS + MOE_GROUPS * EXPERTS_PER_GROUP
    w = jnp.concatenate([w_gr, w_er, jnp.zeros((D, LANES - n), f32)], axis=1)
    b = jnp.concatenate([b_gr, b_er, jnp.zeros((LANES - n,), f32)]).reshape(1, LANES)
    hi, lo = _split_bf16(w)
    return hi, lo, b


def kernel(x, c, positions, w_mod, b_mod, ln_g, ln_b, w_in, q_norm_g, kv_norm_g, w_uq, w_ukv, w_o, w_pool,
           pool_scale, w_group_router, b_group_router, w_expert_router, b_expert_router, w_gate, w_up, w_down):
    B, S, D = x.shape
    N = B * S
    tm = min(512, S)
    tq = min(1024, S)
    tr = min(512, N)
    chunk = min(2048, N)

    mod = _modulation(c, w_mod, b_mod)
    ln = jnp.stack([ln_g, ln_b], axis=2)
    x2 = x.reshape(N, D)

    win, wuqt, wkn, wvt = _prep_mla_weights(w_in[0], w_uq[0], w_ukv[0])
    qt, k, vt = _mla_proj(x2, positions.reshape(N, 1), mod[0], win, q_norm_g[0].reshape(1, -1),
                          kv_norm_g[0].reshape(1, -1), wuqt, wkn, wvt, jnp.asarray(_rope_inv_freq()),
                          B, S, tm, tq)
    o = _attention(qt, k, vt, tq)
    wr_hi, wr_lo, br = _prep_router(w_group_router[0], b_group_router[0], w_expert_router[0], b_expert_router[0])
    x1, hm, cls = _attn_out(o.reshape(N, D), x2, mod[0], w_o[0].astype(bf16), ln[0, 0], wr_hi, wr_lo, br, S, tm)
    wgu = jnp.concatenate([w_gate[0], w_up[0]], axis=-1).astype(bf16)
    y0 = _moe_layer(hm, cls, wgu, w_down[0].astype(bf16), chunk, tr)

    wr_hi, wr_lo, br = _prep_router(w_group_router[1], b_group_router[1], w_expert_router[1], b_expert_router[1])
    x3, hm, cls = _pool_layer(x1, y0, mod[0], ln[0, 1], mod[1], w_pool[0].astype(bf16),
                              pool_scale[0].reshape(1, D), ln[1, 0], wr_hi, wr_lo, br, S, tm)
    wgu = jnp.concatenate([w_gate[1], w_up[1]], axis=-1).astype(bf16)
    y1 = _moe_layer(hm, cls, wgu, w_down[1].astype(bf16), chunk, tr)

    out = _final_merge(x3, y1, mod[1], ln[1, 1], S, tm)
    return out.reshape(B, S, D)
```

```python
import functools

import numpy as np
import jax
import jax.numpy as jnp
from jax import lax
from jax.experimental import pallas as pl
from jax.experimental.pallas import tpu as pltpu

MLA_HEADS = 16
Q_LORA = 384
KV_LORA = 256
QK_NOPE = 64
QK_ROPE = 32
V_HEAD = 64
ROPE_THETA = 10000.0
POOL_WINDOWS = (2, 4, 8, 16)
MOE_GROUPS = 4
EXPERTS_PER_GROUP = 8
DEPTH = 2
DN_ALPHA = (2.0 * DEPTH) ** 0.25
LN_EPS = 1e-5
RMS_EPS = 1e-6
LOG2E = 1.4426950408889634

LANES = 128
SUBLANES = 8
VMEM_LIMIT = 56 * 1024 * 1024

HEAD_W = 128
PAIRS_PER_GROUP = EXPERTS_PER_GROUP * (EXPERTS_PER_GROUP - 1) // 2
N_CLASSES = MOE_GROUPS * PAIRS_PER_GROUP
ROW_TILE = 128
HM_ROWS = 16
Y_ROWS = 8
MOE_SLOTS = 3
STEP_TILES = 1
SMEM_BLOCK = 1024
HALO = 16

f32 = jnp.float32
bf16 = jnp.bfloat16


def _cparams(sem):
    return pltpu.CompilerParams(dimension_semantics=sem, vmem_limit_bytes=VMEM_LIMIT)


def _split_bf16(a):
    hi = a.astype(bf16)
    lo = (a - hi.astype(f32)).astype(bf16)
    return hi, lo


def _dot(a, b):
    return jnp.dot(a, b, preferred_element_type=f32)


def _dot3(a_hi, a_lo, b_hi, b_lo):
    return _dot(a_hi, b_hi) + (_dot(a_lo, b_hi) + _dot(a_hi, b_lo))


def _rows_load(ref, n, pitch, width):
    return jnp.concatenate([ref[pl.ds(s, n, stride=pitch), :] for s in range(width // LANES)], axis=1)


def _rows_store(ref, val, pitch):
    n, width = val.shape
    for s in range(width // LANES):
        ref[pl.ds(s, n, stride=pitch), :] = val[:, s * LANES:(s + 1) * LANES]


def _layer_norm(x, g, b):
    mu = jnp.mean(x, axis=-1, keepdims=True)
    xc = x - mu
    var = jnp.mean(xc * xc, axis=-1, keepdims=True)
    return xc * lax.rsqrt(var + LN_EPS) * g + b


def _mod_kernel(c_ref, w_ref, b_ref, o_ref):
    c = c_ref[...]
    ca = c * (1.0 / (1.0 + jnp.exp(-c)))
    a_hi, a_lo = _split_bf16(ca)
    w_hi, w_lo = _split_bf16(w_ref[...])
    o_ref[...] = _dot3(a_hi, a_lo, w_hi, w_lo) + b_ref[...]


def _modulation(c, w_mod, b_mod):
    B, D = c.shape
    depth = w_mod.shape[0]
    out = pl.pallas_call(
        _mod_kernel,
        grid=(depth, 6),
        in_specs=[
            pl.BlockSpec((B, D), lambda i, j: (0, 0)),
            pl.BlockSpec((None, D, D), lambda i, j: (i, 0, j)),
            pl.BlockSpec((None, None, 1, D), lambda i, j: (i, j, 0, 0)),
        ],
        out_specs=pl.BlockSpec((None, None, B, D), lambda i, j: (i, j, 0, 0)),
        out_shape=jax.ShapeDtypeStruct((depth, 6, B, D), f32),
        compiler_params=_cparams(("arbitrary", "arbitrary")),
        name="adaln_mod",
    )(c, w_mod, b_mod.reshape(depth, 6, 1, D))
    out = jnp.transpose(out, (0, 2, 1, 3))
    return jnp.pad(out, ((0, 0), (0, 0), (0, 2), (0, 0)))


def _mla_proj_kernel(x_ref, pos_ref, mod_ref, win_ref, qg_ref, kg_ref, wuqt_ref, wkn_ref, wvt_ref,
                     invf_ref, qt_ref, k_ref, vt_ref):
    x = x_ref[...]
    h = x * (1.0 + mod_ref[1:2, :]) + mod_ref[0:1, :]
    z = _dot(h.astype(bf16), win_ref[...])
    zq = z[:, :Q_LORA]
    zkv = z[:, Q_LORA:Q_LORA + KV_LORA]
    za = z[:, Q_LORA + KV_LORA:Q_LORA + KV_LORA + LANES]
    zb = z[:, Q_LORA + KV_LORA + LANES:]
    q_lat = zq * lax.rsqrt(jnp.mean(zq * zq, axis=-1, keepdims=True) + RMS_EPS) * qg_ref[...]
    kv_lat = zkv * lax.rsqrt(jnp.mean(zkv * zkv, axis=-1, keepdims=True) + RMS_EPS) * kg_ref[...]

    ang = pos_ref[...].astype(f32) * invf_ref[...]
    cs = jnp.cos(ang)
    sn = jnp.sin(ang)
    lane = lax.broadcasted_iota(jnp.int32, cs.shape, 1)
    scale = (QK_NOPE + QK_ROPE) ** -0.5 * LOG2E
    tqn = scale * jnp.where(lane < QK_NOPE + QK_ROPE, cs, sn)
    kr = za * cs + zb * sn

    kn = _dot(kv_lat.astype(bf16), wkn_ref[...])
    for hd in range(MLA_HEADS):
        k_ref[hd] = (kn[:, hd * HEAD_W:(hd + 1) * HEAD_W] + kr).astype(bf16)

    tqt = tqn.T
    qt = _dot(wuqt_ref[...], q_lat.T.astype(bf16))
    for hd in range(MLA_HEADS):
        qt_ref[hd] = (qt[hd * HEAD_W:(hd + 1) * HEAD_W, :] * tqt).astype(bf16)
    vt_ref[...] = _dot(wvt_ref[...], kv_lat.T.astype(bf16)).astype(bf16)


def _mla_proj(x2, pos2, mod, win, qg, kg, wuqt, wkn, wvt, invf, B, S, tm, tq):
    N, D = x2.shape
    spb = S // tm
    per_q = tq // tm
    const = lambda shape: pl.BlockSpec(shape, lambda i: (0,) * len(shape))
    nq = S // tq
    return pl.pallas_call(
        _mla_proj_kernel,
        grid=(N // tm,),
        in_specs=[
            pl.BlockSpec((tm, D), lambda i: (i, 0)),
            pl.BlockSpec((tm, 1), lambda i: (i, 0)),
            pl.BlockSpec((None, 8, D), lambda i: (i // spb, 0, 0)),
            const(win.shape), const(qg.shape), const(kg.shape), const(wuqt.shape), const(wkn.shape),
            const(wvt.shape), const(invf.shape),
        ],
        out_specs=[
            pl.BlockSpec((None, MLA_HEADS, None, HEAD_W, tm),
                         lambda i: (i // spb, 0, (i % spb) // per_q, 0, i % per_q)),
            pl.BlockSpec((None, MLA_HEADS, tm, HEAD_W), lambda i: (i // spb, 0, i % spb, 0)),
            pl.BlockSpec((None, None, MLA_HEADS * V_HEAD, tm),
                         lambda i: (i // spb, (i % spb) // per_q, 0, i % per_q)),
        ],
        out_shape=[
            jax.ShapeDtypeStruct((B, MLA_HEADS, nq, HEAD_W, tq), bf16),
            jax.ShapeDtypeStruct((B, MLA_HEADS, S, HEAD_W), bf16),
            jax.ShapeDtypeStruct((B, nq, MLA_HEADS * V_HEAD, tq), bf16),
        ],
        compiler_params=_cparams(("arbitrary",)),
        name="mla_proj",
    )(x2, pos2, mod, win, qg, kg, wuqt, wkn, wvt, invf)


ONES_ROWS = 16


def _attn_kernel(qt_ref, k_ref, vt_ref, o_ref, vx_ref, m_ref, acc_ref, ot_ref, *, tq):
    nq = qt_ref.shape[1]
    for hh in range(2):
        for j in range(nq):
            vx_ref[hh, j, :V_HEAD, :] = vt_ref[j, hh * V_HEAD:(hh + 1) * V_HEAD, :]
            vx_ref[hh, j, V_HEAD:, :] = jnp.ones((ONES_ROWS, tq), bf16)

    def kv_step(qi, j, masked):
        k0 = pl.multiple_of(j * tq, tq)
        scores = [_dot(k_ref[hh, pl.ds(k0, tq), :], qt_ref[hh, qi]) for hh in range(2)]
        for hh in range(2):
            s = scores[hh]
            if masked:
                key = lax.broadcasted_iota(jnp.int32, (tq, tq), 0)
                qry = lax.broadcasted_iota(jnp.int32, (tq, tq), 1)
                s = jnp.where(key <= qry, s, -jnp.inf)
            m_old = m_ref[hh]
            m_new = jnp.maximum(m_old, jnp.max(s, axis=0, keepdims=True))
            alpha = jnp.exp2(m_old - m_new)
            p = jnp.exp2(s - m_new[0:1, :])
            pv = _dot(vx_ref[hh, j], p.astype(bf16))
            acc_ref[hh] = acc_ref[hh] * alpha[0:1, :] + pv
            m_ref[hh] = m_new

    def q_body(qi, carry):
        m_ref[...] = jnp.full(m_ref.shape, -jnp.inf, f32)
        acc_ref[...] = jnp.zeros(acc_ref.shape, f32)

        def full_body(j, c):
            kv_step(qi, j, False)
            return c

        lax.fori_loop(0, qi, full_body, 0)
        kv_step(qi, qi, True)
        for hh in range(2):
            acc = acc_ref[hh]
            ot_ref[hh * V_HEAD:(hh + 1) * V_HEAD, :] = acc[:V_HEAD, :] * (1.0 / acc[V_HEAD:V_HEAD + 1, :])
        o_ref[pl.ds(pl.multiple_of(qi * tq, tq), tq), :] = ot_ref[...].T.astype(bf16)
        return carry

    lax.fori_loop(0, nq, q_body, 0)


def _attention(qt, k, vt, tq):
    B, H, nq, W, _ = qt.shape
    S = nq * tq
    return pl.pallas_call(
        functools.partial(_attn_kernel, tq=tq),
        grid=(B, H // 2),
        in_specs=[
            pl.BlockSpec((None, 2, nq, W, tq), lambda b, p: (b, p, 0, 0, 0)),
            pl.BlockSpec((None, 2, S, W), lambda b, p: (b, p, 0, 0)),
            pl.BlockSpec((None, nq, 2 * V_HEAD, tq), lambda b, p: (b, 0, p, 0)),
        ],
        out_specs=pl.BlockSpec((None, S, LANES), lambda b, p: (b, 0, p)),
        out_shape=jax.ShapeDtypeStruct((B, S, H * V_HEAD), bf16),
        scratch_shapes=[
            pltpu.VMEM((2, nq, V_HEAD + ONES_ROWS, tq), bf16),
            pltpu.VMEM((2, SUBLANES, tq), f32),
            pltpu.VMEM((2, V_HEAD + ONES_ROWS, tq), f32),
            pltpu.VMEM((2 * V_HEAD, tq), f32),
        ],
        compiler_params=_cparams(("arbitrary", "arbitrary")),
        name="mla_attention",
    )(qt, k, vt)


def _post_mixer(x, y, mod_ref, ln_ref, wr_hi_ref, wr_lo_ref, br_ref, xo_ref, hm_ref, cls_ref):
    tm, D = x.shape
    x1 = _layer_norm(DN_ALPHA * x + (1.0 + mod_ref[2:3, :]) * y, ln_ref[0:1, :], ln_ref[1:2, :])
    xo_ref[...] = x1
    h2 = x1 * (1.0 + mod_ref[4:5, :]) + mod_ref[3:4, :]
    h_hi, h_lo = _split_bf16(h2)
    logits = _dot3(h_hi, h_lo, wr_hi_ref[...], wr_lo_ref[...]) + br_ref[...]

    lane = lax.broadcasted_iota(jnp.int32, logits.shape, 1).astype(f32)
    neg = -jnp.inf
    far = float(LANES)
    is_g = lane < MOE_GROUPS
    gl = jnp.where(is_g, logits, neg)
    gmax = jnp.max(gl, axis=-1, keepdims=True)
    gidx = jnp.min(jnp.where(gl == gmax, lane, far), axis=-1, keepdims=True)
    g_p = 1.0 / jnp.sum(jnp.where(is_g, jnp.exp(logits - gmax), 0.0), axis=-1, keepdims=True)
    base = MOE_GROUPS + EXPERTS_PER_GROUP * gidx
    el = jnp.where((lane >= base) & (lane < base + EXPERTS_PER_GROUP), logits, neg)
    t1 = jnp.max(el, axis=-1, keepdims=True)
    i1 = jnp.min(jnp.where(el == t1, lane, far), axis=-1, keepdims=True)
    el2 = jnp.where(lane == i1, neg, el)
    t2 = jnp.max(el2, axis=-1, keepdims=True)
    i2 = jnp.min(jnp.where(el2 == t2, lane, far), axis=-1, keepdims=True)
    e = jnp.exp(t2 - t1)
    w1 = g_p / (1.0 + e)
    w2 = g_p * e / (1.0 + e)
    a = i1 - base
    b = i2 - base
    a_first = a < b
    lo = jnp.where(a_first, a, b)
    hi = jnp.where(a_first, b, a)
    w_lo = jnp.where(a_first, w1, w2)
    w_hi = jnp.where(a_first, w2, w1)
    pair = lo * (2 * EXPERTS_PER_GROUP - 1 - lo) * 0.5 + (hi - lo - 1.0)
    cls_ref[...] = (gidx * PAIRS_PER_GROUP + pair).astype(jnp.int32)

    _rows_store(hm_ref, h2, HM_ROWS)
    hm_ref[pl.ds(D // LANES, tm, stride=HM_ROWS), :] = jnp.broadcast_to(w_lo, (tm, LANES))
    hm_ref[pl.ds(D // LANES + 1, tm, stride=HM_ROWS), :] = jnp.broadcast_to(w_hi, (tm, LANES))
    for s in range(D // LANES + 2, HM_ROWS):
        hm_ref[pl.ds(s, tm, stride=HM_ROWS), :] = jnp.zeros((tm, LANES), f32)


def _attn_out_kernel(o_ref, x_ref, mod_ref, wo_ref, ln_ref, wr_hi_ref, wr_lo_ref, br_ref,
                     xo_ref, hm_ref, cls_ref):
    y = _dot(o_ref[...], wo_ref[...])
    _post_mixer(x_ref[...], y, mod_ref, ln_ref, wr_hi_ref, wr_lo_ref, br_ref, xo_ref, hm_ref, cls_ref)


def _post_out_specs(N, D, tm):
    specs = [
        pl.BlockSpec((tm, D), lambda i: (i, 0)),
        pl.BlockSpec((tm * HM_ROWS, LANES), lambda i: (i, 0)),
        pl.BlockSpec((tm, 1), lambda i: (i, 0)),
    ]
    shapes = [
        jax.ShapeDtypeStruct((N, D), f32),
        jax.ShapeDtypeStruct((N * HM_ROWS, LANES), f32),
        jax.ShapeDtypeStruct((N, 1), jnp.int32),
    ]
    return specs, shapes


def _attn_out(o2, x2, mod, wo, ln, wr_hi, wr_lo, br, S, tm):
    N, D = x2.shape
    spb = S // tm
    const = lambda shape: pl.BlockSpec(shape, lambda i: (0,) * len(shape))
    out_specs, out_shapes = _post_out_specs(N, D, tm)
    return pl.pallas_call(
        _attn_out_kernel,
        grid=(N // tm,),
        in_specs=[
            pl.BlockSpec((tm, D), lambda i: (i, 0)),
            pl.BlockSpec((tm, D), lambda i: (i, 0)),
            pl.BlockSpec((None, 8, D), lambda i: (i // spb, 0, 0)),
            const(wo.shape), const(ln.shape), const(wr_hi.shape), const(wr_lo.shape), const(br.shape),
        ],
        out_specs=out_specs,
        out_shape=out_shapes,
        compiler_params=_cparams(("arbitrary",)),
        name="attn_out_router",
    )(o2, x2, mod, wo, ln, wr_hi, wr_lo, br)


def _pool_kernel(x_ref, y_ref, modp_ref, lnp_ref, mod_ref, wp_ref, ps_ref, ln_ref,
                 wr_hi_ref, wr_lo_ref, br_ref, xo_ref, hm_ref, cls_ref, hb_ref, *, spb):
    tm, D = x_ref.shape
    i = pl.program_id(0)
    t_blk = i % spb
    yprev = _rows_load(y_ref, tm, Y_ROWS, D)
    x2 = _layer_norm(DN_ALPHA * x_ref[...] + (1.0 + modp_ref[5:6, :]) * yprev,
                     lnp_ref[0:1, :], lnp_ref[1:2, :])
    h = x2 * (1.0 + mod_ref[1:2, :]) + mod_ref[0:1, :]

    @pl.when(t_blk == 0)
    def _():
        hb_ref[0:HALO, :] = jnp.zeros((HALO, D), f32)

    hb_ref[HALO:, :] = h
    t_seq = t_blk * tm + lax.broadcasted_iota(jnp.int32, (tm, 1), 0)
    gd = D // len(POOL_WINDOWS)
    ys = []
    for gi, w in enumerate(POOL_WINDOWS):
        c0, c1 = gi * gd, (gi + 1) * gd
        a = hb_ref[:, c0:c1]
        span = 1
        while span < w:
            a = a[span:, :] + a[:-span, :]
            span *= 2
        tsum = a[HALO - (w - 1):, :]
        cnt = jnp.minimum(t_seq + 1, w).astype(f32)
        mixed = tsum / cnt - h[:, c0:c1]
        ys.append(_dot(mixed.astype(bf16), wp_ref[gi]))
    hb_ref[0:HALO, :] = h[tm - HALO:, :]
    y = jnp.concatenate(ys, axis=1) * ps_ref[...]
    _post_mixer(x2, y, mod_ref, ln_ref, wr_hi_ref, wr_lo_ref, br_ref, xo_ref, hm_ref, cls_ref)


def _pool_layer(x2, yprev, modp, lnp, mod, wp, ps, ln, wr_hi, wr_lo, br, S, tm):
    N, D = x2.shape
    spb = S // tm
    const = lambda shape: pl.BlockSpec(shape, lambda i: (0,) * len(shape))
    out_specs, out_shapes = _post_out_specs(N, D, tm)
    return pl.pallas_call(
        functools.partial(_pool_kernel, spb=spb),
        grid=(N // tm,),
        in_specs=[
            pl.BlockSpec((tm, D), lambda i: (i, 0)),
            pl.BlockSpec((tm * Y_ROWS, LANES), lambda i: (i, 0)),
            pl.BlockSpec((None, 8, D), lambda i: (i // spb, 0, 0)),
            const(lnp.shape),
            pl.BlockSpec((None, 8, D), lambda i: (i // spb, 0, 0)),
            const(wp.shape), const(ps.shape), const(ln.shape),
            const(wr_hi.shape), const(wr_lo.shape), const(br.shape),
        ],
        out_specs=out_specs,
        out_shape=out_shapes,
        scratch_shapes=[pltpu.VMEM((HALO + tm, D), f32)],
        compiler_params=_cparams(("arbitrary",)),
        name="pool_mixer_router",
    )(x2, yprev, modp, lnp, mod, wp, ps, ln, wr_hi, wr_lo, br)


def _final_kernel(x_ref, y_ref, mod_ref, ln_ref, o_ref):
    tm, D = x_ref.shape
    y = _rows_load(y_ref, tm, Y_ROWS, D)
    o_ref[...] = _layer_norm(DN_ALPHA * x_ref[...] + (1.0 + mod_ref[5:6, :]) * y,
                             ln_ref[0:1, :], ln_ref[1:2, :])


def _final_merge(x2, y, mod, ln, S, tm):
    N, D = x2.shape
    spb = S // tm
    return pl.pallas_call(
        _final_kernel,
        grid=(N // tm,),
        in_specs=[
            pl.BlockSpec((tm, D), lambda i: (i, 0)),
            pl.BlockSpec((tm * Y_ROWS, LANES), lambda i: (i, 0)),
            pl.BlockSpec((None, 8, D), lambda i: (i // spb, 0, 0)),
            pl.BlockSpec(ln.shape, lambda i: (0, 0)),
        ],
        out_specs=pl.BlockSpec((tm, D), lambda i: (i, 0)),
        out_shape=jax.ShapeDtypeStruct((N, D), f32),
        compiler_params=_cparams(("arbitrary",)),
        name="final_merge",
    )(x2, y, mod, ln)


def _rank_kernel(cls_ref, rank_ref, cnt_ref, carry_ref, earlier_ref):
    tr = cls_ref.shape[0]

    @pl.when(pl.program_id(0) == 0)
    def _():
        carry_ref[...] = jnp.zeros(carry_ref.shape, f32)
        r = lax.broadcasted_iota(jnp.int32, (tr, tr), 0)
        c = lax.broadcasted_iota(jnp.int32, (tr, tr), 1)
        earlier_ref[...] = (c < r).astype(bf16)

    lane = lax.broadcasted_iota(jnp.int32, (tr, LANES), 1)
    onehot = (cls_ref[...] == lane)
    oh = onehot.astype(bf16)
    before = _dot(earlier_ref[...], oh) + carry_ref[0:1, :]
    rank_ref[...] = jnp.sum(jnp.where(onehot, before, 0.0), axis=-1, keepdims=True).astype(jnp.int32)
    total = carry_ref[0:1, :] + jnp.sum(oh.astype(f32), axis=0, keepdims=True)
    carry_ref[...] = jnp.broadcast_to(total, carry_ref.shape)
    cnt_ref[...] = jnp.broadcast_to(total, cnt_ref.shape).astype(jnp.int32)


def _class_ranks(cls, tr):
    N = cls.shape[0]
    return pl.pallas_call(
        _rank_kernel,
        grid=(N // tr,),
        in_specs=[pl.BlockSpec((tr, 1), lambda i: (i, 0))],
        out_specs=[
            pl.BlockSpec((tr, 1), lambda i: (i, 0)),
            pl.BlockSpec((SUBLANES, LANES), lambda i: (0, 0)),
        ],
        out_shape=[
            jax.ShapeDtypeStruct((N, 1), jnp.int32),
            jax.ShapeDtypeStruct((SUBLANES, LANES), jnp.int32),
        ],
        scratch_shapes=[pltpu.VMEM((SUBLANES, LANES), f32), pltpu.VMEM((tr, tr), bf16)],
        compiler_params=_cparams(("arbitrary",)),
        name="class_ranks",
    )(cls)


def _slot_token_kernel(off_ref, cls_ref, rank_ref, inv_ref, *, chunk):
    base = pl.program_id(0) * chunk

    @pl.when(pl.program_id(0) == 0)
    def _():
        def clear(s, c):
            inv_ref[s] = 0
            return c
        lax.fori_loop(0, inv_ref.shape[0], clear, 0, unroll=32)

    def body(t, c):
        inv_ref[off_ref[cls_ref[t]] + rank_ref[t]] = base + t
        return c

    lax.fori_loop(0, chunk, body, 0, unroll=16)


def _slot_tokens(off, cls1, rank1, n_slots, chunk):
    N = cls1.shape[0]
    return pl.pallas_call(
        functools.partial(_slot_token_kernel, chunk=chunk),
        grid_spec=pltpu.PrefetchScalarGridSpec(
            num_scalar_prefetch=1,
            grid=(N // chunk,),
            in_specs=[
                pl.BlockSpec((chunk,), lambda i, off: (i,), memory_space=pltpu.SMEM),
                pl.BlockSpec((chunk,), lambda i, off: (i,), memory_space=pltpu.SMEM),
            ],
            out_specs=pl.BlockSpec((n_slots,), lambda i, off: (0,), memory_space=pltpu.SMEM),
        ),
        out_shape=jax.ShapeDtypeStruct((n_slots,), jnp.int32),
        compiler_params=_cparams(("arbitrary",)),
        name="slot_tokens",
    )(off, cls1, rank1)


def _moe_kernel(te1_ref, te2_ref, cnt_ref, inv_ref, hm_ref, *refs):
    weights = refs[:4 * STEP_TILES]
    y_ref = refs[4 * STEP_TILES]
    scratch = refs[4 * STEP_TILES + 1:]
    hbufs = scratch[:MOE_SLOTS]
    ybufs = scratch[MOE_SLOTS:2 * MOE_SLOTS]
    gsem, ssem = scratch[2 * MOE_SLOTS:]
    ff, D = weights[1].shape
    i = pl.program_id(0)

    def counts(group):
        first = jnp.maximum(group, 0) * STEP_TILES
        return [jnp.where(group >= 0, cnt_ref[first + k], 0) for k in range(STEP_TILES)]

    n_cur, n_prev, n_ahead, n_old = counts(i), counts(i - 1), counts(i + 2), counts(i - MOE_SLOTS)

    def token_rows(ref, t, pitch):
        return ref.at[pl.ds(pl.multiple_of(t * pitch, pitch), pitch)]

    def start_gather(tok, r, hb, sem):
        pltpu.make_async_copy(token_rows(hm_ref, tok, HM_ROWS), token_rows(hb, r, HM_ROWS), sem).start()

    def start_scatter(tok, r, yb, sem):
        pltpu.make_async_copy(token_rows(yb, r, Y_ROWS), token_rows(y_ref, tok, Y_ROWS), sem).start()

    def wait_tokens(src, dst, sem, n, pitch):
        @pl.when(n > 0)
        def _():
            rows = pl.multiple_of(n * pitch, pitch)
            pltpu.make_async_copy(src.at[pl.ds(0, rows)], dst.at[pl.ds(0, rows)], sem).wait()

    @pl.when(i == 0)
    def _():
        for g in range(MOE_SLOTS - 1):
            for k in range(STEP_TILES):
                def first(r, c):
                    slot = (g * STEP_TILES + k) * ROW_TILE + r
                    start_gather(inv_ref[slot], k * ROW_TILE + r, hbufs[g], gsem.at[g])
                    return c
                lax.fori_loop(0, cnt_ref[g * STEP_TILES + k], first, 0)

    def step(cur):
        far = (cur + MOE_SLOTS - 1) % MOE_SLOTS
        hb_cur, yb_cur = hbufs[cur], ybufs[cur]
        wait_tokens(hm_ref, hb_cur, gsem.at[cur], sum(n_cur), HM_ROWS)
        wait_tokens(yb_cur, y_ref, ssem.at[cur], sum(n_old), Y_ROWS)

        @pl.when((sum(n_cur) > 0) | (sum(n_prev) > 0))
        def _():
            ahead = (i + 2) * (STEP_TILES * ROW_TILE)
            prev = jnp.maximum(i - 1, 0) * (STEP_TILES * ROW_TILE)
            for k in range(STEP_TILES):
                for r in range(ROW_TILE):
                    row = k * ROW_TILE + r
                    tok_ahead = inv_ref[ahead + row]
                    tok_prev = inv_ref[prev + row]

                    @pl.when(r < n_ahead[k])
                    def _():
                        start_gather(tok_ahead, row, hbufs[far], gsem.at[far])

                    @pl.when(r < n_prev[k])
                    def _():
                        start_scatter(tok_prev, row, ybufs[far], ssem.at[far])

            for k in range(STEP_TILES):
                wgu1_ref, wd1_ref, wgu2_ref, wd2_ref = weights[4 * k:4 * k + 4]
                hb = hb_cur.at[pl.ds(k * ROW_TILE * HM_ROWS, ROW_TILE * HM_ROWS)]
                h = _rows_load(hb, ROW_TILE, HM_ROWS, D).astype(bf16)

                def hidden(gu, gate_row):
                    w = hb[pl.ds(gate_row, ROW_TILE, stride=HM_ROWS), :]
                    g, u = gu[:, :ff], gu[:, ff:]
                    he = g * (1.0 / (1.0 + jnp.exp(-g))) * u
                    return (he * jnp.concatenate([w] * (ff // LANES), axis=1)).astype(bf16)

                gu1 = _dot(h, wgu1_ref[...])
                gu2 = _dot(h, wgu2_ref[...])
                y = (_dot(hidden(gu1, D // LANES), wd1_ref[...])
                     + _dot(hidden(gu2, D // LANES + 1), wd2_ref[...]))
                _rows_store(yb_cur.at[pl.ds(k * ROW_TILE * Y_ROWS, ROW_TILE * Y_ROWS)], y, Y_ROWS)

    for cur in range(MOE_SLOTS):
        @pl.when(i % MOE_SLOTS == cur)
        def _():
            step(cur)


def _moe(te1, te2, cnt, inv, hm, wgu, wd, layer, n_steps):
    N = hm.shape[0] // HM_ROWS
    _, E, D, ff2 = wgu.shape
    ff = ff2 // 2

    def wmap(which, k):
        def f(i, te1, te2, cnt, inv):
            return (layer, (te1, te2)[which][i * STEP_TILES + k], 0, 0)
        return f

    w_specs, w_args = [], []
    for k in range(STEP_TILES):
        for which in range(2):
            w_specs += [pl.BlockSpec((None, None, D, ff2), wmap(which, k)),
                        pl.BlockSpec((None, None, ff, D), wmap(which, k))]
            w_args += [wgu, wd]

    group = STEP_TILES * ROW_TILE
    return pl.pallas_call(
        _moe_kernel,
        grid_spec=pltpu.PrefetchScalarGridSpec(
            num_scalar_prefetch=4,
            grid=(n_steps,),
            in_specs=[pl.BlockSpec(memory_space=pl.ANY)] + w_specs,
            out_specs=pl.BlockSpec(memory_space=pl.ANY),
            scratch_shapes=(
                [pltpu.VMEM((group * HM_ROWS, LANES), f32)] * MOE_SLOTS
                + [pltpu.VMEM((group * Y_ROWS, LANES), f32)] * MOE_SLOTS
                + [pltpu.SemaphoreType.DMA((MOE_SLOTS,)), pltpu.SemaphoreType.DMA((MOE_SLOTS,))]
            ),
        ),
        out_shape=jax.ShapeDtypeStruct((N * Y_ROWS, LANES), f32),
        compiler_params=_cparams(("arbitrary",)),
        name="pair_moe",
    )(te1, te2, cnt, inv, hm, *w_args)


def _pair_tables():
    lo = np.zeros((N_CLASSES,), np.int32)
    hi = np.zeros((N_CLASSES,), np.int32)
    for g in range(MOE_GROUPS):
        p = 0
        for a in range(EXPERTS_PER_GROUP):
            for b in range(a + 1, EXPERTS_PER_GROUP):
                lo[g * PAIRS_PER_GROUP + p] = g * EXPERTS_PER_GROUP + a
                hi[g * PAIRS_PER_GROUP + p] = g * EXPERTS_PER_GROUP + b
                p += 1
    return lo, hi


def _moe_layer(hm, cls, wgu, wd, layer, chunk, tr):
    N = cls.shape[0]
    n_groups = -(-(N // ROW_TILE + N_CLASSES) // STEP_TILES)
    n_tiles = n_groups * STEP_TILES
    n_steps = n_groups + MOE_SLOTS
    rank, cnt = _class_ranks(cls, tr)
    counts = cnt[0, :N_CLASSES]
    tiles = (counts + ROW_TILE - 1) // ROW_TILE
    tile_end = jnp.cumsum(tiles)
    tile_start = tile_end - tiles
    off = jnp.zeros((LANES,), jnp.int32).at[:N_CLASSES].set(tile_start * ROW_TILE)
    nused = tile_end[-1]
    step = jnp.arange((n_steps + 2) * STEP_TILES, dtype=jnp.int32)
    tile_cls = jnp.sum(tile_end[None, :] <= jnp.minimum(step, nused - 1)[:, None], axis=1)
    tile_cls = jnp.minimum(tile_cls, N_CLASSES - 1).astype(jnp.int32)
    lo_tab, hi_tab = _pair_tables()
    te1 = jnp.asarray(lo_tab)[tile_cls]
    te2 = jnp.asarray(hi_tab)[tile_cls]
    in_cls = (step - tile_start[tile_cls]) * ROW_TILE
    tile_cnt = jnp.where(step < nused, jnp.clip(counts[tile_cls] - in_cls, 0, ROW_TILE), 0).astype(jnp.int32)

    n_slots = -(-(n_groups + MOE_SLOTS) * STEP_TILES * ROW_TILE // SMEM_BLOCK) * SMEM_BLOCK
    inv = _slot_tokens(off, cls.reshape(N), rank.reshape(N), n_slots, chunk)
    return _moe(te1, te2, tile_cnt, inv, hm, wgu, wd, layer, n_steps)


def _swap_rope(w):
    half = QK_ROPE // 2
    return jnp.concatenate([-w[..., half:], w[..., :half]], axis=-1)


def _prep_mla_weights(w_in, w_uq, w_ukv):
    D = w_in.shape[0]
    w_kr = w_in[:, Q_LORA + KV_LORA:]
    w_ks = _swap_rope(w_kr)
    zeros = jnp.zeros((D, QK_NOPE), w_in.dtype)
    win = jnp.concatenate([w_in[:, :Q_LORA + KV_LORA], zeros, w_kr, w_kr, zeros, w_ks, w_ks], axis=1)
    uq = w_uq.reshape(Q_LORA, MLA_HEADS, QK_NOPE + QK_ROPE)
    rope = uq[..., QK_NOPE:]
    wuq = jnp.concatenate([uq[..., :QK_NOPE], rope, _swap_rope(rope)], axis=-1).reshape(Q_LORA, -1)
    ukv = w_ukv.reshape(KV_LORA, MLA_HEADS, QK_NOPE + V_HEAD)
    wkn = jnp.concatenate([ukv[..., :QK_NOPE], jnp.zeros((KV_LORA, MLA_HEADS, HEAD_W - QK_NOPE), w_ukv.dtype)],
                          axis=-1).reshape(KV_LORA, -1)
    wv = ukv[..., QK_NOPE:].reshape(KV_LORA, -1)
    return win.astype(bf16), wuq.T.astype(bf16), wkn.astype(bf16), wv.T.astype(bf16)


def _rope_inv_freq():
    inv = 1.0 / (ROPE_THETA ** (np.arange(0, QK_ROPE, 2, dtype=np.float32) / QK_ROPE))
    inv2 = np.concatenate([inv, inv]).astype(np.float32)
    return np.concatenate([np.zeros((QK_NOPE,), np.float32), inv2, inv2]).reshape(1, LANES)


def _prep_router(w_gr, b_gr, w_er, b_er):
    D = w_gr.shape[0]
    n = MOE_GROUPS + MOE_GROUPS * EXPERTS_PER_GROUP
    w = jnp.concatenate([w_gr, w_er, jnp.zeros((D, LANES - n), f32)], axis=1)
    b = jnp.concatenate([b_gr, b_er, jnp.zeros((LANES - n,), f32)]).reshape(1, LANES)
    hi, lo = _split_bf16(w)
    return hi, lo, b


def kernel(x, c, positions, w_mod, b_mod, ln_g, ln_b, w_in, q_norm_g, kv_norm_g, w_uq, w_ukv, w_o, w_pool,
           pool_scale, w_group_router, b_group_router, w_expert_router, b_expert_router, w_gate, w_up, w_down):
    B, S, D = x.shape
    N = B * S
    tm = min(512, S)
    tq = min(1024, S)
    tr = min(1024, N)
    chunk = min(2048, N)

    mod = _modulation(c, w_mod, b_mod)
    ln = jnp.stack([ln_g, ln_b], axis=2)
    x2 = x.reshape(N, D)

    win, wuqt, wkn, wvt = _prep_mla_weights(w_in[0], w_uq[0], w_ukv[0])
    qt, k, vt = _mla_proj(x2, positions.reshape(N, 1), mod[0], win, q_norm_g[0].reshape(1, -1),
                          kv_norm_g[0].reshape(1, -1), wuqt, wkn, wvt, jnp.asarray(_rope_inv_freq()),
                          B, S, tm, tq)
    o = _attention(qt, k, vt, tq)
    wr_hi, wr_lo, br = _prep_router(w_group_router[0], b_group_router[0], w_expert_router[0], b_expert_router[0])
    x1, hm, cls = _attn_out(o.reshape(N, D), x2, mod[0], w_o[0].astype(bf16), ln[0, 0], wr_hi, wr_lo, br, S, tm)
    wgu = jnp.concatenate([w_gate, w_up], axis=-1).astype(bf16)
    wd = w_down.astype(bf16)
    y0 = _moe_layer(hm, cls, wgu, wd, 0, chunk, tr)

    wr_hi, wr_lo, br = _prep_router(w_group_router[1], b_group_router[1], w_expert_router[1], b_expert_router[1])
    x3, hm, cls = _pool_layer(x1, y0, mod[0], ln[0, 1], mod[1], w_pool[0].astype(bf16),
                              pool_scale[0].reshape(1, D), ln[1, 0], wr_hi, wr_lo, br, S, tm)
    y1 = _moe_layer(hm, cls, wgu, wd, 1, chunk, tr)

    out = _final_merge(x3, y1, mod[1], ln[1, 1], S, tm)
    return out.reshape(B, S, D)
```

```python
import functools

import numpy as np
import jax
import jax.numpy as jnp
from jax import lax
from jax.experimental import pallas as pl
from jax.experimental.pallas import tpu as pltpu

MLA_HEADS = 16
Q_LORA = 384
KV_LORA = 256
QK_NOPE = 64
QK_ROPE = 32
V_HEAD = 64
ROPE_THETA = 10000.0
POOL_WINDOWS = (2, 4, 8, 16)
MOE_GROUPS = 4
EXPERTS_PER_GROUP = 8
DEPTH = 2
DN_ALPHA = (2.0 * DEPTH) ** 0.25
LN_EPS = 1e-5
RMS_EPS = 1e-6
LOG2E = 1.4426950408889634

LANES = 128
SUBLANES = 8
VMEM_LIMIT = 56 * 1024 * 1024

HEAD_W = 128
PAIRS_PER_GROUP = EXPERTS_PER_GROUP * (EXPERTS_PER_GROUP - 1) // 2
N_CLASSES = MOE_GROUPS * PAIRS_PER_GROUP
ROW_TILE = 128
HM_ROWS = 16
Y_ROWS = 8
MOE_SLOTS = 3
GATHER_BUFS = 3
SMEM_BLOCK = 1024
HALO = 16

f32 = jnp.float32
bf16 = jnp.bfloat16


def _cparams(sem):
    return pltpu.CompilerParams(dimension_semantics=sem, vmem_limit_bytes=VMEM_LIMIT)


def _split_bf16(a):
    hi = a.astype(bf16)
    lo = (a - hi.astype(f32)).astype(bf16)
    return hi, lo


def _dot(a, b):
    return jnp.dot(a, b, preferred_element_type=f32)


def _dot3(a_hi, a_lo, b_hi, b_lo):
    return _dot(a_hi, b_hi) + (_dot(a_lo, b_hi) + _dot(a_hi, b_lo))


def _rows_load(ref, n, pitch, width):
    return jnp.concatenate([ref[pl.ds(s, n, stride=pitch), :] for s in range(width // LANES)], axis=1)


def _rows_store(ref, val, pitch):
    n, width = val.shape
    for s in range(width // LANES):
        ref[pl.ds(s, n, stride=pitch), :] = val[:, s * LANES:(s + 1) * LANES]


def _layer_norm(x, g, b):
    mu = jnp.mean(x, axis=-1, keepdims=True)
    xc = x - mu
    var = jnp.mean(xc * xc, axis=-1, keepdims=True)
    return xc * lax.rsqrt(var + LN_EPS) * g + b


def _mod_kernel(c_ref, w_ref, b_ref, o_ref):
    c = c_ref[...]
    ca = c * (1.0 / (1.0 + jnp.exp(-c)))
    a_hi, a_lo = _split_bf16(ca)
    w_hi, w_lo = _split_bf16(w_ref[...])
    o_ref[...] = _dot3(a_hi, a_lo, w_hi, w_lo) + b_ref[...]


def _modulation(c, w_mod, b_mod):
    B, D = c.shape
    depth = w_mod.shape[0]
    out = pl.pallas_call(
        _mod_kernel,
        grid=(depth, 6),
        in_specs=[
            pl.BlockSpec((B, D), lambda i, j: (0, 0)),
            pl.BlockSpec((None, D, D), lambda i, j: (i, 0, j)),
            pl.BlockSpec((None, None, 1, D), lambda i, j: (i, j, 0, 0)),
        ],
        out_specs=pl.BlockSpec((None, None, B, D), lambda i, j: (i, j, 0, 0)),
        out_shape=jax.ShapeDtypeStruct((depth, 6, B, D), f32),
        compiler_params=_cparams(("arbitrary", "arbitrary")),
        name="adaln_mod",
    )(c, w_mod, b_mod.reshape(depth, 6, 1, D))
    out = jnp.transpose(out, (0, 2, 1, 3))
    return jnp.pad(out, ((0, 0), (0, 0), (0, 2), (0, 0)))


def _mla_proj_kernel(x_ref, pos_ref, mod_ref, win_ref, qg_ref, kg_ref, wuqt_ref, wkn_ref, wvt_ref,
                     invf_ref, qt_ref, k_ref, vt_ref):
    x = x_ref[...]
    h = x * (1.0 + mod_ref[1:2, :]) + mod_ref[0:1, :]
    z = _dot(h.astype(bf16), win_ref[...])
    zq = z[:, :Q_LORA]
    zkv = z[:, Q_LORA:Q_LORA + KV_LORA]
    za = z[:, Q_LORA + KV_LORA:Q_LORA + KV_LORA + LANES]
    zb = z[:, Q_LORA + KV_LORA + LANES:]
    q_lat = zq * lax.rsqrt(jnp.mean(zq * zq, axis=-1, keepdims=True) + RMS_EPS) * qg_ref[...]
    kv_lat = zkv * lax.rsqrt(jnp.mean(zkv * zkv, axis=-1, keepdims=True) + RMS_EPS) * kg_ref[...]

    ang = pos_ref[...].astype(f32) * invf_ref[...]
    cs = jnp.cos(ang)
    sn = jnp.sin(ang)
    lane = lax.broadcasted_iota(jnp.int32, cs.shape, 1)
    scale = (QK_NOPE + QK_ROPE) ** -0.5 * LOG2E
    tqn = scale * jnp.where(lane < QK_NOPE + QK_ROPE, cs, sn)
    kr = za * cs + zb * sn

    kn = _dot(kv_lat.astype(bf16), wkn_ref[...])
    for hd in range(MLA_HEADS):
        k_ref[hd] = (kn[:, hd * HEAD_W:(hd + 1) * HEAD_W] + kr).astype(bf16)

    tqt = tqn.T
    qt = _dot(wuqt_ref[...], q_lat.T.astype(bf16))
    for hd in range(MLA_HEADS):
        qt_ref[hd] = (qt[hd * HEAD_W:(hd + 1) * HEAD_W, :] * tqt).astype(bf16)
    vt_ref[...] = _dot(wvt_ref[...], kv_lat.T.astype(bf16)).astype(bf16)


def _mla_proj(x2, pos2, mod, win, qg, kg, wuqt, wkn, wvt, invf, B, S, tm, tq):
    N, D = x2.shape
    spb = S // tm
    per_q = tq // tm
    const = lambda shape: pl.BlockSpec(shape, lambda i: (0,) * len(shape))
    nq = S // tq
    return pl.pallas_call(
        _mla_proj_kernel,
        grid=(N // tm,),
        in_specs=[
            pl.BlockSpec((tm, D), lambda i: (i, 0)),
            pl.BlockSpec((tm, 1), lambda i: (i, 0)),
            pl.BlockSpec((None, 8, D), lambda i: (i // spb, 0, 0)),
            const(win.shape), const(qg.shape), const(kg.shape), const(wuqt.shape), const(wkn.shape),
            const(wvt.shape), const(invf.shape),
        ],
        out_specs=[
            pl.BlockSpec((None, MLA_HEADS, None, HEAD_W, tm),
                         lambda i: (i // spb, 0, (i % spb) // per_q, 0, i % per_q)),
            pl.BlockSpec((None, MLA_HEADS, tm, HEAD_W), lambda i: (i // spb, 0, i % spb, 0)),
            pl.BlockSpec((None, None, MLA_HEADS * V_HEAD, tm),
                         lambda i: (i // spb, (i % spb) // per_q, 0, i % per_q)),
        ],
        out_shape=[
            jax.ShapeDtypeStruct((B, MLA_HEADS, nq, HEAD_W, tq), bf16),
            jax.ShapeDtypeStruct((B, MLA_HEADS, S, HEAD_W), bf16),
            jax.ShapeDtypeStruct((B, nq, MLA_HEADS * V_HEAD, tq), bf16),
        ],
        compiler_params=_cparams(("arbitrary",)),
        name="mla_proj",
    )(x2, pos2, mod, win, qg, kg, wuqt, wkn, wvt, invf)


ONES_ROWS = 16


def _attn_kernel(qt_ref, k_ref, vt_ref, o_ref, vx_ref, m_ref, acc_ref, ot_ref, *, tq):
    nq = qt_ref.shape[1]
    for hh in range(2):
        for j in range(nq):
            vx_ref[hh, j, :V_HEAD, :] = vt_ref[j, hh * V_HEAD:(hh + 1) * V_HEAD, :]
            vx_ref[hh, j, V_HEAD:, :] = jnp.ones((ONES_ROWS, tq), bf16)

    def kv_step(qi, j, masked):
        k0 = pl.multiple_of(j * tq, tq)
        scores = [_dot(k_ref[hh, pl.ds(k0, tq), :], qt_ref[hh, qi]) for hh in range(2)]
        for hh in range(2):
            s = scores[hh]
            if masked:
                key = lax.broadcasted_iota(jnp.int32, (tq, tq), 0)
                qry = lax.broadcasted_iota(jnp.int32, (tq, tq), 1)
                s = jnp.where(key <= qry, s, -jnp.inf)
            m_old = m_ref[hh]
            m_new = jnp.maximum(m_old, jnp.max(s, axis=0, keepdims=True))
            alpha = jnp.exp2(m_old - m_new)
            p = jnp.exp2(s - m_new[0:1, :])
            pv = _dot(vx_ref[hh, j], p.astype(bf16))
            acc_ref[hh] = acc_ref[hh] * alpha[0:1, :] + pv
            m_ref[hh] = m_new

    def q_body(qi, carry):
        m_ref[...] = jnp.full(m_ref.shape, -jnp.inf, f32)
        acc_ref[...] = jnp.zeros(acc_ref.shape, f32)

        def full_body(j, c):
            kv_step(qi, j, False)
            return c

        lax.fori_loop(0, qi, full_body, 0)
        kv_step(qi, qi, True)
        for hh in range(2):
            acc = acc_ref[hh]
            ot_ref[hh * V_HEAD:(hh + 1) * V_HEAD, :] = acc[:V_HEAD, :] * (1.0 / acc[V_HEAD:V_HEAD + 1, :])
        o_ref[pl.ds(pl.multiple_of(qi * tq, tq), tq), :] = ot_ref[...].T.astype(bf16)
        return carry

    lax.fori_loop(0, nq, q_body, 0)


def _attention(qt, k, vt, tq):
    B, H, nq, W, _ = qt.shape
    S = nq * tq
    return pl.pallas_call(
        functools.partial(_attn_kernel, tq=tq),
        grid=(B, H // 2),
        in_specs=[
            pl.BlockSpec((None, 2, nq, W, tq), lambda b, p: (b, p, 0, 0, 0)),
            pl.BlockSpec((None, 2, S, W), lambda b, p: (b, p, 0, 0)),
            pl.BlockSpec((None, nq, 2 * V_HEAD, tq), lambda b, p: (b, 0, p, 0)),
        ],
        out_specs=pl.BlockSpec((None, S, LANES), lambda b, p: (b, 0, p)),
        out_shape=jax.ShapeDtypeStruct((B, S, H * V_HEAD), bf16),
        scratch_shapes=[
            pltpu.VMEM((2, nq, V_HEAD + ONES_ROWS, tq), bf16),
            pltpu.VMEM((2, SUBLANES, tq), f32),
            pltpu.VMEM((2, V_HEAD + ONES_ROWS, tq), f32),
            pltpu.VMEM((2 * V_HEAD, tq), f32),
        ],
        compiler_params=_cparams(("arbitrary", "arbitrary")),
        name="mla_attention",
    )(qt, k, vt)


def _post_mixer(x, y, mod_ref, ln_ref, wr_hi_ref, wr_lo_ref, br_ref, xo_ref, hm_ref, cls_ref):
    tm, D = x.shape
    x1 = _layer_norm(DN_ALPHA * x + (1.0 + mod_ref[2:3, :]) * y, ln_ref[0:1, :], ln_ref[1:2, :])
    xo_ref[...] = x1
    h2 = x1 * (1.0 + mod_ref[4:5, :]) + mod_ref[3:4, :]
    h_hi, h_lo = _split_bf16(h2)
    logits = _dot3(h_hi, h_lo, wr_hi_ref[...], wr_lo_ref[...]) + br_ref[...]

    lane = lax.broadcasted_iota(jnp.int32, logits.shape, 1).astype(f32)
    neg = -jnp.inf
    far = float(LANES)
    is_g = lane < MOE_GROUPS
    gl = jnp.where(is_g, logits, neg)
    gmax = jnp.max(gl, axis=-1, keepdims=True)
    gidx = jnp.min(jnp.where(gl == gmax, lane, far), axis=-1, keepdims=True)
    g_p = 1.0 / jnp.sum(jnp.where(is_g, jnp.exp(logits - gmax), 0.0), axis=-1, keepdims=True)
    base = MOE_GROUPS + EXPERTS_PER_GROUP * gidx
    el = jnp.where((lane >= base) & (lane < base + EXPERTS_PER_GROUP), logits, neg)
    t1 = jnp.max(el, axis=-1, keepdims=True)
    i1 = jnp.min(jnp.where(el == t1, lane, far), axis=-1, keepdims=True)
    el2 = jnp.where(lane == i1, neg, el)
    t2 = jnp.max(el2, axis=-1, keepdims=True)
    i2 = jnp.min(jnp.where(el2 == t2, lane, far), axis=-1, keepdims=True)
    e = jnp.exp(t2 - t1)
    w1 = g_p / (1.0 + e)
    w2 = g_p * e / (1.0 + e)
    a = i1 - base
    b = i2 - base
    a_first = a < b
    lo = jnp.where(a_first, a, b)
    hi = jnp.where(a_first, b, a)
    w_lo = jnp.where(a_first, w1, w2)
    w_hi = jnp.where(a_first, w2, w1)
    pair = lo * (2 * EXPERTS_PER_GROUP - 1 - lo) * 0.5 + (hi - lo - 1.0)
    cls_ref[...] = (gidx * PAIRS_PER_GROUP + pair).astype(jnp.int32)

    _rows_store(hm_ref, h2, HM_ROWS)
    hm_ref[pl.ds(D // LANES, tm, stride=HM_ROWS), :] = jnp.broadcast_to(w_lo, (tm, LANES))
    hm_ref[pl.ds(D // LANES + 1, tm, stride=HM_ROWS), :] = jnp.broadcast_to(w_hi, (tm, LANES))
    for s in range(D // LANES + 2, HM_ROWS):
        hm_ref[pl.ds(s, tm, stride=HM_ROWS), :] = jnp.zeros((tm, LANES), f32)


def _attn_out_kernel(o_ref, x_ref, mod_ref, wo_ref, ln_ref, wr_hi_ref, wr_lo_ref, br_ref,
                     xo_ref, hm_ref, cls_ref):
    y = _dot(o_ref[...], wo_ref[...])
    _post_mixer(x_ref[...], y, mod_ref, ln_ref, wr_hi_ref, wr_lo_ref, br_ref, xo_ref, hm_ref, cls_ref)


def _post_out_specs(N, D, tm):
    specs = [
        pl.BlockSpec((tm, D), lambda i, *_: (i, 0)),
        pl.BlockSpec((tm * HM_ROWS, LANES), lambda i, *_: (i, 0)),
        pl.BlockSpec((tm, 1), lambda i, *_: (i, 0)),
    ]
    shapes = [
        jax.ShapeDtypeStruct((N, D), f32),
        jax.ShapeDtypeStruct((N * HM_ROWS, LANES), f32),
        jax.ShapeDtypeStruct((N, 1), jnp.int32),
    ]
    return specs, shapes


def _attn_out(o2, x2, mod, wo, ln, wr_hi, wr_lo, br, S, tm):
    N, D = x2.shape
    spb = S // tm
    const = lambda shape: pl.BlockSpec(shape, lambda i: (0,) * len(shape))
    out_specs, out_shapes = _post_out_specs(N, D, tm)
    return pl.pallas_call(
        _attn_out_kernel,
        grid=(N // tm,),
        in_specs=[
            pl.BlockSpec((tm, D), lambda i: (i, 0)),
            pl.BlockSpec((tm, D), lambda i: (i, 0)),
            pl.BlockSpec((None, 8, D), lambda i: (i // spb, 0, 0)),
            const(wo.shape), const(ln.shape), const(wr_hi.shape), const(wr_lo.shape), const(br.shape),
        ],
        out_specs=out_specs,
        out_shape=out_shapes,
        compiler_params=_cparams(("arbitrary",)),
        name="attn_out_router",
    )(o2, x2, mod, wo, ln, wr_hi, wr_lo, br)


def _gathered_rows(pos_ref, ys_ref, ybufs, sem, tm, compute):
    i = pl.program_id(0)
    n_steps = pl.num_programs(0)
    n_buf = len(ybufs)

    def start(slot, r, yb, s):
        src = ys_ref.at[pl.ds(pl.multiple_of(slot * Y_ROWS, Y_ROWS), Y_ROWS)]
        pltpu.make_async_copy(src, yb.at[pl.ds(r * Y_ROWS, Y_ROWS)], s).start()

    @pl.when(i == 0)
    def _():
        for g in range(n_buf - 1):
            def first(r, c):
                start(pos_ref[g * tm + r], r, ybufs[g], sem.at[g])
                return c
            lax.fori_loop(0, jnp.where(g < n_steps, tm, 0), first, 0)

    for cur in range(n_buf):
        @pl.when(i % n_buf == cur)
        def _():
            far = (cur + n_buf - 1) % n_buf
            yb = ybufs[cur]
            pltpu.make_async_copy(ys_ref.at[pl.ds(0, tm * Y_ROWS)], yb, sem.at[cur]).wait()
            ahead_ok = i + (n_buf - 1) < n_steps
            base = (i + (n_buf - 1)) * tm
            for r in range(tm):
                slot = pos_ref[base + r]

                @pl.when(ahead_ok)
                def _():
                    start(slot, r, ybufs[far], sem.at[far])
            compute(yb)


def _pool_kernel(pos_ref, x_ref, ys_ref, modp_ref, lnp_ref, mod_ref, wp_ref, ps_ref, ln_ref,
                 wr_hi_ref, wr_lo_ref, br_ref, xo_ref, hm_ref, cls_ref, hb_ref, *scratch, spb):
    tm = x_ref.shape[0]
    ybufs, sem = scratch[:-1], scratch[-1]

    def compute(yb):
        _pool_compute(x_ref, yb, modp_ref, lnp_ref, mod_ref, wp_ref, ps_ref, ln_ref,
                      wr_hi_ref, wr_lo_ref, br_ref, xo_ref, hm_ref, cls_ref, hb_ref, spb)

    _gathered_rows(pos_ref, ys_ref, ybufs, sem, tm, compute)


def _pool_compute(x_ref, y_ref, modp_ref, lnp_ref, mod_ref, wp_ref, ps_ref, ln_ref,
                  wr_hi_ref, wr_lo_ref, br_ref, xo_ref, hm_ref, cls_ref, hb_ref, spb):
    tm, D = x_ref.shape
    i = pl.program_id(0)
    t_blk = i % spb
    yprev = _rows_load(y_ref, tm, Y_ROWS, D)
    x2 = _layer_norm(DN_ALPHA * x_ref[...] + (1.0 + modp_ref[5:6, :]) * yprev,
                     lnp_ref[0:1, :], lnp_ref[1:2, :])
    h = x2 * (1.0 + mod_ref[1:2, :]) + mod_ref[0:1, :]

    @pl.when(t_blk == 0)
    def _():
        hb_ref[0:HALO, :] = jnp.zeros((HALO, D), f32)

    hb_ref[HALO:, :] = h
    t_seq = t_blk * tm + lax.broadcasted_iota(jnp.int32, (tm, 1), 0)
    gd = D // len(POOL_WINDOWS)
    ys = []
    for gi, w in enumerate(POOL_WINDOWS):
        c0, c1 = gi * gd, (gi + 1) * gd
        a = hb_ref[:, c0:c1]
        span = 1
        while span < w:
            a = a[span:, :] + a[:-span, :]
            span *= 2
        tsum = a[HALO - (w - 1):, :]
        cnt = jnp.minimum(t_seq + 1, w).astype(f32)
        mixed = tsum / cnt - h[:, c0:c1]
        ys.append(_dot(mixed.astype(bf16), wp_ref[gi]))
    hb_ref[0:HALO, :] = h[tm - HALO:, :]
    y = jnp.concatenate(ys, axis=1) * ps_ref[...]
    _post_mixer(x2, y, mod_ref, ln_ref, wr_hi_ref, wr_lo_ref, br_ref, xo_ref, hm_ref, cls_ref)


def _gather_scratch(tm):
    return [pltpu.VMEM((tm * Y_ROWS, LANES), f32)] * GATHER_BUFS + [pltpu.SemaphoreType.DMA((GATHER_BUFS,))]


def _pool_layer(pos, x2, ys, modp, lnp, mod, wp, ps, ln, wr_hi, wr_lo, br, S, tm):
    N, D = x2.shape
    spb = S // tm
    const = lambda shape: pl.BlockSpec(shape, lambda i, pos: (0,) * len(shape))
    out_specs, out_shapes = _post_out_specs(N, D, tm)
    return pl.pallas_call(
        functools.partial(_pool_kernel, spb=spb),
        grid_spec=pltpu.PrefetchScalarGridSpec(
            num_scalar_prefetch=1,
            grid=(N // tm,),
            in_specs=[
                pl.BlockSpec((tm, D), lambda i, pos: (i, 0)),
                pl.BlockSpec(memory_space=pl.ANY),
                pl.BlockSpec((None, 8, D), lambda i, pos: (i // spb, 0, 0)),
                const(lnp.shape),
                pl.BlockSpec((None, 8, D), lambda i, pos: (i // spb, 0, 0)),
                const(wp.shape), const(ps.shape), const(ln.shape),
                const(wr_hi.shape), const(wr_lo.shape), const(br.shape),
            ],
            out_specs=out_specs,
            scratch_shapes=[pltpu.VMEM((HALO + tm, D), f32)] + _gather_scratch(tm),
        ),
        out_shape=out_shapes,
        compiler_params=_cparams(("arbitrary",)),
        name="pool_mixer_router",
    )(pos, x2, ys, modp, lnp, mod, wp, ps, ln, wr_hi, wr_lo, br)


def _final_kernel(pos_ref, x_ref, ys_ref, mod_ref, ln_ref, o_ref, *scratch):
    tm, D = x_ref.shape

    def compute(yb):
        y = _rows_load(yb, tm, Y_ROWS, D)
        o_ref[...] = _layer_norm(DN_ALPHA * x_ref[...] + (1.0 + mod_ref[5:6, :]) * y,
                                 ln_ref[0:1, :], ln_ref[1:2, :])

    _gathered_rows(pos_ref, ys_ref, scratch[:-1], scratch[-1], tm, compute)


def _final_merge(pos, x2, ys, mod, ln, S, tm):
    N, D = x2.shape
    spb = S // tm
    return pl.pallas_call(
        _final_kernel,
        grid_spec=pltpu.PrefetchScalarGridSpec(
            num_scalar_prefetch=1,
            grid=(N // tm,),
            in_specs=[
                pl.BlockSpec((tm, D), lambda i, pos: (i, 0)),
                pl.BlockSpec(memory_space=pl.ANY),
                pl.BlockSpec((None, 8, D), lambda i, pos: (i // spb, 0, 0)),
                pl.BlockSpec(ln.shape, lambda i, pos: (0, 0)),
            ],
            out_specs=pl.BlockSpec((tm, D), lambda i, pos: (i, 0)),
            scratch_shapes=_gather_scratch(tm),
        ),
        out_shape=jax.ShapeDtypeStruct((N, D), f32),
        compiler_params=_cparams(("arbitrary",)),
        name="final_merge",
    )(pos, x2, ys, mod, ln)


def _rank_kernel(cls_ref, rank_ref, cnt_ref, carry_ref, earlier_ref):
    tr = cls_ref.shape[0]

    @pl.when(pl.program_id(0) == 0)
    def _():
        carry_ref[...] = jnp.zeros(carry_ref.shape, f32)
        r = lax.broadcasted_iota(jnp.int32, (tr, tr), 0)
        c = lax.broadcasted_iota(jnp.int32, (tr, tr), 1)
        earlier_ref[...] = (c < r).astype(bf16)

    lane = lax.broadcasted_iota(jnp.int32, (tr, LANES), 1)
    onehot = (cls_ref[...] == lane)
    oh = onehot.astype(bf16)
    before = _dot(earlier_ref[...], oh) + carry_ref[0:1, :]
    rank_ref[...] = jnp.sum(jnp.where(onehot, before, 0.0), axis=-1, keepdims=True).astype(jnp.int32)
    total = carry_ref[0:1, :] + jnp.sum(oh.astype(f32), axis=0, keepdims=True)
    carry_ref[...] = jnp.broadcast_to(total, carry_ref.shape)
    cnt_ref[...] = jnp.broadcast_to(total, cnt_ref.shape).astype(jnp.int32)


def _class_ranks(cls, tr):
    N = cls.shape[0]
    return pl.pallas_call(
        _rank_kernel,
        grid=(N // tr,),
        in_specs=[pl.BlockSpec((tr, 1), lambda i: (i, 0))],
        out_specs=[
            pl.BlockSpec((tr, 1), lambda i: (i, 0)),
            pl.BlockSpec((SUBLANES, LANES), lambda i: (0, 0)),
        ],
        out_shape=[
            jax.ShapeDtypeStruct((N, 1), jnp.int32),
            jax.ShapeDtypeStruct((SUBLANES, LANES), jnp.int32),
        ],
        scratch_shapes=[pltpu.VMEM((SUBLANES, LANES), f32), pltpu.VMEM((tr, tr), bf16)],
        compiler_params=_cparams(("arbitrary",)),
        name="class_ranks",
    )(cls)


def _slot_token_kernel(pos_ref, inv_ref, *, chunk):
    base = pl.program_id(0) * chunk

    @pl.when(pl.program_id(0) == 0)
    def _():
        def clear(s, c):
            inv_ref[s] = 0
            return c
        lax.fori_loop(0, inv_ref.shape[0], clear, 0, unroll=32)

    def body(t, c):
        inv_ref[pos_ref[t]] = base + t
        return c

    lax.fori_loop(0, chunk, body, 0, unroll=16)


def _slot_tokens(pos, n_slots, chunk):
    N = pos.shape[0]
    return pl.pallas_call(
        functools.partial(_slot_token_kernel, chunk=chunk),
        grid=(N // chunk,),
        in_specs=[pl.BlockSpec((chunk,), lambda i: (i,), memory_space=pltpu.SMEM)],
        out_specs=pl.BlockSpec((n_slots,), lambda i: (0,), memory_space=pltpu.SMEM),
        out_shape=jax.ShapeDtypeStruct((n_slots,), jnp.int32),
        compiler_params=_cparams(("arbitrary",)),
        name="slot_tokens",
    )(pos)


def _moe_kernel(te1_ref, te2_ref, cnt_ref, inv_ref, hm_ref, wgu1_ref, wd1_ref, wgu2_ref, wd2_ref,
                ys_ref, *scratch):
    hbufs, gsem = scratch[:MOE_SLOTS], scratch[MOE_SLOTS]
    ff, D = wd1_ref.shape
    i = pl.program_id(0)
    n_cur = cnt_ref[i]
    n_ahead = cnt_ref[i + 2]

    def start_gather(tok, r, hb, sem):
        src = hm_ref.at[pl.ds(pl.multiple_of(tok * HM_ROWS, HM_ROWS), HM_ROWS)]
        pltpu.make_async_copy(src, hb.at[pl.ds(r * HM_ROWS, HM_ROWS)], sem).start()

    @pl.when(i == 0)
    def _():
        for t in range(MOE_SLOTS - 1):
            def first(r, c):
                src = hm_ref.at[pl.ds(pl.multiple_of(inv_ref[t * ROW_TILE + r] * HM_ROWS, HM_ROWS), HM_ROWS)]
                dst = hbufs[t].at[pl.ds(pl.multiple_of(r * HM_ROWS, HM_ROWS), HM_ROWS)]
                pltpu.make_async_copy(src, dst, gsem.at[t]).start()
                return c
            lax.fori_loop(0, cnt_ref[t], first, 0)

    def step(cur):
        far = (cur + MOE_SLOTS - 1) % MOE_SLOTS
        hb = hbufs[cur]

        @pl.when(n_cur > 0)
        def _():
            rows = pl.multiple_of(n_cur * HM_ROWS, HM_ROWS)
            pltpu.make_async_copy(hm_ref.at[pl.ds(0, rows)], hb.at[pl.ds(0, rows)], gsem.at[cur]).wait()

        @pl.when(n_cur > 0)
        def _():
            ahead = (i + 2) * ROW_TILE
            for r in range(ROW_TILE):
                tok = inv_ref[ahead + r]

                @pl.when(r < n_ahead)
                def _():
                    start_gather(tok, r, hbufs[far], gsem.at[far])

            valid = lax.broadcasted_iota(jnp.int32, (ROW_TILE, 1), 0) < n_cur
            h = jnp.where(valid, _rows_load(hb, ROW_TILE, HM_ROWS, D), 0.0).astype(bf16)

            def hidden(gu, gate_row):
                w = hb[pl.ds(gate_row, ROW_TILE, stride=HM_ROWS), :]
                w = jnp.where(valid, w, 0.0)
                g, u = gu[:, :ff], gu[:, ff:]
                he = g * (1.0 / (1.0 + jnp.exp(-g))) * u
                return (he * jnp.concatenate([w] * (ff // LANES), axis=1)).astype(bf16)

            gu1 = _dot(h, wgu1_ref[...])
            gu2 = _dot(h, wgu2_ref[...])
            y = _dot(hidden(gu1, D // LANES), wd1_ref[...]) + _dot(hidden(gu2, D // LANES + 1), wd2_ref[...])
            _rows_store(ys_ref, y, Y_ROWS)

        @pl.when(n_cur == 0)
        def _():
            ys_ref[...] = jnp.zeros(ys_ref.shape, f32)

    for cur in range(MOE_SLOTS):
        @pl.when(i % MOE_SLOTS == cur)
        def _():
            step(cur)


def _moe(te1, te2, cnt, inv, hm, wgu, wd, layer, n_tiles):
    _, E, D, ff2 = wgu.shape
    ff = ff2 // 2

    def wmap(which):
        def f(i, te1, te2, cnt, inv):
            return (layer, (te1, te2)[which][i], 0, 0)
        return f

    return pl.pallas_call(
        _moe_kernel,
        grid_spec=pltpu.PrefetchScalarGridSpec(
            num_scalar_prefetch=4,
            grid=(n_tiles,),
            in_specs=[
                pl.BlockSpec(memory_space=pl.ANY),
                pl.BlockSpec((None, None, D, ff2), wmap(0)),
                pl.BlockSpec((None, None, ff, D), wmap(0)),
                pl.BlockSpec((None, None, D, ff2), wmap(1)),
                pl.BlockSpec((None, None, ff, D), wmap(1)),
            ],
            out_specs=pl.BlockSpec((ROW_TILE * Y_ROWS, LANES), lambda i, te1, te2, cnt, inv: (i, 0)),
            scratch_shapes=([pltpu.VMEM((ROW_TILE * HM_ROWS, LANES), f32)] * MOE_SLOTS
                            + [pltpu.SemaphoreType.DMA((MOE_SLOTS,))]),
        ),
        out_shape=jax.ShapeDtypeStruct((n_tiles * ROW_TILE * Y_ROWS, LANES), f32),
        compiler_params=_cparams(("arbitrary",)),
        name="pair_moe",
    )(te1, te2, cnt, inv, hm, wgu, wd, wgu, wd)


def _pair_tables():
    lo = np.zeros((N_CLASSES,), np.int32)
    hi = np.zeros((N_CLASSES,), np.int32)
    for g in range(MOE_GROUPS):
        p = 0
        for a in range(EXPERTS_PER_GROUP):
            for b in range(a + 1, EXPERTS_PER_GROUP):
                lo[g * PAIRS_PER_GROUP + p] = g * EXPERTS_PER_GROUP + a
                hi[g * PAIRS_PER_GROUP + p] = g * EXPERTS_PER_GROUP + b
                p += 1
    return lo, hi


def _moe_layer(hm, cls, expert_w, layer, chunk, tr, tm):
    N = cls.shape[0]
    n_tiles = N // ROW_TILE + N_CLASSES
    rank, cnt = _class_ranks(cls, tr)
    counts = cnt[0, :N_CLASSES]
    tiles = (counts + ROW_TILE - 1) // ROW_TILE
    tile_end = jnp.cumsum(tiles)
    tile_start = tile_end - tiles
    off = jnp.zeros((LANES,), jnp.int32).at[:N_CLASSES].set(tile_start * ROW_TILE)
    nused = tile_end[-1]
    step = jnp.arange(n_tiles + MOE_SLOTS - 1, dtype=jnp.int32)
    tile_cls = jnp.sum(tile_end[None, :] <= jnp.minimum(step, nused - 1)[:, None], axis=1)
    tile_cls = jnp.minimum(tile_cls, N_CLASSES - 1).astype(jnp.int32)
    lo_tab, hi_tab = _pair_tables()
    te1 = jnp.asarray(lo_tab)[tile_cls]
    te2 = jnp.asarray(hi_tab)[tile_cls]
    in_cls = (step - tile_start[tile_cls]) * ROW_TILE
    tile_cnt = jnp.where(step < nused, jnp.clip(counts[tile_cls] - in_cls, 0, ROW_TILE), 0).astype(jnp.int32)

    n_slots = -(-(n_tiles + MOE_SLOTS - 1) * ROW_TILE // SMEM_BLOCK) * SMEM_BLOCK
    pos = off[cls.reshape(N)] + rank.reshape(N)
    inv = _slot_tokens(pos, n_slots, chunk)
    ys = _moe(te1, te2, tile_cnt, inv, hm, *expert_w, layer, n_tiles)
    return jnp.pad(pos, (0, (GATHER_BUFS - 1) * tm)), ys


def _swap_rope(w):
    half = QK_ROPE // 2
    return jnp.concatenate([-w[..., half:], w[..., :half]], axis=-1)


def _prep_mla_weights(w_in, w_uq, w_ukv):
    D = w_in.shape[0]
    w_kr = w_in[:, Q_LORA + KV_LORA:]
    w_ks = _swap_rope(w_kr)
    zeros = jnp.zeros((D, QK_NOPE), w_in.dtype)
    win = jnp.concatenate([w_in[:, :Q_LORA + KV_LORA], zeros, w_kr, w_kr, zeros, w_ks, w_ks], axis=1)
    uq = w_uq.reshape(Q_LORA, MLA_HEADS, QK_NOPE + QK_ROPE)
    rope = uq[..., QK_NOPE:]
    wuq = jnp.concatenate([uq[..., :QK_NOPE], rope, _swap_rope(rope)], axis=-1).reshape(Q_LORA, -1)
    ukv = w_ukv.reshape(KV_LORA, MLA_HEADS, QK_NOPE + V_HEAD)
    wkn = jnp.concatenate([ukv[..., :QK_NOPE], jnp.zeros((KV_LORA, MLA_HEADS, HEAD_W - QK_NOPE), w_ukv.dtype)],
                          axis=-1).reshape(KV_LORA, -1)
    wv = ukv[..., QK_NOPE:].reshape(KV_LORA, -1)
    return win.astype(bf16), wuq.T.astype(bf16), wkn.astype(bf16), wv.T.astype(bf16)


def _rope_inv_freq():
    inv = 1.0 / (ROPE_THETA ** (np.arange(0, QK_ROPE, 2, dtype=np.float32) / QK_ROPE))
    inv2 = np.concatenate([inv, inv]).astype(np.float32)
    return np.concatenate([np.zeros((QK_NOPE,), np.float32), inv2, inv2]).reshape(1, LANES)


def _prep_router(w_gr, b_gr, w_er, b_er):
    D = w_gr.shape[0]
    n = MOE_GROUPS + MOE_GROUPS * EXPERTS_PER_GROUP
    w = jnp.concatenate([w_gr, w_er, jnp.zeros((D, LANES - n), f32)], axis=1)
    b = jnp.concatenate([b_gr, b_er, jnp.zeros((LANES - n,), f32)]).reshape(1, LANES)
    hi, lo = _split_bf16(w)
    return hi, lo, b


def kernel(x, c, positions, w_mod, b_mod, ln_g, ln_b, w_in, q_norm_g, kv_norm_g, w_uq, w_ukv, w_o, w_pool,
           pool_scale, w_group_router, b_group_router, w_expert_router, b_expert_router, w_gate, w_up, w_down):
    B, S, D = x.shape
    N = B * S
    tm = min(512, S)
    tq = min(1024, S)
    tr = min(1024, N)
    chunk = min(2048, N)

    mod = _modulation(c, w_mod, b_mod)
    ln = jnp.stack([ln_g, ln_b], axis=2)
    x2 = x.reshape(N, D)

    win, wuqt, wkn, wvt = _prep_mla_weights(w_in[0], w_uq[0], w_ukv[0])
    qt, k, vt = _mla_proj(x2, positions.reshape(N, 1), mod[0], win, q_norm_g[0].reshape(1, -1),
                          kv_norm_g[0].reshape(1, -1), wuqt, wkn, wvt, jnp.asarray(_rope_inv_freq()),
                          B, S, tm, tq)
    o = _attention(qt, k, vt, tq)
    wr_hi, wr_lo, br = _prep_router(w_group_router[0], b_group_router[0], w_expert_router[0], b_expert_router[0])
    x1, hm, cls = _attn_out(o.reshape(N, D), x2, mod[0], w_o[0].astype(bf16), ln[0, 0], wr_hi, wr_lo, br, S, tm)
    expert_w = (jnp.concatenate([w_gate, w_up], axis=-1).astype(bf16), w_down.astype(bf16))
    pos0, ys0 = _moe_layer(hm, cls, expert_w, 0, chunk, tr, tm)

    wr_hi, wr_lo, br = _prep_router(w_group_router[1], b_group_router[1], w_expert_router[1], b_expert_router[1])
    x3, hm, cls = _pool_layer(pos0, x1, ys0, mod[0], ln[0, 1], mod[1], w_pool[0].astype(bf16),
                              pool_scale[0].reshape(1, D), ln[1, 0], wr_hi, wr_lo, br, S, tm)
    pos1, ys1 = _moe_layer(hm, cls, expert_w, 1, chunk, tr, tm)

    out = _final_merge(pos1, x3, ys1, mod[1], ln[1, 1], S, tm)
    return out.reshape(B, S, D)
```

```python
import functools

import numpy as np
import jax
import jax.numpy as jnp
from jax import lax
from jax.experimental import pallas as pl
from jax.experimental.pallas import tpu as pltpu

MLA_HEADS = 16
Q_LORA = 384
KV_LORA = 256
QK_NOPE = 64
QK_ROPE = 32
V_HEAD = 64
ROPE_THETA = 10000.0
POOL_WINDOWS = (2, 4, 8, 16)
MOE_GROUPS = 4
EXPERTS_PER_GROUP = 8
DEPTH = 2
DN_ALPHA = (2.0 * DEPTH) ** 0.25
LN_EPS = 1e-5
RMS_EPS = 1e-6
LOG2E = 1.4426950408889634

LANES = 128
SUBLANES = 8
VMEM_LIMIT = 56 * 1024 * 1024

HEAD_W = 128
PAIRS_PER_GROUP = EXPERTS_PER_GROUP * (EXPERTS_PER_GROUP - 1) // 2
N_CLASSES = MOE_GROUPS * PAIRS_PER_GROUP
ROW_TILE = 128
HM_ROWS = 16
Y_ROWS = 8
MOE_SLOTS = 3
SMEM_BLOCK = 1024
HALO = 16

f32 = jnp.float32
bf16 = jnp.bfloat16


def _cparams(sem):
    return pltpu.CompilerParams(dimension_semantics=sem, vmem_limit_bytes=VMEM_LIMIT)


def _split_bf16(a):
    hi = a.astype(bf16)
    lo = (a - hi.astype(f32)).astype(bf16)
    return hi, lo


def _dot(a, b):
    return jnp.dot(a, b, preferred_element_type=f32)


def _dot3(a_hi, a_lo, b_hi, b_lo):
    return _dot(a_hi, b_hi) + (_dot(a_lo, b_hi) + _dot(a_hi, b_lo))


def _rows_load(ref, n, pitch, width):
    return jnp.concatenate([ref[pl.ds(s, n, stride=pitch), :] for s in range(width // LANES)], axis=1)


def _rows_store(ref, val, pitch):
    n, width = val.shape
    for s in range(width // LANES):
        ref[pl.ds(s, n, stride=pitch), :] = val[:, s * LANES:(s + 1) * LANES]


def _layer_norm(x, g, b):
    mu = jnp.mean(x, axis=-1, keepdims=True)
    xc = x - mu
    var = jnp.mean(xc * xc, axis=-1, keepdims=True)
    return xc * lax.rsqrt(var + LN_EPS) * g + b


def _mod_kernel(c_ref, w_ref, b_ref, o_ref):
    c = c_ref[...]
    ca = c * (1.0 / (1.0 + jnp.exp(-c)))
    a_hi, a_lo = _split_bf16(ca)
    w_hi, w_lo = _split_bf16(w_ref[...])
    o_ref[...] = _dot3(a_hi, a_lo, w_hi, w_lo) + b_ref[...]


def _modulation(c, w_mod, b_mod):
    B, D = c.shape
    depth = w_mod.shape[0]
    out = pl.pallas_call(
        _mod_kernel,
        grid=(depth, 6),
        in_specs=[
            pl.BlockSpec((B, D), lambda i, j: (0, 0)),
            pl.BlockSpec((None, D, D), lambda i, j: (i, 0, j)),
            pl.BlockSpec((None, None, 1, D), lambda i, j: (i, j, 0, 0)),
        ],
        out_specs=pl.BlockSpec((None, None, B, D), lambda i, j: (i, j, 0, 0)),
        out_shape=jax.ShapeDtypeStruct((depth, 6, B, D), f32),
        compiler_params=_cparams(("arbitrary", "arbitrary")),
        name="adaln_mod",
    )(c, w_mod, b_mod.reshape(depth, 6, 1, D))
    out = jnp.transpose(out, (0, 2, 1, 3))
    return jnp.pad(out, ((0, 0), (0, 0), (0, 2), (0, 0)))


def _mla_proj_kernel(x_ref, pos_ref, mod_ref, win_ref, qg_ref, kg_ref, wuqt_ref, wkn_ref, wvt_ref,
                     invf_ref, qt_ref, k_ref, vt_ref):
    x = x_ref[...]
    h = x * (1.0 + mod_ref[1:2, :]) + mod_ref[0:1, :]
    z = _dot(h.astype(bf16), win_ref[...])
    zq = z[:, :Q_LORA]
    zkv = z[:, Q_LORA:Q_LORA + KV_LORA]
    za = z[:, Q_LORA + KV_LORA:Q_LORA + KV_LORA + LANES]
    zb = z[:, Q_LORA + KV_LORA + LANES:]
    q_lat = zq * lax.rsqrt(jnp.mean(zq * zq, axis=-1, keepdims=True) + RMS_EPS) * qg_ref[...]
    kv_lat = zkv * lax.rsqrt(jnp.mean(zkv * zkv, axis=-1, keepdims=True) + RMS_EPS) * kg_ref[...]

    ang = pos_ref[...].astype(f32) * invf_ref[...]
    cs = jnp.cos(ang)
    sn = jnp.sin(ang)
    lane = lax.broadcasted_iota(jnp.int32, cs.shape, 1)
    scale = (QK_NOPE + QK_ROPE) ** -0.5 * LOG2E
    tqn = scale * jnp.where(lane < QK_NOPE + QK_ROPE, cs, sn)
    kr = za * cs + zb * sn

    kn = _dot(kv_lat.astype(bf16), wkn_ref[...])
    for hd in range(MLA_HEADS):
        k_ref[hd] = (kn[:, hd * HEAD_W:(hd + 1) * HEAD_W] + kr).astype(bf16)

    tqt = tqn.T
    qt = _dot(wuqt_ref[...], q_lat.T.astype(bf16))
    for hd in range(MLA_HEADS):
        qt_ref[hd] = (qt[hd * HEAD_W:(hd + 1) * HEAD_W, :] * tqt).astype(bf16)
    vt_ref[...] = _dot(wvt_ref[...], kv_lat.T.astype(bf16)).astype(bf16)


def _mla_proj(x2, pos2, mod, win, qg, kg, wuqt, wkn, wvt, invf, B, S, tm, tq):
    N, D = x2.shape
    spb = S // tm
    per_q = tq // tm
    const = lambda shape: pl.BlockSpec(shape, lambda i: (0,) * len(shape))
    nq = S // tq
    return pl.pallas_call(
        _mla_proj_kernel,
        grid=(N // tm,),
        in_specs=[
            pl.BlockSpec((tm, D), lambda i: (i, 0)),
            pl.BlockSpec((tm, 1), lambda i: (i, 0)),
            pl.BlockSpec((None, 8, D), lambda i: (i // spb, 0, 0)),
            const(win.shape), const(qg.shape), const(kg.shape), const(wuqt.shape), const(wkn.shape),
            const(wvt.shape), const(invf.shape),
        ],
        out_specs=[
            pl.BlockSpec((None, MLA_HEADS, None, HEAD_W, tm),
                         lambda i: (i // spb, 0, (i % spb) // per_q, 0, i % per_q)),
            pl.BlockSpec((None, MLA_HEADS, tm, HEAD_W), lambda i: (i // spb, 0, i % spb, 0)),
            pl.BlockSpec((None, None, MLA_HEADS * V_HEAD, tm),
                         lambda i: (i // spb, (i % spb) // per_q, 0, i % per_q)),
        ],
        out_shape=[
            jax.ShapeDtypeStruct((B, MLA_HEADS, nq, HEAD_W, tq), bf16),
            jax.ShapeDtypeStruct((B, MLA_HEADS, S, HEAD_W), bf16),
            jax.ShapeDtypeStruct((B, nq, MLA_HEADS * V_HEAD, tq), bf16),
        ],
        compiler_params=_cparams(("arbitrary",)),
        name="mla_proj",
    )(x2, pos2, mod, win, qg, kg, wuqt, wkn, wvt, invf)


ONES_ROWS = 16


def _attn_kernel(qt_ref, k_ref, vt_ref, o_ref, vx_ref, m_ref, acc_ref, ot_ref, *, tq):
    nq = qt_ref.shape[1]
    for hh in range(2):
        for j in range(nq):
            vx_ref[hh, j, :V_HEAD, :] = vt_ref[j, hh * V_HEAD:(hh + 1) * V_HEAD, :]
            vx_ref[hh, j, V_HEAD:, :] = jnp.ones((ONES_ROWS, tq), bf16)

    def kv_step(qi, j, masked):
        k0 = pl.multiple_of(j * tq, tq)
        scores = [_dot(k_ref[hh, pl.ds(k0, tq), :], qt_ref[hh, qi]) for hh in range(2)]
        for hh in range(2):
            s = scores[hh]
            if masked:
                key = lax.broadcasted_iota(jnp.int32, (tq, tq), 0)
                qry = lax.broadcasted_iota(jnp.int32, (tq, tq), 1)
                s = jnp.where(key <= qry, s, -jnp.inf)
            m_old = m_ref[hh]
            m_new = jnp.maximum(m_old, jnp.max(s, axis=0, keepdims=True))
            alpha = jnp.exp2(m_old - m_new)
            p = jnp.exp2(s - m_new[0:1, :])
            pv = _dot(vx_ref[hh, j], p.astype(bf16))
            acc_ref[hh] = acc_ref[hh] * alpha[0:1, :] + pv
            m_ref[hh] = m_new

    def q_body(qi, carry):
        m_ref[...] = jnp.full(m_ref.shape, -jnp.inf, f32)
        acc_ref[...] = jnp.zeros(acc_ref.shape, f32)

        def full_body(j, c):
            kv_step(qi, j, False)
            return c

        lax.fori_loop(0, qi, full_body, 0)
        kv_step(qi, qi, True)
        for hh in range(2):
            acc = acc_ref[hh]
            ot_ref[hh * V_HEAD:(hh + 1) * V_HEAD, :] = acc[:V_HEAD, :] * (1.0 / acc[V_HEAD:V_HEAD + 1, :])
        o_ref[pl.ds(pl.multiple_of(qi * tq, tq), tq), :] = ot_ref[...].T.astype(bf16)
        return carry

    lax.fori_loop(0, nq, q_body, 0)


def _attention(qt, k, vt, tq):
    B, H, nq, W, _ = qt.shape
    S = nq * tq
    return pl.pallas_call(
        functools.partial(_attn_kernel, tq=tq),
        grid=(B, H // 2),
        in_specs=[
            pl.BlockSpec((None, 2, nq, W, tq), lambda b, p: (b, p, 0, 0, 0)),
            pl.BlockSpec((None, 2, S, W), lambda b, p: (b, p, 0, 0)),
            pl.BlockSpec((None, nq, 2 * V_HEAD, tq), lambda b, p: (b, 0, p, 0)),
        ],
        out_specs=pl.BlockSpec((None, S, LANES), lambda b, p: (b, 0, p)),
        out_shape=jax.ShapeDtypeStruct((B, S, H * V_HEAD), bf16),
        scratch_shapes=[
            pltpu.VMEM((2, nq, V_HEAD + ONES_ROWS, tq), bf16),
            pltpu.VMEM((2, SUBLANES, tq), f32),
            pltpu.VMEM((2, V_HEAD + ONES_ROWS, tq), f32),
            pltpu.VMEM((2 * V_HEAD, tq), f32),
        ],
        compiler_params=_cparams(("arbitrary", "arbitrary")),
        name="mla_attention",
    )(qt, k, vt)


def _post_mixer(x, y, mod_ref, ln_ref, wr_hi_ref, wr_lo_ref, br_ref, xo_ref, hm_ref, cls_ref):
    tm, D = x.shape
    x1 = _layer_norm(DN_ALPHA * x + (1.0 + mod_ref[2:3, :]) * y, ln_ref[0:1, :], ln_ref[1:2, :])
    xo_ref[...] = x1
    h2 = x1 * (1.0 + mod_ref[4:5, :]) + mod_ref[3:4, :]
    h_hi, h_lo = _split_bf16(h2)
    logits = _dot3(h_hi, h_lo, wr_hi_ref[...], wr_lo_ref[...]) + br_ref[...]

    lane = lax.broadcasted_iota(jnp.int32, logits.shape, 1).astype(f32)
    neg = -jnp.inf
    far = float(LANES)
    is_g = lane < MOE_GROUPS
    gl = jnp.where(is_g, logits, neg)
    gmax = jnp.max(gl, axis=-1, keepdims=True)
    gidx = jnp.min(jnp.where(gl == gmax, lane, far), axis=-1, keepdims=True)
    g_p = 1.0 / jnp.sum(jnp.where(is_g, jnp.exp(logits - gmax), 0.0), axis=-1, keepdims=True)
    base = MOE_GROUPS + EXPERTS_PER_GROUP * gidx
    el = jnp.where((lane >= base) & (lane < base + EXPERTS_PER_GROUP), logits, neg)
    t1 = jnp.max(el, axis=-1, keepdims=True)
    i1 = jnp.min(jnp.where(el == t1, lane, far), axis=-1, keepdims=True)
    el2 = jnp.where(lane == i1, neg, el)
    t2 = jnp.max(el2, axis=-1, keepdims=True)
    i2 = jnp.min(jnp.where(el2 == t2, lane, far), axis=-1, keepdims=True)
    e = jnp.exp(t2 - t1)
    w1 = g_p / (1.0 + e)
    w2 = g_p * e / (1.0 + e)
    a = i1 - base
    b = i2 - base
    a_first = a < b
    lo = jnp.where(a_first, a, b)
    hi = jnp.where(a_first, b, a)
    w_lo = jnp.where(a_first, w1, w2)
    w_hi = jnp.where(a_first, w2, w1)
    pair = lo * (2 * EXPERTS_PER_GROUP - 1 - lo) * 0.5 + (hi - lo - 1.0)
    cls_ref[...] = (gidx * PAIRS_PER_GROUP + pair).astype(jnp.int32)

    _rows_store(hm_ref, h2, HM_ROWS)
    hm_ref.reshape(tm, HM_ROWS, LANES)[:, D // LANES:, :] = jnp.zeros((tm, HM_ROWS - D // LANES, LANES), f32)
    hm_ref[pl.ds(D // LANES, tm, stride=HM_ROWS), :] = jnp.broadcast_to(w_lo, (tm, LANES))
    hm_ref[pl.ds(D // LANES + 1, tm, stride=HM_ROWS), :] = jnp.broadcast_to(w_hi, (tm, LANES))


def _attn_out_kernel(o_ref, x_ref, mod_ref, wo_ref, ln_ref, wr_hi_ref, wr_lo_ref, br_ref,
                     xo_ref, hm_ref, cls_ref):
    y = _dot(o_ref[...], wo_ref[...])
    _post_mixer(x_ref[...], y, mod_ref, ln_ref, wr_hi_ref, wr_lo_ref, br_ref, xo_ref, hm_ref, cls_ref)


def _post_out_specs(N, D, tm):
    specs = [
        pl.BlockSpec((tm, D), lambda i: (i, 0)),
        pl.BlockSpec((tm * HM_ROWS, LANES), lambda i: (i, 0)),
        pl.BlockSpec((tm, 1), lambda i: (i, 0)),
    ]
    shapes = [
        jax.ShapeDtypeStruct((N, D), f32),
        jax.ShapeDtypeStruct((N * HM_ROWS, LANES), f32),
        jax.ShapeDtypeStruct((N, 1), jnp.int32),
    ]
    return specs, shapes


def _attn_out(o2, x2, mod, wo, ln, wr_hi, wr_lo, br, S, tm):
    N, D = x2.shape
    spb = S // tm
    const = lambda shape: pl.BlockSpec(shape, lambda i: (0,) * len(shape))
    out_specs, out_shapes = _post_out_specs(N, D, tm)
    return pl.pallas_call(
        _attn_out_kernel,
        grid=(N // tm,),
        in_specs=[
            pl.BlockSpec((tm, D), lambda i: (i, 0)),
            pl.BlockSpec((tm, D), lambda i: (i, 0)),
            pl.BlockSpec((None, 8, D), lambda i: (i // spb, 0, 0)),
            const(wo.shape), const(ln.shape), const(wr_hi.shape), const(wr_lo.shape), const(br.shape),
        ],
        out_specs=out_specs,
        out_shape=out_shapes,
        compiler_params=_cparams(("arbitrary",)),
        name="attn_out_router",
    )(o2, x2, mod, wo, ln, wr_hi, wr_lo, br)


def _pool_kernel(x_ref, y_ref, modp_ref, lnp_ref, mod_ref, wp_ref, ps_ref, ln_ref,
                 wr_hi_ref, wr_lo_ref, br_ref, xo_ref, hm_ref, cls_ref, hb_ref, *, spb):
    tm, D = x_ref.shape
    i = pl.program_id(0)
    t_blk = i % spb
    yprev = _rows_load(y_ref, tm, Y_ROWS, D)
    x2 = _layer_norm(DN_ALPHA * x_ref[...] + (1.0 + modp_ref[5:6, :]) * yprev,
                     lnp_ref[0:1, :], lnp_ref[1:2, :])
    h = x2 * (1.0 + mod_ref[1:2, :]) + mod_ref[0:1, :]

    @pl.when(t_blk == 0)
    def _():
        hb_ref[0:HALO, :] = jnp.zeros((HALO, D), f32)

    hb_ref[HALO:, :] = h
    t_seq = t_blk * tm + lax.broadcasted_iota(jnp.int32, (tm, 1), 0)
    gd = D // len(POOL_WINDOWS)
    ys = []
    for gi, w in enumerate(POOL_WINDOWS):
        c0, c1 = gi * gd, (gi + 1) * gd
        a = hb_ref[:, c0:c1]
        span = 1
        while span < w:
            a = a[span:, :] + a[:-span, :]
            span *= 2
        tsum = a[HALO - (w - 1):, :]
        cnt = jnp.minimum(t_seq + 1, w).astype(f32)
        mixed = tsum / cnt - h[:, c0:c1]
        ys.append(_dot(mixed.astype(bf16), wp_ref[gi]))
    hb_ref[0:HALO, :] = h[tm - HALO:, :]
    y = jnp.concatenate(ys, axis=1) * ps_ref[...]
    _post_mixer(x2, y, mod_ref, ln_ref, wr_hi_ref, wr_lo_ref, br_ref, xo_ref, hm_ref, cls_ref)


def _pool_layer(x2, yprev, modp, lnp, mod, wp, ps, ln, wr_hi, wr_lo, br, S, tm):
    N, D = x2.shape
    spb = S // tm
    const = lambda shape: pl.BlockSpec(shape, lambda i: (0,) * len(shape))
    out_specs, out_shapes = _post_out_specs(N, D, tm)
    return pl.pallas_call(
        functools.partial(_pool_kernel, spb=spb),
        grid=(N // tm,),
        in_specs=[
            pl.BlockSpec((tm, D), lambda i: (i, 0)),
            pl.BlockSpec((tm * Y_ROWS, LANES), lambda i: (i, 0)),
            pl.BlockSpec((None, 8, D), lambda i: (i // spb, 0, 0)),
            const(lnp.shape),
            pl.BlockSpec((None, 8, D), lambda i: (i // spb, 0, 0)),
            const(wp.shape), const(ps.shape), const(ln.shape),
            const(wr_hi.shape), const(wr_lo.shape), const(br.shape),
        ],
        out_specs=out_specs,
        out_shape=out_shapes,
        scratch_shapes=[pltpu.VMEM((HALO + tm, D), f32)],
        compiler_params=_cparams(("arbitrary",)),
        name="pool_mixer_router",
    )(x2, yprev, modp, lnp, mod, wp, ps, ln, wr_hi, wr_lo, br)


def _final_kernel(x_ref, y_ref, mod_ref, ln_ref, o_ref):
    tm, D = x_ref.shape
    y = _rows_load(y_ref, tm, Y_ROWS, D)
    o_ref[...] = _layer_norm(DN_ALPHA * x_ref[...] + (1.0 + mod_ref[5:6, :]) * y,
                             ln_ref[0:1, :], ln_ref[1:2, :])


def _final_merge(x2, y, mod, ln, S, tm):
    N, D = x2.shape
    spb = S // tm
    return pl.pallas_call(
        _final_kernel,
        grid=(N // tm,),
        in_specs=[
            pl.BlockSpec((tm, D), lambda i: (i, 0)),
            pl.BlockSpec((tm * Y_ROWS, LANES), lambda i: (i, 0)),
            pl.BlockSpec((None, 8, D), lambda i: (i // spb, 0, 0)),
            pl.BlockSpec(ln.shape, lambda i: (0, 0)),
        ],
        out_specs=pl.BlockSpec((tm, D), lambda i: (i, 0)),
        out_shape=jax.ShapeDtypeStruct((N, D), f32),
        compiler_params=_cparams(("arbitrary",)),
        name="final_merge",
    )(x2, y, mod, ln)


def _rank_kernel(cls_ref, rank_ref, cnt_ref, carry_ref, earlier_ref):
    tr = cls_ref.shape[0]

    @pl.when(pl.program_id(0) == 0)
    def _():
        carry_ref[...] = jnp.zeros(carry_ref.shape, f32)
        r = lax.broadcasted_iota(jnp.int32, (tr, tr), 0)
        c = lax.broadcasted_iota(jnp.int32, (tr, tr), 1)
        earlier_ref[...] = (c < r).astype(bf16)

    lane = lax.broadcasted_iota(jnp.int32, (tr, LANES), 1)
    onehot = (cls_ref[...] == lane)
    oh = onehot.astype(bf16)
    before = _dot(earlier_ref[...], oh) + carry_ref[0:1, :]
    rank_ref[...] = jnp.sum(jnp.where(onehot, before, 0.0), axis=-1, keepdims=True).astype(jnp.int32)
    total = carry_ref[0:1, :] + jnp.sum(oh.astype(f32), axis=0, keepdims=True)
    carry_ref[...] = jnp.broadcast_to(total, carry_ref.shape)
    cnt_ref[...] = jnp.broadcast_to(total, cnt_ref.shape).astype(jnp.int32)


def _class_ranks(cls, tr):
    N = cls.shape[0]
    return pl.pallas_call(
        _rank_kernel,
        grid=(N // tr,),
        in_specs=[pl.BlockSpec((tr, 1), lambda i: (i, 0))],
        out_specs=[
            pl.BlockSpec((tr, 1), lambda i: (i, 0)),
            pl.BlockSpec((SUBLANES, LANES), lambda i: (0, 0)),
        ],
        out_shape=[
            jax.ShapeDtypeStruct((N, 1), jnp.int32),
            jax.ShapeDtypeStruct((SUBLANES, LANES), jnp.int32),
        ],
        scratch_shapes=[pltpu.VMEM((SUBLANES, LANES), f32), pltpu.VMEM((tr, tr), bf16)],
        compiler_params=_cparams(("arbitrary",)),
        name="class_ranks",
    )(cls)


def _slot_token_kernel(off_ref, cls_ref, rank_ref, inv_ref, *, chunk):
    base = pl.program_id(0) * chunk

    @pl.when(pl.program_id(0) == 0)
    def _():
        def clear(s, c):
            inv_ref[s] = 0
            return c
        lax.fori_loop(0, inv_ref.shape[0], clear, 0, unroll=32)

    def body(t, c):
        inv_ref[off_ref[cls_ref[t]] + rank_ref[t]] = base + t
        return c

    lax.fori_loop(0, chunk, body, 0, unroll=16)


def _slot_tokens(off, cls1, rank1, n_slots, chunk):
    N = cls1.shape[0]
    return pl.pallas_call(
        functools.partial(_slot_token_kernel, chunk=chunk),
        grid_spec=pltpu.PrefetchScalarGridSpec(
            num_scalar_prefetch=1,
            grid=(N // chunk,),
            in_specs=[
                pl.BlockSpec((chunk,), lambda i, off: (i,), memory_space=pltpu.SMEM),
                pl.BlockSpec((chunk,), lambda i, off: (i,), memory_space=pltpu.SMEM),
            ],
            out_specs=pl.BlockSpec((n_slots,), lambda i, off: (0,), memory_space=pltpu.SMEM),
        ),
        out_shape=jax.ShapeDtypeStruct((n_slots,), jnp.int32),
        compiler_params=_cparams(("arbitrary",)),
        name="slot_tokens",
    )(off, cls1, rank1)


def _moe_kernel(te1_ref, te2_ref, cnt_ref, inv_ref, hm_ref, wgu1_ref, wd1_ref, wgu2_ref, wd2_ref,
                y_ref, *scratch):
    hbufs = scratch[:MOE_SLOTS]
    ybufs = scratch[MOE_SLOTS:2 * MOE_SLOTS]
    gsem, ssem = scratch[2 * MOE_SLOTS:]
    ff, D = wd1_ref.shape
    i = pl.program_id(0)

    def count(tile):
        return jnp.where(tile >= 0, cnt_ref[jnp.maximum(tile, 0)], 0)

    n_cur, n_prev, n_ahead, n_old = count(i), count(i - 1), count(i + 2), count(i - MOE_SLOTS)

    def token_rows(ref, t, pitch):
        return ref.at[pl.ds(pl.multiple_of(t * pitch, pitch), pitch)]

    def start_gather(tok, r, hb, sem):
        pltpu.make_async_copy(token_rows(hm_ref, tok, HM_ROWS), token_rows(hb, r, HM_ROWS), sem).start()

    def start_scatter(tok, r, yb, sem):
        pltpu.make_async_copy(token_rows(yb, r, Y_ROWS), token_rows(y_ref, tok, Y_ROWS), sem).start()

    def wait_tokens(src, dst, sem, n, pitch):
        @pl.when(n > 0)
        def _():
            rows = pl.multiple_of(n * pitch, pitch)
            pltpu.make_async_copy(src.at[pl.ds(0, rows)], dst.at[pl.ds(0, rows)], sem).wait()

    @pl.when(i == 0)
    def _():
        for t in range(MOE_SLOTS - 1):
            def first(r, c):
                start_gather(inv_ref[t * ROW_TILE + r], r, hbufs[t], gsem.at[t])
                return c
            lax.fori_loop(0, cnt_ref[t], first, 0)

    def step(cur):
        far = (cur + MOE_SLOTS - 1) % MOE_SLOTS
        hb, yb = hbufs[cur], ybufs[cur]
        wait_tokens(hm_ref, hb, gsem.at[cur], n_cur, HM_ROWS)
        wait_tokens(yb, y_ref, ssem.at[cur], n_old, Y_ROWS)

        @pl.when((n_cur > 0) | (n_prev > 0))
        def _():
            ahead = (i + 2) * ROW_TILE
            prev = jnp.maximum(i - 1, 0) * ROW_TILE
            for r in range(ROW_TILE):
                tok_ahead = inv_ref[ahead + r]
                tok_prev = inv_ref[prev + r]

                @pl.when(r < n_ahead)
                def _():
                    start_gather(tok_ahead, r, hbufs[far], gsem.at[far])

                @pl.when(r < n_prev)
                def _():
                    start_scatter(tok_prev, r, ybufs[far], ssem.at[far])

            h = _rows_load(hb, ROW_TILE, HM_ROWS, D).astype(bf16)

            def hidden(gu, gate_row):
                w = hb[pl.ds(gate_row, ROW_TILE, stride=HM_ROWS), :]
                g, u = gu[:, :ff], gu[:, ff:]
                he = g * (1.0 / (1.0 + jnp.exp(-g))) * u
                return (he * jnp.concatenate([w] * (ff // LANES), axis=1)).astype(bf16)

            gu1 = _dot(h, wgu1_ref[...])
            gu2 = _dot(h, wgu2_ref[...])
            y = _dot(hidden(gu1, D // LANES), wd1_ref[...]) + _dot(hidden(gu2, D // LANES + 1), wd2_ref[...])
            _rows_store(yb, y, Y_ROWS)

    for cur in range(MOE_SLOTS):
        @pl.when(i % MOE_SLOTS == cur)
        def _():
            step(cur)


def _moe(te1, te2, cnt, inv, hm, wgu, wd, layer, n_steps):
    N = hm.shape[0] // HM_ROWS
    _, E, D, ff2 = wgu.shape
    ff = ff2 // 2

    def wmap(which):
        def f(i, te1, te2, cnt, inv):
            return (layer, (te1, te2)[which][i], 0, 0)
        return f

    return pl.pallas_call(
        _moe_kernel,
        grid_spec=pltpu.PrefetchScalarGridSpec(
            num_scalar_prefetch=4,
            grid=(n_steps,),
            in_specs=[
                pl.BlockSpec(memory_space=pl.ANY),
                pl.BlockSpec((None, None, D, ff2), wmap(0)),
                pl.BlockSpec((None, None, ff, D), wmap(0)),
                pl.BlockSpec((None, None, D, ff2), wmap(1)),
                pl.BlockSpec((None, None, ff, D), wmap(1)),
            ],
            out_specs=pl.BlockSpec(memory_space=pl.ANY),
            scratch_shapes=(
                [pltpu.VMEM((ROW_TILE * HM_ROWS, LANES), f32)] * MOE_SLOTS
                + [pltpu.VMEM((ROW_TILE * Y_ROWS, LANES), f32)] * MOE_SLOTS
                + [pltpu.SemaphoreType.DMA((MOE_SLOTS,)), pltpu.SemaphoreType.DMA((MOE_SLOTS,))]
            ),
        ),
        out_shape=jax.ShapeDtypeStruct((N * Y_ROWS, LANES), f32),
        compiler_params=_cparams(("arbitrary",)),
        name="pair_moe",
    )(te1, te2, cnt, inv, hm, wgu, wd, wgu, wd)


def _pair_tables():
    lo = np.zeros((N_CLASSES,), np.int32)
    hi = np.zeros((N_CLASSES,), np.int32)
    for g in range(MOE_GROUPS):
        p = 0
        for a in range(EXPERTS_PER_GROUP):
            for b in range(a + 1, EXPERTS_PER_GROUP):
                lo[g * PAIRS_PER_GROUP + p] = g * EXPERTS_PER_GROUP + a
                hi[g * PAIRS_PER_GROUP + p] = g * EXPERTS_PER_GROUP + b
                p += 1
    return lo, hi


def _moe_layer(hm, cls, expert_w, layer, chunk, tr):
    N = cls.shape[0]
    n_tiles = N // ROW_TILE + N_CLASSES
    n_steps = n_tiles + MOE_SLOTS
    rank, cnt = _class_ranks(cls, tr)
    counts = cnt[0, :N_CLASSES]
    tiles = (counts + ROW_TILE - 1) // ROW_TILE
    tile_end = jnp.cumsum(tiles)
    tile_start = tile_end - tiles
    off = jnp.zeros((LANES,), jnp.int32).at[:N_CLASSES].set(tile_start * ROW_TILE)
    nused = tile_end[-1]
    step = jnp.arange(n_steps + 2, dtype=jnp.int32)
    tile_cls = jnp.sum(tile_end[None, :] <= jnp.minimum(step, nused - 1)[:, None], axis=1)
    tile_cls = jnp.minimum(tile_cls, N_CLASSES - 1).astype(jnp.int32)
    lo_tab, hi_tab = _pair_tables()
    te1 = jnp.asarray(lo_tab)[tile_cls]
    te2 = jnp.asarray(hi_tab)[tile_cls]
    in_cls = (step - tile_start[tile_cls]) * ROW_TILE
    tile_cnt = jnp.where(step < nused, jnp.clip(counts[tile_cls] - in_cls, 0, ROW_TILE), 0).astype(jnp.int32)

    n_slots = -(-(n_tiles + MOE_SLOTS) * ROW_TILE // SMEM_BLOCK) * SMEM_BLOCK
    inv = _slot_tokens(off, cls.reshape(N), rank.reshape(N), n_slots, chunk)
    return _moe(te1, te2, tile_cnt, inv, hm, *expert_w, layer, n_steps)


def _swap_rope(w):
    half = QK_ROPE // 2
    return jnp.concatenate([-w[..., half:], w[..., :half]], axis=-1)


def _prep_mla_weights(w_in, w_uq, w_ukv):
    D = w_in.shape[0]
    w_kr = w_in[:, Q_LORA + KV_LORA:]
    w_ks = _swap_rope(w_kr)
    zeros = jnp.zeros((D, QK_NOPE), w_in.dtype)
    win = jnp.concatenate([w_in[:, :Q_LORA + KV_LORA], zeros, w_kr, w_kr, zeros, w_ks, w_ks], axis=1)
    uq = w_uq.reshape(Q_LORA, MLA_HEADS, QK_NOPE + QK_ROPE)
    rope = uq[..., QK_NOPE:]
    wuq = jnp.concatenate([uq[..., :QK_NOPE], rope, _swap_rope(rope)], axis=-1).reshape(Q_LORA, -1)
    ukv = w_ukv.reshape(KV_LORA, MLA_HEADS, QK_NOPE + V_HEAD)
    wkn = jnp.concatenate([ukv[..., :QK_NOPE], jnp.zeros((KV_LORA, MLA_HEADS, HEAD_W - QK_NOPE), w_ukv.dtype)],
                          axis=-1).reshape(KV_LORA, -1)
    wv = ukv[..., QK_NOPE:].reshape(KV_LORA, -1)
    return win.astype(bf16), wuq.T.astype(bf16), wkn.astype(bf16), wv.T.astype(bf16)


def _rope_inv_freq():
    inv = 1.0 / (ROPE_THETA ** (np.arange(0, QK_ROPE, 2, dtype=np.float32) / QK_ROPE))
    inv2 = np.concatenate([inv, inv]).astype(np.float32)
    return np.concatenate([np.zeros((QK_NOPE,), np.float32), inv2, inv2]).reshape(1, LANES)


def _prep_router(w_gr, b_gr, w_er, b_er):
    D = w_gr.shape[0]
    n = MOE_GROUPS + MOE_GROUPS * EXPERTS_PER_GROUP
    w = jnp.concatenate([w_gr, w_er, jnp.zeros((D, LANES - n), f32)], axis=1)
    b = jnp.concatenate([b_gr, b_er, jnp.zeros((LANES - n,), f32)]).reshape(1, LANES)
    hi, lo = _split_bf16(w)
    return hi, lo, b


def kernel(x, c, positions, w_mod, b_mod, ln_g, ln_b, w_in, q_norm_g, kv_norm_g, w_uq, w_ukv, w_o, w_pool,
           pool_scale, w_group_router, b_group_router, w_expert_router, b_expert_router, w_gate, w_up, w_down):
    B, S, D = x.shape
    N = B * S
    tm = min(512, S)
    tq = min(1024, S)
    tr = min(1024, N)
    chunk = min(2048, N)

    mod = _modulation(c, w_mod, b_mod)
    ln = jnp.stack([ln_g, ln_b], axis=2)
    x2 = x.reshape(N, D)

    win, wuqt, wkn, wvt = _prep_mla_weights(w_in[0], w_uq[0], w_ukv[0])
    qt, k, vt = _mla_proj(x2, positions.reshape(N, 1), mod[0], win, q_norm_g[0].reshape(1, -1),
                          kv_norm_g[0].reshape(1, -1), wuqt, wkn, wvt, jnp.asarray(_rope_inv_freq()),
                          B, S, tm, tq)
    o = _attention(qt, k, vt, tq)
    wr_hi, wr_lo, br = _prep_router(w_group_router[0], b_group_router[0], w_expert_router[0], b_expert_router[0])
    x1, hm, cls = _attn_out(o.reshape(N, D), x2, mod[0], w_o[0].astype(bf16), ln[0, 0], wr_hi, wr_lo, br, S, tm)
    expert_w = (jnp.concatenate([w_gate, w_up], axis=-1).astype(bf16), w_down.astype(bf16))
    y0 = _moe_layer(hm, cls, expert_w, 0, chunk, tr)

    wr_hi, wr_lo, br = _prep_router(w_group_router[1], b_group_router[1], w_expert_router[1], b_expert_router[1])
    x3, hm, cls = _pool_layer(x1, y0, mod[0], ln[0, 1], mod[1], w_pool[0].astype(bf16),
                              pool_scale[0].reshape(1, D), ln[1, 0], wr_hi, wr_lo, br, S, tm)
    y1 = _moe_layer(hm, cls, expert_w, 1, chunk, tr)

    out = _final_merge(x3, y1, mod[1], ln[1, 1], S, tm)
    return out.reshape(B, S, D)
```

```python
import functools

import numpy as np
import jax
import jax.numpy as jnp
from jax import lax
from jax.experimental import pallas as pl
from jax.experimental.pallas import tpu as pltpu

MLA_HEADS = 16
Q_LORA = 384
KV_LORA = 256
QK_NOPE = 64
QK_ROPE = 32
V_HEAD = 64
ROPE_THETA = 10000.0
POOL_WINDOWS = (2, 4, 8, 16)
MOE_GROUPS = 4
EXPERTS_PER_GROUP = 8
DEPTH = 2
DN_ALPHA = (2.0 * DEPTH) ** 0.25
LN_EPS = 1e-5
RMS_EPS = 1e-6
LOG2E = 1.4426950408889634

LANES = 128
SUBLANES = 8
VMEM_LIMIT = 56 * 1024 * 1024

HEAD_W = 128
PAIRS_PER_GROUP = EXPERTS_PER_GROUP * (EXPERTS_PER_GROUP - 1) // 2
N_CLASSES = MOE_GROUPS * PAIRS_PER_GROUP
ROW_TILE = 128
HM_ROWS = 16
Y_ROWS = 8
MOE_SLOTS = 3
SMEM_BLOCK = 1024
HALO = 16

f32 = jnp.float32
bf16 = jnp.bfloat16


def _cparams(sem):
    return pltpu.CompilerParams(dimension_semantics=sem, vmem_limit_bytes=VMEM_LIMIT)


def _split_bf16(a):
    hi = a.astype(bf16)
    lo = (a - hi.astype(f32)).astype(bf16)
    return hi, lo


def _dot(a, b):
    return jnp.dot(a, b, preferred_element_type=f32)


def _dot3(a_hi, a_lo, b_hi, b_lo):
    return _dot(a_hi, b_hi) + (_dot(a_lo, b_hi) + _dot(a_hi, b_lo))


def _rows_load(ref, n, pitch, width):
    return jnp.concatenate([ref[pl.ds(s, n, stride=pitch), :] for s in range(width // LANES)], axis=1)


def _rows_store(ref, val, pitch):
    n, width = val.shape
    for s in range(width // LANES):
        ref[pl.ds(s, n, stride=pitch), :] = val[:, s * LANES:(s + 1) * LANES]


def _layer_norm(x, g, b):
    mu = jnp.mean(x, axis=-1, keepdims=True)
    xc = x - mu
    var = jnp.mean(xc * xc, axis=-1, keepdims=True)
    return xc * lax.rsqrt(var + LN_EPS) * g + b


def _mod_kernel(c_ref, w_ref, b_ref, o_ref):
    c = c_ref[...]
    ca = c * (1.0 / (1.0 + jnp.exp(-c)))
    a_hi, a_lo = _split_bf16(ca)
    w_hi, w_lo = _split_bf16(w_ref[...])
    o_ref[...] = _dot3(a_hi, a_lo, w_hi, w_lo) + b_ref[...]


def _modulation(c, w_mod, b_mod):
    B, D = c.shape
    depth = w_mod.shape[0]
    out = pl.pallas_call(
        _mod_kernel,
        grid=(depth, 6),
        in_specs=[
            pl.BlockSpec((B, D), lambda i, j: (0, 0)),
            pl.BlockSpec((None, D, D), lambda i, j: (i, 0, j)),
            pl.BlockSpec((None, None, 1, D), lambda i, j: (i, j, 0, 0)),
        ],
        out_specs=pl.BlockSpec((None, None, B, D), lambda i, j: (i, j, 0, 0)),
        out_shape=jax.ShapeDtypeStruct((depth, 6, B, D), f32),
        compiler_params=_cparams(("arbitrary", "arbitrary")),
        name="adaln_mod",
    )(c, w_mod, b_mod.reshape(depth, 6, 1, D))
    out = jnp.transpose(out, (0, 2, 1, 3))
    return jnp.pad(out, ((0, 0), (0, 0), (0, 2), (0, 0)))


def _mla_proj_kernel(x_ref, pos_ref, mod_ref, win_ref, qg_ref, kg_ref, wuqt_ref, wkn_ref, wvt_ref,
                     invf_ref, qt_ref, k_ref, vt_ref):
    x = x_ref[...]
    h = x * (1.0 + mod_ref[1:2, :]) + mod_ref[0:1, :]
    z = _dot(h.astype(bf16), win_ref[...])
    zq = z[:, :Q_LORA]
    zkv = z[:, Q_LORA:Q_LORA + KV_LORA]
    za = z[:, Q_LORA + KV_LORA:Q_LORA + KV_LORA + LANES]
    zb = z[:, Q_LORA + KV_LORA + LANES:]
    q_lat = zq * lax.rsqrt(jnp.mean(zq * zq, axis=-1, keepdims=True) + RMS_EPS) * qg_ref[...]
    kv_lat = zkv * lax.rsqrt(jnp.mean(zkv * zkv, axis=-1, keepdims=True) + RMS_EPS) * kg_ref[...]

    ang = pos_ref[...].astype(f32) * invf_ref[...]
    cs = jnp.cos(ang)
    sn = jnp.sin(ang)
    lane = lax.broadcasted_iota(jnp.int32, cs.shape, 1)
    scale = (QK_NOPE + QK_ROPE) ** -0.5 * LOG2E
    tqn = scale * jnp.where(lane < QK_NOPE + QK_ROPE, cs, sn)
    kr = za * cs + zb * sn

    kn = _dot(kv_lat.astype(bf16), wkn_ref[...])
    for hd in range(MLA_HEADS):
        k_ref[hd] = (kn[:, hd * HEAD_W:(hd + 1) * HEAD_W] + kr).astype(bf16)

    tqt = tqn.T
    qt = _dot(wuqt_ref[...], q_lat.T.astype(bf16))
    for hd in range(MLA_HEADS):
        qt_ref[hd] = (qt[hd * HEAD_W:(hd + 1) * HEAD_W, :] * tqt).astype(bf16)
    vt_ref[...] = _dot(wvt_ref[...], kv_lat.T.astype(bf16)).astype(bf16)


def _mla_proj(x2, pos2, mod, win, qg, kg, wuqt, wkn, wvt, invf, B, S, tm, tq):
    N, D = x2.shape
    spb = S // tm
    per_q = tq // tm
    const = lambda shape: pl.BlockSpec(shape, lambda i: (0,) * len(shape))
    nq = S // tq
    return pl.pallas_call(
        _mla_proj_kernel,
        grid=(N // tm,),
        in_specs=[
            pl.BlockSpec((tm, D), lambda i: (i, 0)),
            pl.BlockSpec((tm, 1), lambda i: (i, 0)),
            pl.BlockSpec((None, 8, D), lambda i: (i // spb, 0, 0)),
            const(win.shape), const(qg.shape), const(kg.shape), const(wuqt.shape), const(wkn.shape),
            const(wvt.shape), const(invf.shape),
        ],
        out_specs=[
            pl.BlockSpec((None, MLA_HEADS, None, HEAD_W, tm),
                         lambda i: (i // spb, 0, (i % spb) // per_q, 0, i % per_q)),
            pl.BlockSpec((None, MLA_HEADS, tm, HEAD_W), lambda i: (i // spb, 0, i % spb, 0)),
            pl.BlockSpec((None, None, MLA_HEADS * V_HEAD, tm),
                         lambda i: (i // spb, (i % spb) // per_q, 0, i % per_q)),
        ],
        out_shape=[
            jax.ShapeDtypeStruct((B, MLA_HEADS, nq, HEAD_W, tq), bf16),
            jax.ShapeDtypeStruct((B, MLA_HEADS, S, HEAD_W), bf16),
            jax.ShapeDtypeStruct((B, nq, MLA_HEADS * V_HEAD, tq), bf16),
        ],
        compiler_params=_cparams(("arbitrary",)),
        name="mla_proj",
    )(x2, pos2, mod, win, qg, kg, wuqt, wkn, wvt, invf)


ONES_ROWS = 16


def _attn_kernel(qt_ref, k_ref, vt_ref, o_ref, vx_ref, m_ref, acc_ref, ot_ref, *, tq):
    nq = qt_ref.shape[1]
    for hh in range(2):
        for j in range(nq):
            vx_ref[hh, j, :V_HEAD, :] = vt_ref[j, hh * V_HEAD:(hh + 1) * V_HEAD, :]
            vx_ref[hh, j, V_HEAD:, :] = jnp.ones((ONES_ROWS, tq), bf16)

    def kv_step(qi, j, masked):
        k0 = pl.multiple_of(j * tq, tq)
        scores = [_dot(k_ref[hh, pl.ds(k0, tq), :], qt_ref[hh, qi]) for hh in range(2)]
        for hh in range(2):
            s = scores[hh]
            if masked:
                key = lax.broadcasted_iota(jnp.int32, (tq, tq), 0)
                qry = lax.broadcasted_iota(jnp.int32, (tq, tq), 1)
                s = jnp.where(key <= qry, s, -jnp.inf)
            m_old = m_ref[hh]
            m_new = jnp.maximum(m_old, jnp.max(s, axis=0, keepdims=True))
            alpha = jnp.exp2(m_old - m_new)
            p = jnp.exp2(s - m_new[0:1, :])
            pv = _dot(vx_ref[hh, j], p.astype(bf16))
            acc_ref[hh] = acc_ref[hh] * alpha[0:1, :] + pv
            m_ref[hh] = m_new

    def q_body(qi, carry):
        m_ref[...] = jnp.full(m_ref.shape, -jnp.inf, f32)
        acc_ref[...] = jnp.zeros(acc_ref.shape, f32)

        def full_body(j, c):
            kv_step(qi, j, False)
            return c

        lax.fori_loop(0, qi, full_body, 0)
        kv_step(qi, qi, True)
        for hh in range(2):
            acc = acc_ref[hh]
            ot_ref[hh * V_HEAD:(hh + 1) * V_HEAD, :] = acc[:V_HEAD, :] * (1.0 / acc[V_HEAD:V_HEAD + 1, :])
        o_ref[pl.ds(pl.multiple_of(qi * tq, tq), tq), :] = ot_ref[...].T.astype(bf16)
        return carry

    lax.fori_loop(0, nq, q_body, 0)


def _attention(qt, k, vt, tq):
    B, H, nq, W, _ = qt.shape
    S = nq * tq
    return pl.pallas_call(
        functools.partial(_attn_kernel, tq=tq),
        grid=(B, H // 2),
        in_specs=[
            pl.BlockSpec((None, 2, nq, W, tq), lambda b, p: (b, p, 0, 0, 0)),
            pl.BlockSpec((None, 2, S, W), lambda b, p: (b, p, 0, 0)),
            pl.BlockSpec((None, nq, 2 * V_HEAD, tq), lambda b, p: (b, 0, p, 0)),
        ],
        out_specs=pl.BlockSpec((None, S, LANES), lambda b, p: (b, 0, p)),
        out_shape=jax.ShapeDtypeStruct((B, S, H * V_HEAD), bf16),
        scratch_shapes=[
            pltpu.VMEM((2, nq, V_HEAD + ONES_ROWS, tq), bf16),
            pltpu.VMEM((2, SUBLANES, tq), f32),
            pltpu.VMEM((2, V_HEAD + ONES_ROWS, tq), f32),
            pltpu.VMEM((2 * V_HEAD, tq), f32),
        ],
        compiler_params=_cparams(("arbitrary", "arbitrary")),
        name="mla_attention",
    )(qt, k, vt)


def _post_mixer(x, y, mod_ref, ln_ref, wr_ref, br_ref, xo_ref, hm_ref, cls_ref):
    tm, D = x.shape
    x1 = _layer_norm(DN_ALPHA * x + (1.0 + mod_ref[2:3, :]) * y, ln_ref[0:1, :], ln_ref[1:2, :])
    xo_ref[...] = x1
    h2 = x1 * (1.0 + mod_ref[4:5, :]) + mod_ref[3:4, :]
    h_hi, h_lo = _split_bf16(h2)
    wr = wr_ref[...]
    t_hi = _dot(h_hi, wr)
    t_lo = _dot(h_lo, wr)
    logits = t_hi[:, :LANES] + (t_hi[:, LANES:] + t_lo[:, :LANES]) + br_ref[...]

    lane = lax.broadcasted_iota(jnp.int32, logits.shape, 1).astype(f32)
    neg = -jnp.inf
    far = float(LANES)
    is_g = lane < MOE_GROUPS
    gl = jnp.where(is_g, logits, neg)
    gmax = jnp.max(gl, axis=-1, keepdims=True)
    gidx = jnp.min(jnp.where(gl == gmax, lane, far), axis=-1, keepdims=True)
    g_p = 1.0 / jnp.sum(jnp.where(is_g, jnp.exp(logits - gmax), 0.0), axis=-1, keepdims=True)
    base = MOE_GROUPS + EXPERTS_PER_GROUP * gidx
    el = jnp.where((lane >= base) & (lane < base + EXPERTS_PER_GROUP), logits, neg)
    t1 = jnp.max(el, axis=-1, keepdims=True)
    i1 = jnp.min(jnp.where(el == t1, lane, far), axis=-1, keepdims=True)
    el2 = jnp.where(lane == i1, neg, el)
    t2 = jnp.max(el2, axis=-1, keepdims=True)
    i2 = jnp.min(jnp.where(el2 == t2, lane, far), axis=-1, keepdims=True)
    e = jnp.exp(t2 - t1)
    w1 = g_p / (1.0 + e)
    w2 = g_p * e / (1.0 + e)
    a = i1 - base
    b = i2 - base
    a_first = a < b
    lo = jnp.where(a_first, a, b)
    hi = jnp.where(a_first, b, a)
    w_lo = jnp.where(a_first, w1, w2)
    w_hi = jnp.where(a_first, w2, w1)
    pair = lo * (2 * EXPERTS_PER_GROUP - 1 - lo) * 0.5 + (hi - lo - 1.0)
    cls_ref[...] = (gidx * PAIRS_PER_GROUP + pair).astype(jnp.int32)

    _rows_store(hm_ref, h2, HM_ROWS)
    hm_ref.reshape(tm, HM_ROWS, LANES)[:, D // LANES:, :] = jnp.zeros((tm, HM_ROWS - D // LANES, LANES), f32)
    hm_ref[pl.ds(D // LANES, tm, stride=HM_ROWS), :] = jnp.broadcast_to(w_lo, (tm, LANES))
    hm_ref[pl.ds(D // LANES + 1, tm, stride=HM_ROWS), :] = jnp.broadcast_to(w_hi, (tm, LANES))


def _attn_out_kernel(o_ref, x_ref, mod_ref, wo_ref, ln_ref, wr_ref, br_ref,
                     xo_ref, hm_ref, cls_ref):
    y = _dot(o_ref[...], wo_ref[...])
    _post_mixer(x_ref[...], y, mod_ref, ln_ref, wr_ref, br_ref, xo_ref, hm_ref, cls_ref)


def _post_out_specs(N, D, tm):
    specs = [
        pl.BlockSpec((tm, D), lambda i: (i, 0)),
        pl.BlockSpec((tm * HM_ROWS, LANES), lambda i: (i, 0)),
        pl.BlockSpec((tm, 1), lambda i: (i, 0)),
    ]
    shapes = [
        jax.ShapeDtypeStruct((N, D), f32),
        jax.ShapeDtypeStruct((N * HM_ROWS, LANES), f32),
        jax.ShapeDtypeStruct((N, 1), jnp.int32),
    ]
    return specs, shapes


def _attn_out(o2, x2, mod, wo, ln, wr, br, S, tm):
    N, D = x2.shape
    spb = S // tm
    const = lambda shape: pl.BlockSpec(shape, lambda i: (0,) * len(shape))
    out_specs, out_shapes = _post_out_specs(N, D, tm)
    return pl.pallas_call(
        _attn_out_kernel,
        grid=(N // tm,),
        in_specs=[
            pl.BlockSpec((tm, D), lambda i: (i, 0)),
            pl.BlockSpec((tm, D), lambda i: (i, 0)),
            pl.BlockSpec((None, 8, D), lambda i: (i // spb, 0, 0)),
            const(wo.shape), const(ln.shape), const(wr.shape), const(br.shape),
        ],
        out_specs=out_specs,
        out_shape=out_shapes,
        compiler_params=_cparams(("arbitrary",)),
        name="attn_out_router",
    )(o2, x2, mod, wo, ln, wr, br)


def _pool_kernel(x_ref, y_ref, modp_ref, lnp_ref, mod_ref, wp_ref, ps_ref, ln_ref,
                 wr_ref, br_ref, xo_ref, hm_ref, cls_ref, hb_ref, *, spb):
    tm, D = x_ref.shape
    i = pl.program_id(0)
    t_blk = i % spb
    yprev = _rows_load(y_ref, tm, Y_ROWS, D)
    x2 = _layer_norm(DN_ALPHA * x_ref[...] + (1.0 + modp_ref[5:6, :]) * yprev,
                     lnp_ref[0:1, :], lnp_ref[1:2, :])
    h = x2 * (1.0 + mod_ref[1:2, :]) + mod_ref[0:1, :]

    @pl.when(t_blk == 0)
    def _():
        hb_ref[0:HALO, :] = jnp.zeros((HALO, D), f32)

    hb_ref[HALO:, :] = h
    t_seq = t_blk * tm + lax.broadcasted_iota(jnp.int32, (tm, 1), 0)
    gd = D // len(POOL_WINDOWS)
    ys = []
    for gi, w in enumerate(POOL_WINDOWS):
        c0, c1 = gi * gd, (gi + 1) * gd
        a = hb_ref[:, c0:c1]
        span = 1
        while span < w:
            a = a[span:, :] + a[:-span, :]
            span *= 2
        tsum = a[HALO - (w - 1):, :]
        cnt = jnp.minimum(t_seq + 1, w).astype(f32)
        mixed = tsum / cnt - h[:, c0:c1]
        ys.append(_dot(mixed.astype(bf16), wp_ref[gi]))
    hb_ref[0:HALO, :] = h[tm - HALO:, :]
    y = jnp.concatenate(ys, axis=1) * ps_ref[...]
    _post_mixer(x2, y, mod_ref, ln_ref, wr_ref, br_ref, xo_ref, hm_ref, cls_ref)


def _pool_layer(x2, yprev, modp, lnp, mod, wp, ps, ln, wr, br, S, tm):
    N, D = x2.shape
    spb = S // tm
    const = lambda shape: pl.BlockSpec(shape, lambda i: (0,) * len(shape))
    out_specs, out_shapes = _post_out_specs(N, D, tm)
    return pl.pallas_call(
        functools.partial(_pool_kernel, spb=spb),
        grid=(N // tm,),
        in_specs=[
            pl.BlockSpec((tm, D), lambda i: (i, 0)),
            pl.BlockSpec((tm * Y_ROWS, LANES), lambda i: (i, 0)),
            pl.BlockSpec((None, 8, D), lambda i: (i // spb, 0, 0)),
            const(lnp.shape),
            pl.BlockSpec((None, 8, D), lambda i: (i // spb, 0, 0)),
            const(wp.shape), const(ps.shape), const(ln.shape),
            const(wr.shape), const(br.shape),
        ],
        out_specs=out_specs,
        out_shape=out_shapes,
        scratch_shapes=[pltpu.VMEM((HALO + tm, D), f32)],
        compiler_params=_cparams(("arbitrary",)),
        name="pool_mixer_router",
    )(x2, yprev, modp, lnp, mod, wp, ps, ln, wr, br)


def _final_kernel(x_ref, y_ref, mod_ref, ln_ref, o_ref):
    tm, D = x_ref.shape
    y = _rows_load(y_ref, tm, Y_ROWS, D)
    o_ref[...] = _layer_norm(DN_ALPHA * x_ref[...] + (1.0 + mod_ref[5:6, :]) * y,
                             ln_ref[0:1, :], ln_ref[1:2, :])


def _final_merge(x2, y, mod, ln, S, tm):
    N, D = x2.shape
    spb = S // tm
    return pl.pallas_call(
        _final_kernel,
        grid=(N // tm,),
        in_specs=[
            pl.BlockSpec((tm, D), lambda i: (i, 0)),
            pl.BlockSpec((tm * Y_ROWS, LANES), lambda i: (i, 0)),
            pl.BlockSpec((None, 8, D), lambda i: (i // spb, 0, 0)),
            pl.BlockSpec(ln.shape, lambda i: (0, 0)),
        ],
        out_specs=pl.BlockSpec((tm, D), lambda i: (i, 0)),
        out_shape=jax.ShapeDtypeStruct((N, D), f32),
        compiler_params=_cparams(("arbitrary",)),
        name="final_merge",
    )(x2, y, mod, ln)


def _rank_kernel(cls_ref, rank_ref, cnt_ref, carry_ref, earlier_ref):
    tr = cls_ref.shape[0]

    @pl.when(pl.program_id(0) == 0)
    def _():
        carry_ref[...] = jnp.zeros(carry_ref.shape, f32)
        r = lax.broadcasted_iota(jnp.int32, (tr, tr), 0)
        c = lax.broadcasted_iota(jnp.int32, (tr, tr), 1)
        earlier_ref[...] = (c < r).astype(bf16)

    lane = lax.broadcasted_iota(jnp.int32, (tr, LANES), 1)
    onehot = (cls_ref[...] == lane)
    oh = onehot.astype(bf16)
    before = _dot(earlier_ref[...], oh) + carry_ref[0:1, :]
    rank_ref[...] = jnp.sum(jnp.where(onehot, before, 0.0), axis=-1, keepdims=True).astype(jnp.int32)
    total = carry_ref[0:1, :] + jnp.sum(oh.astype(f32), axis=0, keepdims=True)
    carry_ref[...] = jnp.broadcast_to(total, carry_ref.shape)
    cnt_ref[...] = jnp.broadcast_to(total, cnt_ref.shape).astype(jnp.int32)


def _class_ranks(cls, tr):
    N = cls.shape[0]
    return pl.pallas_call(
        _rank_kernel,
        grid=(N // tr,),
        in_specs=[pl.BlockSpec((tr, 1), lambda i: (i, 0))],
        out_specs=[
            pl.BlockSpec((tr, 1), lambda i: (i, 0)),
            pl.BlockSpec((SUBLANES, LANES), lambda i: (0, 0)),
        ],
        out_shape=[
            jax.ShapeDtypeStruct((N, 1), jnp.int32),
            jax.ShapeDtypeStruct((SUBLANES, LANES), jnp.int32),
        ],
        scratch_shapes=[pltpu.VMEM((SUBLANES, LANES), f32), pltpu.VMEM((tr, tr), bf16)],
        compiler_params=_cparams(("arbitrary",)),
        name="class_ranks",
    )(cls)


def _slot_token_kernel(off_ref, cls_ref, rank_ref, inv_ref, *, chunk):
    base = pl.program_id(0) * chunk

    @pl.when(pl.program_id(0) == 0)
    def _():
        def clear(s, c):
            inv_ref[s] = 0
            return c
        lax.fori_loop(0, inv_ref.shape[0], clear, 0, unroll=32)

    def body(t, c):
        inv_ref[off_ref[cls_ref[t]] + rank_ref[t]] = base + t
        return c

    lax.fori_loop(0, chunk, body, 0, unroll=16)


def _slot_tokens(off, cls1, rank1, n_slots, chunk):
    N = cls1.shape[0]
    return pl.pallas_call(
        functools.partial(_slot_token_kernel, chunk=chunk),
        grid_spec=pltpu.PrefetchScalarGridSpec(
            num_scalar_prefetch=1,
            grid=(N // chunk,),
            in_specs=[
                pl.BlockSpec((chunk,), lambda i, off: (i,), memory_space=pltpu.SMEM),
                pl.BlockSpec((chunk,), lambda i, off: (i,), memory_space=pltpu.SMEM),
            ],
            out_specs=pl.BlockSpec((n_slots,), lambda i, off: (0,), memory_space=pltpu.SMEM),
        ),
        out_shape=jax.ShapeDtypeStruct((n_slots,), jnp.int32),
        compiler_params=_cparams(("arbitrary",)),
        name="slot_tokens",
    )(off, cls1, rank1)


def _moe_kernel(te1_ref, te2_ref, cnt_ref, inv_ref, hm_ref, wgu1_ref, wd1_ref, wgu2_ref, wd2_ref,
                y_ref, *scratch):
    hbufs = scratch[:MOE_SLOTS]
    ybufs = scratch[MOE_SLOTS:2 * MOE_SLOTS]
    gsem, ssem = scratch[2 * MOE_SLOTS:]
    ff, D = wd1_ref.shape
    i = pl.program_id(0)

    def count(tile):
        return jnp.where(tile >= 0, cnt_ref[jnp.maximum(tile, 0)], 0)

    n_cur, n_prev, n_ahead, n_old = count(i), count(i - 1), count(i + 2), count(i - MOE_SLOTS)

    def token_rows(ref, t, pitch):
        return ref.at[pl.ds(pl.multiple_of(t * pitch, pitch), pitch)]

    def start_gather(tok, r, hb, sem):
        pltpu.make_async_copy(token_rows(hm_ref, tok, HM_ROWS), token_rows(hb, r, HM_ROWS), sem).start()

    def start_scatter(tok, r, yb, sem):
        pltpu.make_async_copy(token_rows(yb, r, Y_ROWS), token_rows(y_ref, tok, Y_ROWS), sem).start()

    def wait_tokens(src, dst, sem, n, pitch):
        @pl.when(n > 0)
        def _():
            rows = pl.multiple_of(n * pitch, pitch)
            pltpu.make_async_copy(src.at[pl.ds(0, rows)], dst.at[pl.ds(0, rows)], sem).wait()

    @pl.when(i == 0)
    def _():
        for t in range(MOE_SLOTS - 1):
            def first(r, c):
                start_gather(inv_ref[t * ROW_TILE + r], r, hbufs[t], gsem.at[t])
                return c
            lax.fori_loop(0, cnt_ref[t], first, 0)

    def step(cur):
        far = (cur + MOE_SLOTS - 1) % MOE_SLOTS
        hb, yb = hbufs[cur], ybufs[cur]
        wait_tokens(hm_ref, hb, gsem.at[cur], n_cur, HM_ROWS)
        wait_tokens(yb, y_ref, ssem.at[cur], n_old, Y_ROWS)

        @pl.when((n_cur > 0) | (n_prev > 0))
        def _():
            ahead = (i + 2) * ROW_TILE
            prev = jnp.maximum(i - 1, 0) * ROW_TILE
            for r in range(ROW_TILE):
                tok_ahead = inv_ref[ahead + r]
                tok_prev = inv_ref[prev + r]

                @pl.when(r < n_ahead)
                def _():
                    start_gather(tok_ahead, r, hbufs[far], gsem.at[far])

                @pl.when(r < n_prev)
                def _():
                    start_scatter(tok_prev, r, ybufs[far], ssem.at[far])

            h = _rows_load(hb, ROW_TILE, HM_ROWS, D).astype(bf16)

            def hidden(gu, gate_row):
                w = hb[pl.ds(gate_row, ROW_TILE, stride=HM_ROWS), :]
                g, u = gu[:, :ff], gu[:, ff:]
                he = g * (1.0 / (1.0 + jnp.exp(-g))) * u
                return (he * jnp.concatenate([w] * (ff // LANES), axis=1)).astype(bf16)

            gu1 = _dot(h, wgu1_ref[...])
            gu2 = _dot(h, wgu2_ref[...])
            y = _dot(hidden(gu1, D // LANES), wd1_ref[...]) + _dot(hidden(gu2, D // LANES + 1), wd2_ref[...])
            _rows_store(yb, y, Y_ROWS)

    for cur in range(MOE_SLOTS):
        @pl.when(i % MOE_SLOTS == cur)
        def _():
            step(cur)


def _moe(te1, te2, cnt, inv, hm, wgu, wd, layer, n_steps):
    N = hm.shape[0] // HM_ROWS
    _, E, D, ff2 = wgu.shape
    ff = ff2 // 2

    def wmap(which):
        def f(i, te1, te2, cnt, inv):
            return (layer, (te1, te2)[which][i], 0, 0)
        return f

    return pl.pallas_call(
        _moe_kernel,
        grid_spec=pltpu.PrefetchScalarGridSpec(
            num_scalar_prefetch=4,
            grid=(n_steps,),
            in_specs=[
                pl.BlockSpec(memory_space=pl.ANY),
                pl.BlockSpec((None, None, D, ff2), wmap(0)),
                pl.BlockSpec((None, None, ff, D), wmap(0)),
                pl.BlockSpec((None, None, D, ff2), wmap(1)),
                pl.BlockSpec((None, None, ff, D), wmap(1)),
            ],
            out_specs=pl.BlockSpec(memory_space=pl.ANY),
            scratch_shapes=(
                [pltpu.VMEM((ROW_TILE * HM_ROWS, LANES), f32)] * MOE_SLOTS
                + [pltpu.VMEM((ROW_TILE * Y_ROWS, LANES), f32)] * MOE_SLOTS
                + [pltpu.SemaphoreType.DMA((MOE_SLOTS,)), pltpu.SemaphoreType.DMA((MOE_SLOTS,))]
            ),
        ),
        out_shape=jax.ShapeDtypeStruct((N * Y_ROWS, LANES), f32),
        compiler_params=_cparams(("arbitrary",)),
        name="pair_moe",
    )(te1, te2, cnt, inv, hm, wgu, wd, wgu, wd)


def _pair_tables():
    lo = np.zeros((N_CLASSES,), np.int32)
    hi = np.zeros((N_CLASSES,), np.int32)
    for g in range(MOE_GROUPS):
        p = 0
        for a in range(EXPERTS_PER_GROUP):
            for b in range(a + 1, EXPERTS_PER_GROUP):
                lo[g * PAIRS_PER_GROUP + p] = g * EXPERTS_PER_GROUP + a
                hi[g * PAIRS_PER_GROUP + p] = g * EXPERTS_PER_GROUP + b
                p += 1
    return lo, hi


def _moe_layer(hm, cls, expert_w, layer, chunk, tr):
    N = cls.shape[0]
    n_tiles = N // ROW_TILE + N_CLASSES
    n_steps = n_tiles + MOE_SLOTS
    rank, cnt = _class_ranks(cls, tr)
    counts = cnt[0, :N_CLASSES]
    tiles = (counts + ROW_TILE - 1) // ROW_TILE
    tile_end = jnp.cumsum(tiles)
    tile_start = tile_end - tiles
    off = jnp.zeros((LANES,), jnp.int32).at[:N_CLASSES].set(tile_start * ROW_TILE)
    nused = tile_end[-1]
    step = jnp.arange(n_steps + 2, dtype=jnp.int32)
    tile_cls = jnp.sum(tile_end[None, :] <= jnp.minimum(step, nused - 1)[:, None], axis=1)
    tile_cls = jnp.minimum(tile_cls, N_CLASSES - 1).astype(jnp.int32)
    lo_tab, hi_tab = _pair_tables()
    te1 = jnp.asarray(lo_tab)[tile_cls]
    te2 = jnp.asarray(hi_tab)[tile_cls]
    in_cls = (step - tile_start[tile_cls]) * ROW_TILE
    tile_cnt = jnp.where(step < nused, jnp.clip(counts[tile_cls] - in_cls, 0, ROW_TILE), 0).astype(jnp.int32)

    n_slots = -(-(n_tiles + MOE_SLOTS) * ROW_TILE // SMEM_BLOCK) * SMEM_BLOCK
    inv = _slot_tokens(off, cls.reshape(N), rank.reshape(N), n_slots, chunk)
    return _moe(te1, te2, tile_cnt, inv, hm, *expert_w, layer, n_steps)


def _swap_rope(w):
    half = QK_ROPE // 2
    return jnp.concatenate([-w[..., half:], w[..., :half]], axis=-1)


def _prep_mla_weights(w_in, w_uq, w_ukv):
    D = w_in.shape[0]
    w_kr = w_in[:, Q_LORA + KV_LORA:]
    w_ks = _swap_rope(w_kr)
    zeros = jnp.zeros((D, QK_NOPE), w_in.dtype)
    win = jnp.concatenate([w_in[:, :Q_LORA + KV_LORA], zeros, w_kr, w_kr, zeros, w_ks, w_ks], axis=1)
    uq = w_uq.reshape(Q_LORA, MLA_HEADS, QK_NOPE + QK_ROPE)
    rope = uq[..., QK_NOPE:]
    wuq = jnp.concatenate([uq[..., :QK_NOPE], rope, _swap_rope(rope)], axis=-1).reshape(Q_LORA, -1)
    ukv = w_ukv.reshape(KV_LORA, MLA_HEADS, QK_NOPE + V_HEAD)
    wkn = jnp.concatenate([ukv[..., :QK_NOPE], jnp.zeros((KV_LORA, MLA_HEADS, HEAD_W - QK_NOPE), w_ukv.dtype)],
                          axis=-1).reshape(KV_LORA, -1)
    wv = ukv[..., QK_NOPE:].reshape(KV_LORA, -1)
    return win.astype(bf16), wuq.T.astype(bf16), wkn.astype(bf16), wv.T.astype(bf16)


def _rope_inv_freq():
    inv = 1.0 / (ROPE_THETA ** (np.arange(0, QK_ROPE, 2, dtype=np.float32) / QK_ROPE))
    inv2 = np.concatenate([inv, inv]).astype(np.float32)
    return np.concatenate([np.zeros((QK_NOPE,), np.float32), inv2, inv2]).reshape(1, LANES)


def _prep_router(w_gr, b_gr, w_er, b_er):
    D = w_gr.shape[0]
    n = MOE_GROUPS + MOE_GROUPS * EXPERTS_PER_GROUP
    w = jnp.concatenate([w_gr, w_er, jnp.zeros((D, LANES - n), f32)], axis=1)
    b = jnp.concatenate([b_gr, b_er, jnp.zeros((LANES - n,), f32)]).reshape(1, LANES)
    hi, lo = _split_bf16(w)
    return jnp.concatenate([hi, lo], axis=1), b


def kernel(x, c, positions, w_mod, b_mod, ln_g, ln_b, w_in, q_norm_g, kv_norm_g, w_uq, w_ukv, w_o, w_pool,
           pool_scale, w_group_router, b_group_router, w_expert_router, b_expert_router, w_gate, w_up, w_down):
    B, S, D = x.shape
    N = B * S
    tm = min(512, S)
    tq = min(1024, S)
    tr = min(1024, N)
    chunk = min(2048, N)

    mod = _modulation(c, w_mod, b_mod)
    ln = jnp.stack([ln_g, ln_b], axis=2)
    x2 = x.reshape(N, D)

    win, wuqt, wkn, wvt = _prep_mla_weights(w_in[0], w_uq[0], w_ukv[0])
    qt, k, vt = _mla_proj(x2, positions.reshape(N, 1), mod[0], win, q_norm_g[0].reshape(1, -1),
                          kv_norm_g[0].reshape(1, -1), wuqt, wkn, wvt, jnp.asarray(_rope_inv_freq()),
                          B, S, tm, tq)
    o = _attention(qt, k, vt, tq)
    wr, br = _prep_router(w_group_router[0], b_group_router[0], w_expert_router[0], b_expert_router[0])
    x1, hm, cls = _attn_out(o.reshape(N, D), x2, mod[0], w_o[0].astype(bf16), ln[0, 0], wr, br, S, tm)
    expert_w = (jnp.concatenate([w_gate, w_up], axis=-1).astype(bf16), w_down.astype(bf16))
    y0 = _moe_layer(hm, cls, expert_w, 0, chunk, tr)

    wr, br = _prep_router(w_group_router[1], b_group_router[1], w_expert_router[1], b_expert_router[1])
    x3, hm, cls = _pool_layer(x1, y0, mod[0], ln[0, 1], mod[1], w_pool[0].astype(bf16),
                              pool_scale[0].reshape(1, D), ln[1, 0], wr, br, S, tm)
    y1 = _moe_layer(hm, cls, expert_w, 1, chunk, tr)

    out = _final_merge(x3, y1, mod[1], ln[1, 1], S, tm)
    return out.reshape(B, S, D)
```

```python
import functools

import numpy as np
import jax
import jax.numpy as jnp
from jax import lax
from jax.experimental import pallas as pl
from jax.experimental.pallas import tpu as pltpu

MLA_HEADS = 16
Q_LORA = 384
KV_LORA = 256
QK_NOPE = 64
QK_ROPE = 32
V_HEAD = 64
ROPE_THETA = 10000.0
POOL_WINDOWS = (2, 4, 8, 16)
MOE_GROUPS = 4
EXPERTS_PER_GROUP = 8
DEPTH = 2
DN_ALPHA = (2.0 * DEPTH) ** 0.25
LN_EPS = 1e-5
RMS_EPS = 1e-6
LOG2E = 1.4426950408889634

LANES = 128
SUBLANES = 8
VMEM_LIMIT = 56 * 1024 * 1024

HEAD_W = 128
PAIRS_PER_GROUP = EXPERTS_PER_GROUP * (EXPERTS_PER_GROUP - 1) // 2
N_CLASSES = MOE_GROUPS * PAIRS_PER_GROUP
ROW_TILE = 128
HM_ROWS = 16
Y_ROWS = 8
MOE_SLOTS = 3
SMEM_BLOCK = 1024
HALO = 16

f32 = jnp.float32
bf16 = jnp.bfloat16


def _cparams(sem):
    return pltpu.CompilerParams(dimension_semantics=sem, vmem_limit_bytes=VMEM_LIMIT)


def _split_bf16(a):
    hi = a.astype(bf16)
    lo = (a - hi.astype(f32)).astype(bf16)
    return hi, lo


def _dot(a, b):
    return jnp.dot(a, b, preferred_element_type=f32)


def _dot3(a_hi, a_lo, b_hi, b_lo):
    return _dot(a_hi, b_hi) + (_dot(a_lo, b_hi) + _dot(a_hi, b_lo))


def _rows_load(ref, n, pitch, width):
    return jnp.concatenate([ref[pl.ds(s, n, stride=pitch), :] for s in range(width // LANES)], axis=1)


def _rows_store(ref, val, pitch):
    n, width = val.shape
    for s in range(width // LANES):
        ref[pl.ds(s, n, stride=pitch), :] = val[:, s * LANES:(s + 1) * LANES]


def _layer_norm(x, g, b):
    mu = jnp.mean(x, axis=-1, keepdims=True)
    xc = x - mu
    var = jnp.mean(xc * xc, axis=-1, keepdims=True)
    return xc * lax.rsqrt(var + LN_EPS) * g + b


def _mod_kernel(c_ref, w_ref, b_ref, o_ref):
    c = c_ref[...]
    ca = c * (1.0 / (1.0 + jnp.exp(-c)))
    a_hi, a_lo = _split_bf16(ca)
    w_hi, w_lo = _split_bf16(w_ref[...])
    o_ref[...] = _dot3(a_hi, a_lo, w_hi, w_lo) + b_ref[...]


def _modulation(c, w_mod, b_mod):
    B, D = c.shape
    depth = w_mod.shape[0]
    out = pl.pallas_call(
        _mod_kernel,
        grid=(depth, 6),
        in_specs=[
            pl.BlockSpec((B, D), lambda i, j: (0, 0)),
            pl.BlockSpec((None, D, D), lambda i, j: (i, 0, j)),
            pl.BlockSpec((None, None, 1, D), lambda i, j: (i, j, 0, 0)),
        ],
        out_specs=pl.BlockSpec((None, None, B, D), lambda i, j: (i, j, 0, 0)),
        out_shape=jax.ShapeDtypeStruct((depth, 6, B, D), f32),
        compiler_params=_cparams(("arbitrary", "arbitrary")),
        name="adaln_mod",
    )(c, w_mod, b_mod.reshape(depth, 6, 1, D))
    out = jnp.transpose(out, (0, 2, 1, 3))
    return jnp.pad(out, ((0, 0), (0, 0), (0, 2), (0, 0)))


def _mla_proj_kernel(x_ref, pos_ref, mod_ref, win_ref, qg_ref, kg_ref, wuqt_ref, wkn_ref, wvt_ref,
                     invf_ref, qt_ref, k_ref, vt_ref):
    x = x_ref[...]
    h = x * (1.0 + mod_ref[1:2, :]) + mod_ref[0:1, :]
    z = _dot(h.astype(bf16), win_ref[...])
    zq = z[:, :Q_LORA]
    zkv = z[:, Q_LORA:Q_LORA + KV_LORA]
    za = z[:, Q_LORA + KV_LORA:Q_LORA + KV_LORA + LANES]
    zb = z[:, Q_LORA + KV_LORA + LANES:]
    q_lat = zq * lax.rsqrt(jnp.mean(zq * zq, axis=-1, keepdims=True) + RMS_EPS) * qg_ref[...]
    kv_lat = zkv * lax.rsqrt(jnp.mean(zkv * zkv, axis=-1, keepdims=True) + RMS_EPS) * kg_ref[...]

    ang = pos_ref[...].astype(f32) * invf_ref[...]
    cs = jnp.cos(ang)
    sn = jnp.sin(ang)
    lane = lax.broadcasted_iota(jnp.int32, cs.shape, 1)
    scale = (QK_NOPE + QK_ROPE) ** -0.5 * LOG2E
    tqn = scale * jnp.where(lane < QK_NOPE + QK_ROPE, cs, sn)
    kr = za * cs + zb * sn

    kn = _dot(kv_lat.astype(bf16), wkn_ref[...])
    for hd in range(MLA_HEADS):
        k_ref[hd] = (kn[:, hd * HEAD_W:(hd + 1) * HEAD_W] + kr).astype(bf16)

    tqt = tqn.T
    qt = _dot(wuqt_ref[...], q_lat.T.astype(bf16))
    for hd in range(MLA_HEADS):
        qt_ref[hd] = (qt[hd * HEAD_W:(hd + 1) * HEAD_W, :] * tqt).astype(bf16)
    vt_ref[...] = _dot(wvt_ref[...], kv_lat.T.astype(bf16)).astype(bf16)


def _mla_proj(x2, pos2, mod, win, qg, kg, wuqt, wkn, wvt, invf, B, S, tm, tq):
    N, D = x2.shape
    spb = S // tm
    per_q = tq // tm
    const = lambda shape: pl.BlockSpec(shape, lambda i: (0,) * len(shape))
    nq = S // tq
    return pl.pallas_call(
        _mla_proj_kernel,
        grid=(N // tm,),
        in_specs=[
            pl.BlockSpec((tm, D), lambda i: (i, 0)),
            pl.BlockSpec((tm, 1), lambda i: (i, 0)),
            pl.BlockSpec((None, 8, D), lambda i: (i // spb, 0, 0)),
            const(win.shape), const(qg.shape), const(kg.shape), const(wuqt.shape), const(wkn.shape),
            const(wvt.shape), const(invf.shape),
        ],
        out_specs=[
            pl.BlockSpec((None, MLA_HEADS, None, HEAD_W, tm),
                         lambda i: (i // spb, 0, (i % spb) // per_q, 0, i % per_q)),
            pl.BlockSpec((None, MLA_HEADS, tm, HEAD_W), lambda i: (i // spb, 0, i % spb, 0)),
            pl.BlockSpec((None, None, MLA_HEADS * V_HEAD, tm),
                         lambda i: (i // spb, (i % spb) // per_q, 0, i % per_q)),
        ],
        out_shape=[
            jax.ShapeDtypeStruct((B, MLA_HEADS, nq, HEAD_W, tq), bf16),
            jax.ShapeDtypeStruct((B, MLA_HEADS, S, HEAD_W), bf16),
            jax.ShapeDtypeStruct((B, nq, MLA_HEADS * V_HEAD, tq), bf16),
        ],
        compiler_params=_cparams(("arbitrary",)),
        name="mla_proj",
    )(x2, pos2, mod, win, qg, kg, wuqt, wkn, wvt, invf)


ONES_ROWS = 16


def _attn_kernel(qt_ref, k_ref, vt_ref, o_ref, vx_ref, m_ref, acc_ref, ot_ref, *, tq):
    nq = qt_ref.shape[1]
    for hh in range(2):
        for j in range(nq):
            vx_ref[hh, j, :V_HEAD, :] = vt_ref[j, hh * V_HEAD:(hh + 1) * V_HEAD, :]
            vx_ref[hh, j, V_HEAD:, :] = jnp.ones((ONES_ROWS, tq), bf16)

    def kv_step(qi, j, masked):
        k0 = j * tq
        scores = [_dot(k_ref[hh, pl.ds(k0, tq), :], qt_ref[hh, qi]) for hh in range(2)]
        for hh in range(2):
            s = scores[hh]
            if masked:
                key = lax.broadcasted_iota(jnp.int32, (tq, tq), 0)
                qry = lax.broadcasted_iota(jnp.int32, (tq, tq), 1)
                s = jnp.where(key <= qry, s, -jnp.inf)
            m_old = m_ref[hh]
            m_new = jnp.maximum(m_old, jnp.max(s, axis=0, keepdims=True))
            alpha = jnp.exp2(m_old - m_new)
            p = jnp.exp2(s - m_new[0:1, :])
            pv = _dot(vx_ref[hh, j], p.astype(bf16))
            acc_ref[hh] = acc_ref[hh] * alpha[0:1, :] + pv
            m_ref[hh] = m_new

    def q_body(qi, carry):
        m_ref[...] = jnp.full(m_ref.shape, -jnp.inf, f32)
        acc_ref[...] = jnp.zeros(acc_ref.shape, f32)

        for j in range(qi):
            kv_step(qi, j, False)
        kv_step(qi, qi, True)
        for hh in range(2):
            acc = acc_ref[hh]
            ot_ref[hh * V_HEAD:(hh + 1) * V_HEAD, :] = acc[:V_HEAD, :] * (1.0 / acc[V_HEAD:V_HEAD + 1, :])
        o_ref[pl.ds(qi * tq, tq), :] = ot_ref[...].T.astype(bf16)
        return carry

    for qi in range(nq):
        q_body(qi, 0)


def _attention(qt, k, vt, tq):
    B, H, nq, W, _ = qt.shape
    S = nq * tq
    return pl.pallas_call(
        functools.partial(_attn_kernel, tq=tq),
        grid=(B, H // 2),
        in_specs=[
            pl.BlockSpec((None, 2, nq, W, tq), lambda b, p: (b, p, 0, 0, 0)),
            pl.BlockSpec((None, 2, S, W), lambda b, p: (b, p, 0, 0)),
            pl.BlockSpec((None, nq, 2 * V_HEAD, tq), lambda b, p: (b, 0, p, 0)),
        ],
        out_specs=pl.BlockSpec((None, S, LANES), lambda b, p: (b, 0, p)),
        out_shape=jax.ShapeDtypeStruct((B, S, H * V_HEAD), bf16),
        scratch_shapes=[
            pltpu.VMEM((2, nq, V_HEAD + ONES_ROWS, tq), bf16),
            pltpu.VMEM((2, SUBLANES, tq), f32),
            pltpu.VMEM((2, V_HEAD + ONES_ROWS, tq), f32),
            pltpu.VMEM((2 * V_HEAD, tq), f32),
        ],
        compiler_params=_cparams(("arbitrary", "arbitrary")),
        name="mla_attention",
    )(qt, k, vt)


def _post_mixer(x, y, mod_ref, ln_ref, wr_ref, br_ref, xo_ref, hm_ref, cls_ref):
    tm, D = x.shape
    x1 = _layer_norm(DN_ALPHA * x + (1.0 + mod_ref[2:3, :]) * y, ln_ref[0:1, :], ln_ref[1:2, :])
    xo_ref[...] = x1
    h2 = x1 * (1.0 + mod_ref[4:5, :]) + mod_ref[3:4, :]
    h_hi, h_lo = _split_bf16(h2)
    wr = wr_ref[...]
    t_hi = _dot(h_hi, wr)
    t_lo = _dot(h_lo, wr)
    logits = t_hi[:, :LANES] + (t_hi[:, LANES:] + t_lo[:, :LANES]) + br_ref[...]

    lane = lax.broadcasted_iota(jnp.int32, logits.shape, 1).astype(f32)
    neg = -jnp.inf
    far = float(LANES)
    is_g = lane < MOE_GROUPS
    gl = jnp.where(is_g, logits, neg)
    gmax = jnp.max(gl, axis=-1, keepdims=True)
    gidx = jnp.min(jnp.where(gl == gmax, lane, far), axis=-1, keepdims=True)
    g_p = 1.0 / jnp.sum(jnp.where(is_g, jnp.exp(logits - gmax), 0.0), axis=-1, keepdims=True)
    base = MOE_GROUPS + EXPERTS_PER_GROUP * gidx
    el = jnp.where((lane >= base) & (lane < base + EXPERTS_PER_GROUP), logits, neg)
    t1 = jnp.max(el, axis=-1, keepdims=True)
    i1 = jnp.min(jnp.where(el == t1, lane, far), axis=-1, keepdims=True)
    el2 = jnp.where(lane == i1, neg, el)
    t2 = jnp.max(el2, axis=-1, keepdims=True)
    i2 = jnp.min(jnp.where(el2 == t2, lane, far), axis=-1, keepdims=True)
    e = jnp.exp(t2 - t1)
    w1 = g_p / (1.0 + e)
    w2 = g_p * e / (1.0 + e)
    a = i1 - base
    b = i2 - base
    a_first = a < b
    lo = jnp.where(a_first, a, b)
    hi = jnp.where(a_first, b, a)
    w_lo = jnp.where(a_first, w1, w2)
    w_hi = jnp.where(a_first, w2, w1)
    pair = lo * (2 * EXPERTS_PER_GROUP - 1 - lo) * 0.5 + (hi - lo - 1.0)
    cls_ref[...] = (gidx * PAIRS_PER_GROUP + pair).astype(jnp.int32)

    _rows_store(hm_ref, h2, HM_ROWS)
    hm_ref.reshape(tm, HM_ROWS, LANES)[:, D // LANES:, :] = jnp.zeros((tm, HM_ROWS - D // LANES, LANES), f32)
    hm_ref[pl.ds(D // LANES, tm, stride=HM_ROWS), :] = jnp.broadcast_to(w_lo, (tm, LANES))
    hm_ref[pl.ds(D // LANES + 1, tm, stride=HM_ROWS), :] = jnp.broadcast_to(w_hi, (tm, LANES))


def _attn_out_kernel(o_ref, x_ref, mod_ref, wo_ref, ln_ref, wr_ref, br_ref,
                     xo_ref, hm_ref, cls_ref):
    y = _dot(o_ref[...], wo_ref[...])
    _post_mixer(x_ref[...], y, mod_ref, ln_ref, wr_ref, br_ref, xo_ref, hm_ref, cls_ref)


def _post_out_specs(N, D, tm):
    specs = [
        pl.BlockSpec((tm, D), lambda i: (i, 0)),
        pl.BlockSpec((tm * HM_ROWS, LANES), lambda i: (i, 0)),
        pl.BlockSpec((tm, 1), lambda i: (i, 0)),
    ]
    shapes = [
        jax.ShapeDtypeStruct((N, D), f32),
        jax.ShapeDtypeStruct((N * HM_ROWS, LANES), f32),
        jax.ShapeDtypeStruct((N, 1), jnp.int32),
    ]
    return specs, shapes


def _attn_out(o2, x2, mod, wo, ln, wr, br, S, tm):
    N, D = x2.shape
    spb = S // tm
    const = lambda shape: pl.BlockSpec(shape, lambda i: (0,) * len(shape))
    out_specs, out_shapes = _post_out_specs(N, D, tm)
    return pl.pallas_call(
        _attn_out_kernel,
        grid=(N // tm,),
        in_specs=[
            pl.BlockSpec((tm, D), lambda i: (i, 0)),
            pl.BlockSpec((tm, D), lambda i: (i, 0)),
            pl.BlockSpec((None, 8, D), lambda i: (i // spb, 0, 0)),
            const(wo.shape), const(ln.shape), const(wr.shape), const(br.shape),
        ],
        out_specs=out_specs,
        out_shape=out_shapes,
        compiler_params=_cparams(("arbitrary",)),
        name="attn_out_router",
    )(o2, x2, mod, wo, ln, wr, br)


def _pool_kernel(x_ref, y_ref, modp_ref, lnp_ref, mod_ref, wp_ref, ps_ref, ln_ref,
                 wr_ref, br_ref, xo_ref, hm_ref, cls_ref, hb_ref, *, spb):
    tm, D = x_ref.shape
    i = pl.program_id(0)
    t_blk = i % spb
    yprev = _rows_load(y_ref, tm, Y_ROWS, D)
    x2 = _layer_norm(DN_ALPHA * x_ref[...] + (1.0 + modp_ref[5:6, :]) * yprev,
                     lnp_ref[0:1, :], lnp_ref[1:2, :])
    h = x2 * (1.0 + mod_ref[1:2, :]) + mod_ref[0:1, :]

    @pl.when(t_blk == 0)
    def _():
        hb_ref[0:HALO, :] = jnp.zeros((HALO, D), f32)

    hb_ref[HALO:, :] = h
    t_seq = t_blk * tm + lax.broadcasted_iota(jnp.int32, (tm, 1), 0)
    gd = D // len(POOL_WINDOWS)
    ys = []
    for gi, w in enumerate(POOL_WINDOWS):
        c0, c1 = gi * gd, (gi + 1) * gd
        a = hb_ref[:, c0:c1]
        span = 1
        while span < w:
            a = a[span:, :] + a[:-span, :]
            span *= 2
        tsum = a[HALO - (w - 1):, :]
        cnt = jnp.minimum(t_seq + 1, w).astype(f32)
        mixed = tsum / cnt - h[:, c0:c1]
        ys.append(_dot(mixed.astype(bf16), wp_ref[gi]))
    hb_ref[0:HALO, :] = h[tm - HALO:, :]
    y = jnp.concatenate(ys, axis=1) * ps_ref[...]
    _post_mixer(x2, y, mod_ref, ln_ref, wr_ref, br_ref, xo_ref, hm_ref, cls_ref)


def _pool_layer(x2, yprev, modp, lnp, mod, wp, ps, ln, wr, br, S, tm):
    N, D = x2.shape
    spb = S // tm
    const = lambda shape: pl.BlockSpec(shape, lambda i: (0,) * len(shape))
    out_specs, out_shapes = _post_out_specs(N, D, tm)
    return pl.pallas_call(
        functools.partial(_pool_kernel, spb=spb),
        grid=(N // tm,),
        in_specs=[
            pl.BlockSpec((tm, D), lambda i: (i, 0)),
            pl.BlockSpec((tm * Y_ROWS, LANES), lambda i: (i, 0)),
            pl.BlockSpec((None, 8, D), lambda i: (i // spb, 0, 0)),
            const(lnp.shape),
            pl.BlockSpec((None, 8, D), lambda i: (i // spb, 0, 0)),
            const(wp.shape), const(ps.shape), const(ln.shape),
            const(wr.shape), const(br.shape),
        ],
        out_specs=out_specs,
        out_shape=out_shapes,
        scratch_shapes=[pltpu.VMEM((HALO + tm, D), f32)],
        compiler_params=_cparams(("arbitrary",)),
        name="pool_mixer_router",
    )(x2, yprev, modp, lnp, mod, wp, ps, ln, wr, br)


def _final_kernel(x_ref, y_ref, mod_ref, ln_ref, o_ref):
    tm, D = x_ref.shape
    y = _rows_load(y_ref, tm, Y_ROWS, D)
    o_ref[...] = _layer_norm(DN_ALPHA * x_ref[...] + (1.0 + mod_ref[5:6, :]) * y,
                             ln_ref[0:1, :], ln_ref[1:2, :])


def _final_merge(x2, y, mod, ln, S, tm):
    N, D = x2.shape
    spb = S // tm
    return pl.pallas_call(
        _final_kernel,
        grid=(N // tm,),
        in_specs=[
            pl.BlockSpec((tm, D), lambda i: (i, 0)),
            pl.BlockSpec((tm * Y_ROWS, LANES), lambda i: (i, 0)),
            pl.BlockSpec((None, 8, D), lambda i: (i // spb, 0, 0)),
            pl.BlockSpec(ln.shape, lambda i: (0, 0)),
        ],
        out_specs=pl.BlockSpec((tm, D), lambda i: (i, 0)),
        out_shape=jax.ShapeDtypeStruct((N, D), f32),
        compiler_params=_cparams(("arbitrary",)),
        name="final_merge",
    )(x2, y, mod, ln)


def _rank_kernel(cls_ref, rank_ref, cnt_ref, carry_ref, earlier_ref):
    tr = cls_ref.shape[0]

    @pl.when(pl.program_id(0) == 0)
    def _():
        carry_ref[...] = jnp.zeros(carry_ref.shape, f32)
        r = lax.broadcasted_iota(jnp.int32, (tr, tr), 0)
        c = lax.broadcasted_iota(jnp.int32, (tr, tr), 1)
        earlier_ref[...] = (c < r).astype(bf16)

    lane = lax.broadcasted_iota(jnp.int32, (tr, LANES), 1)
    onehot = (cls_ref[...] == lane)
    oh = onehot.astype(bf16)
    before = _dot(earlier_ref[...], oh) + carry_ref[0:1, :]
    rank_ref[...] = jnp.sum(jnp.where(onehot, before, 0.0), axis=-1, keepdims=True).astype(jnp.int32)
    total = carry_ref[0:1, :] + jnp.sum(oh.astype(f32), axis=0, keepdims=True)
    carry_ref[...] = jnp.broadcast_to(total, carry_ref.shape)
    cnt_ref[...] = jnp.broadcast_to(total, cnt_ref.shape).astype(jnp.int32)


def _class_ranks(cls, tr):
    N = cls.shape[0]
    return pl.pallas_call(
        _rank_kernel,
        grid=(N // tr,),
        in_specs=[pl.BlockSpec((tr, 1), lambda i: (i, 0))],
        out_specs=[
            pl.BlockSpec((tr, 1), lambda i: (i, 0)),
            pl.BlockSpec((SUBLANES, LANES), lambda i: (0, 0)),
        ],
        out_shape=[
            jax.ShapeDtypeStruct((N, 1), jnp.int32),
            jax.ShapeDtypeStruct((SUBLANES, LANES), jnp.int32),
        ],
        scratch_shapes=[pltpu.VMEM((SUBLANES, LANES), f32), pltpu.VMEM((tr, tr), bf16)],
        compiler_params=_cparams(("arbitrary",)),
        name="class_ranks",
    )(cls)


def _slot_token_kernel(off_ref, cls_ref, rank_ref, inv_ref, *, chunk):
    base = pl.program_id(0) * chunk

    @pl.when(pl.program_id(0) == 0)
    def _():
        def clear(s, c):
            inv_ref[s] = 0
            return c
        lax.fori_loop(0, inv_ref.shape[0], clear, 0, unroll=32)

    def body(t, c):
        inv_ref[off_ref[cls_ref[t]] + rank_ref[t]] = base + t
        return c

    lax.fori_loop(0, chunk, body, 0, unroll=16)


def _slot_tokens(off, cls1, rank1, n_slots, chunk):
    N = cls1.shape[0]
    return pl.pallas_call(
        functools.partial(_slot_token_kernel, chunk=chunk),
        grid_spec=pltpu.PrefetchScalarGridSpec(
            num_scalar_prefetch=1,
            grid=(N // chunk,),
            in_specs=[
                pl.BlockSpec((chunk,), lambda i, off: (i,), memory_space=pltpu.SMEM),
                pl.BlockSpec((chunk,), lambda i, off: (i,), memory_space=pltpu.SMEM),
            ],
            out_specs=pl.BlockSpec((n_slots,), lambda i, off: (0,), memory_space=pltpu.SMEM),
        ),
        out_shape=jax.ShapeDtypeStruct((n_slots,), jnp.int32),
        compiler_params=_cparams(("arbitrary",)),
        name="slot_tokens",
    )(off, cls1, rank1)


def _moe_kernel(te1_ref, te2_ref, cnt_ref, inv_ref, hm_ref, wgu1_ref, wd1_ref, wgu2_ref, wd2_ref,
                y_ref, *scratch):
    hbufs = scratch[:MOE_SLOTS]
    ybufs = scratch[MOE_SLOTS:2 * MOE_SLOTS]
    gsem, ssem = scratch[2 * MOE_SLOTS:]
    ff, D = wd1_ref.shape
    i = pl.program_id(0)

    def count(tile):
        return jnp.where(tile >= 0, cnt_ref[jnp.maximum(tile, 0)], 0)

    n_cur, n_prev, n_ahead, n_old = count(i), count(i - 1), count(i + 2), count(i - MOE_SLOTS)

    def token_rows(ref, t, pitch):
        return ref.at[pl.ds(pl.multiple_of(t * pitch, pitch), pitch)]

    def start_gather(tok, r, hb, sem):
        pltpu.make_async_copy(token_rows(hm_ref, tok, HM_ROWS), token_rows(hb, r, HM_ROWS), sem).start()

    def start_scatter(tok, r, yb, sem):
        pltpu.make_async_copy(token_rows(yb, r, Y_ROWS), token_rows(y_ref, tok, Y_ROWS), sem).start()

    def wait_tokens(src, dst, sem, n, pitch):
        @pl.when(n > 0)
        def _():
            rows = pl.multiple_of(n * pitch, pitch)
            pltpu.make_async_copy(src.at[pl.ds(0, rows)], dst.at[pl.ds(0, rows)], sem).wait()

    @pl.when(i == 0)
    def _():
        for t in range(MOE_SLOTS - 1):
            def first(r, c):
                start_gather(inv_ref[t * ROW_TILE + r], r, hbufs[t], gsem.at[t])
                return c
            lax.fori_loop(0, cnt_ref[t], first, 0)

    def step(cur):
        far = (cur + MOE_SLOTS - 1) % MOE_SLOTS
        hb, yb = hbufs[cur], ybufs[cur]
        wait_tokens(hm_ref, hb, gsem.at[cur], n_cur, HM_ROWS)
        wait_tokens(yb, y_ref, ssem.at[cur], n_old, Y_ROWS)

        @pl.when((n_cur > 0) | (n_prev > 0))
        def _():
            ahead = (i + 2) * ROW_TILE
            prev = jnp.maximum(i - 1, 0) * ROW_TILE
            for r in range(ROW_TILE):
                tok_ahead = inv_ref[ahead + r]
                tok_prev = inv_ref[prev + r]

                @pl.when(r < n_ahead)
                def _():
                    start_gather(tok_ahead, r, hbufs[far], gsem.at[far])

                @pl.when(r < n_prev)
                def _():
                    start_scatter(tok_prev, r, ybufs[far], ssem.at[far])

            h = _rows_load(hb, ROW_TILE, HM_ROWS, D).astype(bf16)

            def hidden(gu, gate_row):
                w = hb[pl.ds(gate_row, ROW_TILE, stride=HM_ROWS), :]
                g, u = gu[:, :ff], gu[:, ff:]
                he = g * (1.0 / (1.0 + jnp.exp(-g))) * u
                return (he * jnp.concatenate([w] * (ff // LANES), axis=1)).astype(bf16)

            gu1 = _dot(h, wgu1_ref[...])
            gu2 = _dot(h, wgu2_ref[...])
            y = _dot(hidden(gu1, D // LANES), wd1_ref[...]) + _dot(hidden(gu2, D // LANES + 1), wd2_ref[...])
            _rows_store(yb, y, Y_ROWS)

    for cur in range(MOE_SLOTS):
        @pl.when(i % MOE_SLOTS == cur)
        def _():
            step(cur)


def _moe(te1, te2, cnt, inv, hm, wgu, wd, layer, n_steps):
    N = hm.shape[0] // HM_ROWS
    _, E, D, ff2 = wgu.shape
    ff = ff2 // 2

    def wmap(which):
        def f(i, te1, te2, cnt, inv):
            return (layer, (te1, te2)[which][i], 0, 0)
        return f

    return pl.pallas_call(
        _moe_kernel,
        grid_spec=pltpu.PrefetchScalarGridSpec(
            num_scalar_prefetch=4,
            grid=(n_steps,),
            in_specs=[
                pl.BlockSpec(memory_space=pl.ANY),
                pl.BlockSpec((None, None, D, ff2), wmap(0)),
                pl.BlockSpec((None, None, ff, D), wmap(0)),
                pl.BlockSpec((None, None, D, ff2), wmap(1)),
                pl.BlockSpec((None, None, ff, D), wmap(1)),
            ],
            out_specs=pl.BlockSpec(memory_space=pl.ANY),
            scratch_shapes=(
                [pltpu.VMEM((ROW_TILE * HM_ROWS, LANES), f32)] * MOE_SLOTS
                + [pltpu.VMEM((ROW_TILE * Y_ROWS, LANES), f32)] * MOE_SLOTS
                + [pltpu.SemaphoreType.DMA((MOE_SLOTS,)), pltpu.SemaphoreType.DMA((MOE_SLOTS,))]
            ),
        ),
        out_shape=jax.ShapeDtypeStruct((N * Y_ROWS, LANES), f32),
        compiler_params=_cparams(("arbitrary",)),
        name="pair_moe",
    )(te1, te2, cnt, inv, hm, wgu, wd, wgu, wd)


def _pair_tables():
    lo = np.zeros((N_CLASSES,), np.int32)
    hi = np.zeros((N_CLASSES,), np.int32)
    for g in range(MOE_GROUPS):
        p = 0
        for a in range(EXPERTS_PER_GROUP):
            for b in range(a + 1, EXPERTS_PER_GROUP):
                lo[g * PAIRS_PER_GROUP + p] = g * EXPERTS_PER_GROUP + a
                hi[g * PAIRS_PER_GROUP + p] = g * EXPERTS_PER_GROUP + b
                p += 1
    return lo, hi


def _moe_layer(hm, cls, expert_w, layer, chunk, tr):
    N = cls.shape[0]
    n_tiles = N // ROW_TILE + N_CLASSES
    n_steps = n_tiles + MOE_SLOTS
    rank, cnt = _class_ranks(cls, tr)
    counts = cnt[0, :N_CLASSES]
    tiles = (counts + ROW_TILE - 1) // ROW_TILE
    tile_end = jnp.cumsum(tiles)
    tile_start = tile_end - tiles
    off = jnp.zeros((LANES,), jnp.int32).at[:N_CLASSES].set(tile_start * ROW_TILE)
    nused = tile_end[-1]
    step = jnp.arange(n_steps + 2, dtype=jnp.int32)
    tile_cls = jnp.sum(tile_end[None, :] <= jnp.minimum(step, nused - 1)[:, None], axis=1)
    tile_cls = jnp.minimum(tile_cls, N_CLASSES - 1).astype(jnp.int32)
    lo_tab, hi_tab = _pair_tables()
    te1 = jnp.asarray(lo_tab)[tile_cls]
    te2 = jnp.asarray(hi_tab)[tile_cls]
    in_cls = (step - tile_start[tile_cls]) * ROW_TILE
    tile_cnt = jnp.where(step < nused, jnp.clip(counts[tile_cls] - in_cls, 0, ROW_TILE), 0).astype(jnp.int32)

    n_slots = -(-(n_tiles + MOE_SLOTS) * ROW_TILE // SMEM_BLOCK) * SMEM_BLOCK
    inv = _slot_tokens(off, cls.reshape(N), rank.reshape(N), n_slots, chunk)
    return _moe(te1, te2, tile_cnt, inv, hm, *expert_w, layer, n_steps)


def _swap_rope(w):
    half = QK_ROPE // 2
    return jnp.concatenate([-w[..., half:], w[..., :half]], axis=-1)


def _prep_mla_weights(w_in, w_uq, w_ukv):
    D = w_in.shape[0]
    w_kr = w_in[:, Q_LORA + KV_LORA:]
    w_ks = _swap_rope(w_kr)
    zeros = jnp.zeros((D, QK_NOPE), w_in.dtype)
    win = jnp.concatenate([w_in[:, :Q_LORA + KV_LORA], zeros, w_kr, w_kr, zeros, w_ks, w_ks], axis=1)
    uq = w_uq.reshape(Q_LORA, MLA_HEADS, QK_NOPE + QK_ROPE)
    rope = uq[..., QK_NOPE:]
    wuq = jnp.concatenate([uq[..., :QK_NOPE], rope, _swap_rope(rope)], axis=-1).reshape(Q_LORA, -1)
    ukv = w_ukv.reshape(KV_LORA, MLA_HEADS, QK_NOPE + V_HEAD)
    wkn = jnp.concatenate([ukv[..., :QK_NOPE], jnp.zeros((KV_LORA, MLA_HEADS, HEAD_W - QK_NOPE), w_ukv.dtype)],
                          axis=-1).reshape(KV_LORA, -1)
    wv = ukv[..., QK_NOPE:].reshape(KV_LORA, -1)
    return win.astype(bf16), wuq.T.astype(bf16), wkn.astype(bf16), wv.T.astype(bf16)


def _rope_inv_freq():
    inv = 1.0 / (ROPE_THETA ** (np.arange(0, QK_ROPE, 2, dtype=np.float32) / QK_ROPE))
    inv2 = np.concatenate([inv, inv]).astype(np.float32)
    return np.concatenate([np.zeros((QK_NOPE,), np.float32), inv2, inv2]).reshape(1, LANES)


def _prep_router(w_gr, b_gr, w_er, b_er):
    D = w_gr.shape[0]
    n = MOE_GROUPS + MOE_GROUPS * EXPERTS_PER_GROUP
    w = jnp.concatenate([w_gr, w_er, jnp.zeros((D, LANES - n), f32)], axis=1)
    b = jnp.concatenate([b_gr, b_er, jnp.zeros((LANES - n,), f32)]).reshape(1, LANES)
    hi, lo = _split_bf16(w)
    return jnp.concatenate([hi, lo], axis=1), b


def kernel(x, c, positions, w_mod, b_mod, ln_g, ln_b, w_in, q_norm_g, kv_norm_g, w_uq, w_ukv, w_o, w_pool,
           pool_scale, w_group_router, b_group_router, w_expert_router, b_expert_router, w_gate, w_up, w_down):
    B, S, D = x.shape
    N = B * S
    tm = min(512, S)
    tq = min(1024, S)
    tr = min(1024, N)
    chunk = min(2048, N)

    mod = _modulation(c, w_mod, b_mod)
    ln = jnp.stack([ln_g, ln_b], axis=2)
    x2 = x.reshape(N, D)

    win, wuqt, wkn, wvt = _prep_mla_weights(w_in[0], w_uq[0], w_ukv[0])
    qt, k, vt = _mla_proj(x2, positions.reshape(N, 1), mod[0], win, q_norm_g[0].reshape(1, -1),
                          kv_norm_g[0].reshape(1, -1), wuqt, wkn, wvt, jnp.asarray(_rope_inv_freq()),
                          B, S, tm, tq)
    o = _attention(qt, k, vt, tq)
    wr, br = _prep_router(w_group_router[0], b_group_router[0], w_expert_router[0], b_expert_router[0])
    x1, hm, cls = _attn_out(o.reshape(N, D), x2, mod[0], w_o[0].astype(bf16), ln[0, 0], wr, br, S, tm)
    expert_w = (jnp.concatenate([w_gate, w_up], axis=-1).astype(bf16), w_down.astype(bf16))
    y0 = _moe_layer(hm, cls, expert_w, 0, chunk, tr)

    wr, br = _prep_router(w_group_router[1], b_group_router[1], w_expert_router[1], b_expert_router[1])
    x3, hm, cls = _pool_layer(x1, y0, mod[0], ln[0, 1], mod[1], w_pool[0].astype(bf16),
                              pool_scale[0].reshape(1, D), ln[1, 0], wr, br, S, tm)
    y1 = _moe_layer(hm, cls, expert_w, 1, chunk, tr)

    out = _final_merge(x3, y1, mod[1], ln[1, 1], S, tm)
    return out.reshape(B, S, D)
```

```python
import functools

import numpy as np
import jax
import jax.numpy as jnp
from jax import lax
from jax.experimental import pallas as pl
from jax.experimental.pallas import tpu as pltpu

MLA_HEADS = 16
Q_LORA = 384
KV_LORA = 256
QK_NOPE = 64
QK_ROPE = 32
V_HEAD = 64
ROPE_THETA = 10000.0
POOL_WINDOWS = (2, 4, 8, 16)
MOE_GROUPS = 4
EXPERTS_PER_GROUP = 8
DEPTH = 2
DN_ALPHA = (2.0 * DEPTH) ** 0.25
LN_EPS = 1e-5
RMS_EPS = 1e-6
LOG2E = 1.4426950408889634

LANES = 128
SUBLANES = 8
VMEM_LIMIT = 56 * 1024 * 1024

HEAD_W = 128
PAIRS_PER_GROUP = EXPERTS_PER_GROUP * (EXPERTS_PER_GROUP - 1) // 2
N_CLASSES = MOE_GROUPS * PAIRS_PER_GROUP
ROW_TILE = 128
HM_ROWS = 16
Y_ROWS = 8
MOE_SLOTS = 3
SMEM_BLOCK = 1024
HALO = 16

f32 = jnp.float32
bf16 = jnp.bfloat16


def _cparams(sem):
    return pltpu.CompilerParams(dimension_semantics=sem, vmem_limit_bytes=VMEM_LIMIT)


def _split_bf16(a):
    hi = a.astype(bf16)
    lo = (a - hi.astype(f32)).astype(bf16)
    return hi, lo


def _dot(a, b):
    return jnp.dot(a, b, preferred_element_type=f32)


def _dot3(a_hi, a_lo, b_hi, b_lo):
    return _dot(a_hi, b_hi) + (_dot(a_lo, b_hi) + _dot(a_hi, b_lo))


def _rows_load(ref, n, pitch, width):
    return jnp.concatenate([ref[pl.ds(s, n, stride=pitch), :] for s in range(width // LANES)], axis=1)


def _rows_store(ref, val, pitch):
    n, width = val.shape
    for s in range(width // LANES):
        ref[pl.ds(s, n, stride=pitch), :] = val[:, s * LANES:(s + 1) * LANES]


def _layer_norm(x, g, b):
    mu = jnp.mean(x, axis=-1, keepdims=True)
    xc = x - mu
    var = jnp.mean(xc * xc, axis=-1, keepdims=True)
    return xc * lax.rsqrt(var + LN_EPS) * g + b


def _mod_kernel(c_ref, w_ref, b_ref, o_ref):
    c = c_ref[...]
    ca = c * (1.0 / (1.0 + jnp.exp(-c)))
    a_hi, a_lo = _split_bf16(ca)
    w_hi, w_lo = _split_bf16(w_ref[...])
    o_ref[...] = _dot3(a_hi, a_lo, w_hi, w_lo) + b_ref[...]


def _modulation(c, w_mod, b_mod):
    B, D = c.shape
    depth = w_mod.shape[0]
    out = pl.pallas_call(
        _mod_kernel,
        grid=(depth, 6),
        in_specs=[
            pl.BlockSpec((B, D), lambda i, j: (0, 0)),
            pl.BlockSpec((None, D, D), lambda i, j: (i, 0, j)),
            pl.BlockSpec((None, None, 1, D), lambda i, j: (i, j, 0, 0)),
        ],
        out_specs=pl.BlockSpec((None, None, B, D), lambda i, j: (i, j, 0, 0)),
        out_shape=jax.ShapeDtypeStruct((depth, 6, B, D), f32),
        compiler_params=_cparams(("arbitrary", "arbitrary")),
        name="adaln_mod",
    )(c, w_mod, b_mod.reshape(depth, 6, 1, D))
    out = jnp.transpose(out, (0, 2, 1, 3))
    return jnp.pad(out, ((0, 0), (0, 0), (0, 2), (0, 0)))


def _mla_proj_kernel(x_ref, pos_ref, mod_ref, win_ref, qg_ref, kg_ref, wuqt_ref, wkn_ref, wvt_ref,
                     invf_ref, qt_ref, k_ref, vt_ref):
    x = x_ref[...]
    h = x * (1.0 + mod_ref[1:2, :]) + mod_ref[0:1, :]
    z = _dot(h.astype(bf16), win_ref[...])
    zq = z[:, :Q_LORA]
    zkv = z[:, Q_LORA:Q_LORA + KV_LORA]
    za = z[:, Q_LORA + KV_LORA:Q_LORA + KV_LORA + LANES]
    zb = z[:, Q_LORA + KV_LORA + LANES:]
    q_lat = zq * lax.rsqrt(jnp.mean(zq * zq, axis=-1, keepdims=True) + RMS_EPS) * qg_ref[...]
    kv_lat = zkv * lax.rsqrt(jnp.mean(zkv * zkv, axis=-1, keepdims=True) + RMS_EPS) * kg_ref[...]

    ang = pos_ref[...].astype(f32) * invf_ref[...]
    cs = jnp.cos(ang)
    sn = jnp.sin(ang)
    lane = lax.broadcasted_iota(jnp.int32, cs.shape, 1)
    scale = (QK_NOPE + QK_ROPE) ** -0.5 * LOG2E
    tqn = scale * jnp.where(lane < QK_NOPE + QK_ROPE, cs, sn)
    kr = za * cs + zb * sn

    kn = _dot(kv_lat.astype(bf16), wkn_ref[...])
    for hd in range(MLA_HEADS):
        k_ref[hd] = (kn[:, hd * HEAD_W:(hd + 1) * HEAD_W] + kr).astype(bf16)

    tqt = tqn.T
    qt = _dot(wuqt_ref[...], q_lat.T.astype(bf16))
    for hd in range(MLA_HEADS):
        qt_ref[hd] = (qt[hd * HEAD_W:(hd + 1) * HEAD_W, :] * tqt).astype(bf16)
    vt_ref[...] = _dot(wvt_ref[...], kv_lat.T.astype(bf16)).astype(bf16)


def _mla_proj(x2, pos2, mod, win, qg, kg, wuqt, wkn, wvt, invf, B, S, tm, tq):
    N, D = x2.shape
    spb = S // tm
    per_q = tq // tm
    const = lambda shape: pl.BlockSpec(shape, lambda i: (0,) * len(shape))
    nq = S // tq
    return pl.pallas_call(
        _mla_proj_kernel,
        grid=(N // tm,),
        in_specs=[
            pl.BlockSpec((tm, D), lambda i: (i, 0)),
            pl.BlockSpec((tm, 1), lambda i: (i, 0)),
            pl.BlockSpec((None, 8, D), lambda i: (i // spb, 0, 0)),
            const(win.shape), const(qg.shape), const(kg.shape), const(wuqt.shape), const(wkn.shape),
            const(wvt.shape), const(invf.shape),
        ],
        out_specs=[
            pl.BlockSpec((None, MLA_HEADS, None, HEAD_W, tm),
                         lambda i: (i // spb, 0, (i % spb) // per_q, 0, i % per_q)),
            pl.BlockSpec((None, MLA_HEADS, tm, HEAD_W), lambda i: (i // spb, 0, i % spb, 0)),
            pl.BlockSpec((None, None, MLA_HEADS * V_HEAD, tm),
                         lambda i: (i // spb, (i % spb) // per_q, 0, i % per_q)),
        ],
        out_shape=[
            jax.ShapeDtypeStruct((B, MLA_HEADS, nq, HEAD_W, tq), bf16),
            jax.ShapeDtypeStruct((B, MLA_HEADS, S, HEAD_W), bf16),
            jax.ShapeDtypeStruct((B, nq, MLA_HEADS * V_HEAD, tq), bf16),
        ],
        compiler_params=_cparams(("arbitrary",)),
        name="mla_proj",
    )(x2, pos2, mod, win, qg, kg, wuqt, wkn, wvt, invf)


ONES_ROWS = 16


def _attn_kernel(qt_ref, k_ref, vt_ref, o_ref, vx_ref, m_ref, acc_ref, ot_ref, *, tq):
    nq = qt_ref.shape[1]
    for hh in range(2):
        for j in range(nq):
            vx_ref[hh, j, :V_HEAD, :] = vt_ref[j, hh * V_HEAD:(hh + 1) * V_HEAD, :]
            vx_ref[hh, j, V_HEAD:, :] = jnp.ones((ONES_ROWS, tq), bf16)

    def kv_step(qi, j):
        scores = [_dot(k_ref[hh, pl.ds(j * tq, tq), :], qt_ref[hh, qi]) for hh in range(2)]
        for hh in range(2):
            update(hh, 0, tq, [(scores[hh], vx_ref[hh, j])])

    def update(hh, lo, hi, parts):
        m_old = m_ref[hh, :, lo:hi]
        m_new = m_old
        for s, _ in parts:
            m_new = jnp.maximum(m_new, jnp.max(s, axis=0, keepdims=True))
        alpha = jnp.exp2(m_old - m_new)
        pv = sum(_dot(vx, jnp.exp2(s - m_new[0:1, :]).astype(bf16)) for s, vx in parts)
        acc_ref[hh, :, lo:hi] = acc_ref[hh, :, lo:hi] * alpha[0:1, :] + pv
        m_ref[hh, :, lo:hi] = m_new

    def diag_step(qi):
        half = tq // 2
        k0 = qi * tq
        tops = [_dot(k_ref[hh, pl.ds(k0, half), :], qt_ref[hh, qi]) for hh in range(2)]
        bots = [_dot(k_ref[hh, pl.ds(k0 + half, half), :], qt_ref[hh, qi, :, half:]) for hh in range(2)]
        key = lax.broadcasted_iota(jnp.int32, (half, half), 0)
        qry = lax.broadcasted_iota(jnp.int32, (half, half), 1)
        visible = key <= qry
        for hh in range(2):
            vx = vx_ref[hh, qi]
            early = jnp.where(visible, tops[hh][:, :half], -jnp.inf)
            late = jnp.where(visible, bots[hh], -jnp.inf)
            update(hh, 0, half, [(early, vx[:, :half])])
            update(hh, half, tq, [(tops[hh][:, half:], vx[:, :half]), (late, vx[:, half:])])

    def q_body(qi, carry):
        m_ref[...] = jnp.full(m_ref.shape, -jnp.inf, f32)
        acc_ref[...] = jnp.zeros(acc_ref.shape, f32)

        for j in range(qi):
            kv_step(qi, j)
        diag_step(qi)
        for hh in range(2):
            acc = acc_ref[hh]
            ot_ref[hh * V_HEAD:(hh + 1) * V_HEAD, :] = acc[:V_HEAD, :] * (1.0 / acc[V_HEAD:V_HEAD + 1, :])
        o_ref[pl.ds(qi * tq, tq), :] = ot_ref[...].T.astype(bf16)
        return carry

    for qi in range(nq):
        q_body(qi, 0)


def _attention(qt, k, vt, tq):
    B, H, nq, W, _ = qt.shape
    S = nq * tq
    return pl.pallas_call(
        functools.partial(_attn_kernel, tq=tq),
        grid=(B, H // 2),
        in_specs=[
            pl.BlockSpec((None, 2, nq, W, tq), lambda b, p: (b, p, 0, 0, 0)),
            pl.BlockSpec((None, 2, S, W), lambda b, p: (b, p, 0, 0)),
            pl.BlockSpec((None, nq, 2 * V_HEAD, tq), lambda b, p: (b, 0, p, 0)),
        ],
        out_specs=pl.BlockSpec((None, S, LANES), lambda b, p: (b, 0, p)),
        out_shape=jax.ShapeDtypeStruct((B, S, H * V_HEAD), bf16),
        scratch_shapes=[
            pltpu.VMEM((2, nq, V_HEAD + ONES_ROWS, tq), bf16),
            pltpu.VMEM((2, SUBLANES, tq), f32),
            pltpu.VMEM((2, V_HEAD + ONES_ROWS, tq), f32),
            pltpu.VMEM((2 * V_HEAD, tq), f32),
        ],
        compiler_params=_cparams(("arbitrary", "arbitrary")),
        name="mla_attention",
    )(qt, k, vt)


def _post_mixer(x, y, mod_ref, ln_ref, wr_ref, br_ref, xo_ref, hm_ref, cls_ref):
    tm, D = x.shape
    x1 = _layer_norm(DN_ALPHA * x + (1.0 + mod_ref[2:3, :]) * y, ln_ref[0:1, :], ln_ref[1:2, :])
    xo_ref[...] = x1
    h2 = x1 * (1.0 + mod_ref[4:5, :]) + mod_ref[3:4, :]
    h_hi, h_lo = _split_bf16(h2)
    wr = wr_ref[...]
    t_hi = _dot(h_hi, wr)
    t_lo = _dot(h_lo, wr)
    logits = t_hi[:, :LANES] + (t_hi[:, LANES:] + t_lo[:, :LANES]) + br_ref[...]

    lane = lax.broadcasted_iota(jnp.int32, logits.shape, 1).astype(f32)
    neg = -jnp.inf
    far = float(LANES)
    is_g = lane < MOE_GROUPS
    gl = jnp.where(is_g, logits, neg)
    gmax = jnp.max(gl, axis=-1, keepdims=True)
    gidx = jnp.min(jnp.where(gl == gmax, lane, far), axis=-1, keepdims=True)
    g_p = 1.0 / jnp.sum(jnp.where(is_g, jnp.exp(logits - gmax), 0.0), axis=-1, keepdims=True)
    base = MOE_GROUPS + EXPERTS_PER_GROUP * gidx
    el = jnp.where((lane >= base) & (lane < base + EXPERTS_PER_GROUP), logits, neg)
    t1 = jnp.max(el, axis=-1, keepdims=True)
    i1 = jnp.min(jnp.where(el == t1, lane, far), axis=-1, keepdims=True)
    el2 = jnp.where(lane == i1, neg, el)
    t2 = jnp.max(el2, axis=-1, keepdims=True)
    i2 = jnp.min(jnp.where(el2 == t2, lane, far), axis=-1, keepdims=True)
    e = jnp.exp(t2 - t1)
    w1 = g_p / (1.0 + e)
    w2 = g_p * e / (1.0 + e)
    a = i1 - base
    b = i2 - base
    a_first = a < b
    lo = jnp.where(a_first, a, b)
    hi = jnp.where(a_first, b, a)
    w_lo = jnp.where(a_first, w1, w2)
    w_hi = jnp.where(a_first, w2, w1)
    pair = lo * (2 * EXPERTS_PER_GROUP - 1 - lo) * 0.5 + (hi - lo - 1.0)
    cls_ref[...] = (gidx * PAIRS_PER_GROUP + pair).astype(jnp.int32)

    _rows_store(hm_ref, h2, HM_ROWS)
    hm_ref.reshape(tm, HM_ROWS, LANES)[:, D // LANES:, :] = jnp.zeros((tm, HM_ROWS - D // LANES, LANES), f32)
    hm_ref[pl.ds(D // LANES, tm, stride=HM_ROWS), :] = jnp.broadcast_to(w_lo, (tm, LANES))
    hm_ref[pl.ds(D // LANES + 1, tm, stride=HM_ROWS), :] = jnp.broadcast_to(w_hi, (tm, LANES))


def _attn_out_kernel(o_ref, x_ref, mod_ref, wo_ref, ln_ref, wr_ref, br_ref,
                     xo_ref, hm_ref, cls_ref):
    y = _dot(o_ref[...], wo_ref[...])
    _post_mixer(x_ref[...], y, mod_ref, ln_ref, wr_ref, br_ref, xo_ref, hm_ref, cls_ref)


def _post_out_specs(N, D, tm):
    specs = [
        pl.BlockSpec((tm, D), lambda i: (i, 0)),
        pl.BlockSpec((tm * HM_ROWS, LANES), lambda i: (i, 0)),
        pl.BlockSpec((tm, 1), lambda i: (i, 0)),
    ]
    shapes = [
        jax.ShapeDtypeStruct((N, D), f32),
        jax.ShapeDtypeStruct((N * HM_ROWS, LANES), f32),
        jax.ShapeDtypeStruct((N, 1), jnp.int32),
    ]
    return specs, shapes


def _attn_out(o2, x2, mod, wo, ln, wr, br, S, tm):
    N, D = x2.shape
    spb = S // tm
    const = lambda shape: pl.BlockSpec(shape, lambda i: (0,) * len(shape))
    out_specs, out_shapes = _post_out_specs(N, D, tm)
    return pl.pallas_call(
        _attn_out_kernel,
        grid=(N // tm,),
        in_specs=[
            pl.BlockSpec((tm, D), lambda i: (i, 0)),
            pl.BlockSpec((tm, D), lambda i: (i, 0)),
            pl.BlockSpec((None, 8, D), lambda i: (i // spb, 0, 0)),
            const(wo.shape), const(ln.shape), const(wr.shape), const(br.shape),
        ],
        out_specs=out_specs,
        out_shape=out_shapes,
        compiler_params=_cparams(("arbitrary",)),
        name="attn_out_router",
    )(o2, x2, mod, wo, ln, wr, br)


def _pool_kernel(x_ref, y_ref, modp_ref, lnp_ref, mod_ref, wp_ref, ps_ref, ln_ref,
                 wr_ref, br_ref, xo_ref, hm_ref, cls_ref, hb_ref, *, spb):
    tm, D = x_ref.shape
    i = pl.program_id(0)
    t_blk = i % spb
    yprev = _rows_load(y_ref, tm, Y_ROWS, D)
    x2 = _layer_norm(DN_ALPHA * x_ref[...] + (1.0 + modp_ref[5:6, :]) * yprev,
                     lnp_ref[0:1, :], lnp_ref[1:2, :])
    h = x2 * (1.0 + mod_ref[1:2, :]) + mod_ref[0:1, :]

    @pl.when(t_blk == 0)
    def _():
        hb_ref[0:HALO, :] = jnp.zeros((HALO, D), f32)

    hb_ref[HALO:, :] = h
    t_seq = t_blk * tm + lax.broadcasted_iota(jnp.int32, (tm, 1), 0)
    gd = D // len(POOL_WINDOWS)
    ys = []
    for gi, w in enumerate(POOL_WINDOWS):
        c0, c1 = gi * gd, (gi + 1) * gd
        a = hb_ref[:, c0:c1]
        span = 1
        while span < w:
            a = a[span:, :] + a[:-span, :]
            span *= 2
        tsum = a[HALO - (w - 1):, :]
        cnt = jnp.minimum(t_seq + 1, w).astype(f32)
        mixed = tsum / cnt - h[:, c0:c1]
        ys.append(_dot(mixed.astype(bf16), wp_ref[gi]))
    hb_ref[0:HALO, :] = h[tm - HALO:, :]
    y = jnp.concatenate(ys, axis=1) * ps_ref[...]
    _post_mixer(x2, y, mod_ref, ln_ref, wr_ref, br_ref, xo_ref, hm_ref, cls_ref)


def _pool_layer(x2, yprev, modp, lnp, mod, wp, ps, ln, wr, br, S, tm):
    N, D = x2.shape
    spb = S // tm
    const = lambda shape: pl.BlockSpec(shape, lambda i: (0,) * len(shape))
    out_specs, out_shapes = _post_out_specs(N, D, tm)
    return pl.pallas_call(
        functools.partial(_pool_kernel, spb=spb),
        grid=(N // tm,),
        in_specs=[
            pl.BlockSpec((tm, D), lambda i: (i, 0)),
            pl.BlockSpec((tm * Y_ROWS, LANES), lambda i: (i, 0)),
            pl.BlockSpec((None, 8, D), lambda i: (i // spb, 0, 0)),
            const(lnp.shape),
            pl.BlockSpec((None, 8, D), lambda i: (i // spb, 0, 0)),
            const(wp.shape), const(ps.shape), const(ln.shape),
            const(wr.shape), const(br.shape),
        ],
        out_specs=out_specs,
        out_shape=out_shapes,
        scratch_shapes=[pltpu.VMEM((HALO + tm, D), f32)],
        compiler_params=_cparams(("arbitrary",)),
        name="pool_mixer_router",
    )(x2, yprev, modp, lnp, mod, wp, ps, ln, wr, br)


def _final_kernel(x_ref, y_ref, mod_ref, ln_ref, o_ref):
    tm, D = x_ref.shape
    y = _rows_load(y_ref, tm, Y_ROWS, D)
    o_ref[...] = _layer_norm(DN_ALPHA * x_ref[...] + (1.0 + mod_ref[5:6, :]) * y,
                             ln_ref[0:1, :], ln_ref[1:2, :])


def _final_merge(x2, y, mod, ln, S, tm):
    N, D = x2.shape
    spb = S // tm
    return pl.pallas_call(
        _final_kernel,
        grid=(N // tm,),
        in_specs=[
            pl.BlockSpec((tm, D), lambda i: (i, 0)),
            pl.BlockSpec((tm * Y_ROWS, LANES), lambda i: (i, 0)),
            pl.BlockSpec((None, 8, D), lambda i: (i // spb, 0, 0)),
            pl.BlockSpec(ln.shape, lambda i: (0, 0)),
        ],
        out_specs=pl.BlockSpec((tm, D), lambda i: (i, 0)),
        out_shape=jax.ShapeDtypeStruct((N, D), f32),
        compiler_params=_cparams(("arbitrary",)),
        name="final_merge",
    )(x2, y, mod, ln)


def _rank_kernel(cls_ref, rank_ref, cnt_ref, carry_ref, earlier_ref):
    tr = cls_ref.shape[0]

    @pl.when(pl.program_id(0) == 0)
    def _():
        carry_ref[...] = jnp.zeros(carry_ref.shape, f32)
        r = lax.broadcasted_iota(jnp.int32, (tr, tr), 0)
        c = lax.broadcasted_iota(jnp.int32, (tr, tr), 1)
        earlier_ref[...] = (c < r).astype(bf16)

    lane = lax.broadcasted_iota(jnp.int32, (tr, LANES), 1)
    onehot = (cls_ref[...] == lane)
    oh = onehot.astype(bf16)
    before = _dot(earlier_ref[...], oh) + carry_ref[0:1, :]
    rank_ref[...] = jnp.sum(jnp.where(onehot, before, 0.0), axis=-1, keepdims=True).astype(jnp.int32)
    total = carry_ref[0:1, :] + jnp.sum(oh.astype(f32), axis=0, keepdims=True)
    carry_ref[...] = jnp.broadcast_to(total, carry_ref.shape)
    cnt_ref[...] = jnp.broadcast_to(total, cnt_ref.shape).astype(jnp.int32)


def _class_ranks(cls, tr):
    N = cls.shape[0]
    return pl.pallas_call(
        _rank_kernel,
        grid=(N // tr,),
        in_specs=[pl.BlockSpec((tr, 1), lambda i: (i, 0))],
        out_specs=[
            pl.BlockSpec((tr, 1), lambda i: (i, 0)),
            pl.BlockSpec((SUBLANES, LANES), lambda i: (0, 0)),
        ],
        out_shape=[
            jax.ShapeDtypeStruct((N, 1), jnp.int32),
            jax.ShapeDtypeStruct((SUBLANES, LANES), jnp.int32),
        ],
        scratch_shapes=[pltpu.VMEM((SUBLANES, LANES), f32), pltpu.VMEM((tr, tr), bf16)],
        compiler_params=_cparams(("arbitrary",)),
        name="class_ranks",
    )(cls)


def _slot_token_kernel(off_ref, cls_ref, rank_ref, inv_ref, *, chunk):
    base = pl.program_id(0) * chunk

    @pl.when(pl.program_id(0) == 0)
    def _():
        def clear(s, c):
            inv_ref[s] = 0
            return c
        lax.fori_loop(0, inv_ref.shape[0], clear, 0, unroll=32)

    def body(t, c):
        inv_ref[off_ref[cls_ref[t]] + rank_ref[t]] = base + t
        return c

    lax.fori_loop(0, chunk, body, 0, unroll=16)


def _slot_tokens(off, cls1, rank1, n_slots, chunk):
    N = cls1.shape[0]
    return pl.pallas_call(
        functools.partial(_slot_token_kernel, chunk=chunk),
        grid_spec=pltpu.PrefetchScalarGridSpec(
            num_scalar_prefetch=1,
            grid=(N // chunk,),
            in_specs=[
                pl.BlockSpec((chunk,), lambda i, off: (i,), memory_space=pltpu.SMEM),
                pl.BlockSpec((chunk,), lambda i, off: (i,), memory_space=pltpu.SMEM),
            ],
            out_specs=pl.BlockSpec((n_slots,), lambda i, off: (0,), memory_space=pltpu.SMEM),
        ),
        out_shape=jax.ShapeDtypeStruct((n_slots,), jnp.int32),
        compiler_params=_cparams(("arbitrary",)),
        name="slot_tokens",
    )(off, cls1, rank1)


def _moe_kernel(te1_ref, te2_ref, cnt_ref, inv_ref, hm_ref, wgu1_ref, wd1_ref, wgu2_ref, wd2_ref,
                y_ref, *scratch):
    hbufs = scratch[:MOE_SLOTS]
    ybufs = scratch[MOE_SLOTS:2 * MOE_SLOTS]
    gsem, ssem = scratch[2 * MOE_SLOTS:]
    ff, D = wd1_ref.shape
    i = pl.program_id(0)

    def count(tile):
        return jnp.where(tile >= 0, cnt_ref[jnp.maximum(tile, 0)], 0)

    n_cur, n_prev, n_ahead, n_old = count(i), count(i - 1), count(i + 2), count(i - MOE_SLOTS)

    def token_rows(ref, t, pitch):
        return ref.at[pl.ds(pl.multiple_of(t * pitch, pitch), pitch)]

    def start_gather(tok, r, hb, sem):
        pltpu.make_async_copy(token_rows(hm_ref, tok, HM_ROWS), token_rows(hb, r, HM_ROWS), sem).start()

    def start_scatter(tok, r, yb, sem):
        pltpu.make_async_copy(token_rows(yb, r, Y_ROWS), token_rows(y_ref, tok, Y_ROWS), sem).start()

    def wait_tokens(src, dst, sem, n, pitch):
        @pl.when(n > 0)
        def _():
            rows = pl.multiple_of(n * pitch, pitch)
            pltpu.make_async_copy(src.at[pl.ds(0, rows)], dst.at[pl.ds(0, rows)], sem).wait()

    @pl.when(i == 0)
    def _():
        for t in range(MOE_SLOTS - 1):
            def first(r, c):
                start_gather(inv_ref[t * ROW_TILE + r], r, hbufs[t], gsem.at[t])
                return c
            lax.fori_loop(0, cnt_ref[t], first, 0)

    def step(cur):
        far = (cur + MOE_SLOTS - 1) % MOE_SLOTS
        hb, yb = hbufs[cur], ybufs[cur]
        wait_tokens(hm_ref, hb, gsem.at[cur], n_cur, HM_ROWS)
        wait_tokens(yb, y_ref, ssem.at[cur], n_old, Y_ROWS)

        @pl.when((n_cur > 0) | (n_prev > 0))
        def _():
            ahead = (i + 2) * ROW_TILE
            prev = jnp.maximum(i - 1, 0) * ROW_TILE
            for r in range(ROW_TILE):
                tok_ahead = inv_ref[ahead + r]
                tok_prev = inv_ref[prev + r]

                @pl.when(r < n_ahead)
                def _():
                    start_gather(tok_ahead, r, hbufs[far], gsem.at[far])

                @pl.when(r < n_prev)
                def _():
                    start_scatter(tok_prev, r, ybufs[far], ssem.at[far])

            h = _rows_load(hb, ROW_TILE, HM_ROWS, D).astype(bf16)

            def hidden(gu, gate_row):
                w = hb[pl.ds(gate_row, ROW_TILE, stride=HM_ROWS), :]
                g, u = gu[:, :ff], gu[:, ff:]
                he = g * (1.0 / (1.0 + jnp.exp(-g))) * u
                return (he * jnp.concatenate([w] * (ff // LANES), axis=1)).astype(bf16)

            gu1 = _dot(h, wgu1_ref[...])
            gu2 = _dot(h, wgu2_ref[...])
            y = _dot(hidden(gu1, D // LANES), wd1_ref[...]) + _dot(hidden(gu2, D // LANES + 1), wd2_ref[...])
            _rows_store(yb, y, Y_ROWS)

    for cur in range(MOE_SLOTS):
        @pl.when(i % MOE_SLOTS == cur)
        def _():
            step(cur)


def _moe(te1, te2, cnt, inv, hm, wgu, wd, layer, n_steps):
    N = hm.shape[0] // HM_ROWS
    _, E, D, ff2 = wgu.shape
    ff = ff2 // 2

    def wmap(which):
        def f(i, te1, te2, cnt, inv):
            return (layer, (te1, te2)[which][i], 0, 0)
        return f

    return pl.pallas_call(
        _moe_kernel,
        grid_spec=pltpu.PrefetchScalarGridSpec(
            num_scalar_prefetch=4,
            grid=(n_steps,),
            in_specs=[
                pl.BlockSpec(memory_space=pl.ANY),
                pl.BlockSpec((None, None, D, ff2), wmap(0)),
                pl.BlockSpec((None, None, ff, D), wmap(0)),
                pl.BlockSpec((None, None, D, ff2), wmap(1)),
                pl.BlockSpec((None, None, ff, D), wmap(1)),
            ],
            out_specs=pl.BlockSpec(memory_space=pl.ANY),
            scratch_shapes=(
                [pltpu.VMEM((ROW_TILE * HM_ROWS, LANES), f32)] * MOE_SLOTS
                + [pltpu.VMEM((ROW_TILE * Y_ROWS, LANES), f32)] * MOE_SLOTS
                + [pltpu.SemaphoreType.DMA((MOE_SLOTS,)), pltpu.SemaphoreType.DMA((MOE_SLOTS,))]
            ),
        ),
        out_shape=jax.ShapeDtypeStruct((N * Y_ROWS, LANES), f32),
        compiler_params=_cparams(("arbitrary",)),
        name="pair_moe",
    )(te1, te2, cnt, inv, hm, wgu, wd, wgu, wd)


def _pair_tables():
    lo = np.zeros((N_CLASSES,), np.int32)
    hi = np.zeros((N_CLASSES,), np.int32)
    for g in range(MOE_GROUPS):
        p = 0
        for a in range(EXPERTS_PER_GROUP):
            for b in range(a + 1, EXPERTS_PER_GROUP):
                lo[g * PAIRS_PER_GROUP + p] = g * EXPERTS_PER_GROUP + a
                hi[g * PAIRS_PER_GROUP + p] = g * EXPERTS_PER_GROUP + b
                p += 1
    return lo, hi


def _moe_layer(hm, cls, expert_w, layer, chunk, tr):
    N = cls.shape[0]
    n_tiles = N // ROW_TILE + N_CLASSES
    n_steps = n_tiles + MOE_SLOTS
    rank, cnt = _class_ranks(cls, tr)
    counts = cnt[0, :N_CLASSES]
    tiles = (counts + ROW_TILE - 1) // ROW_TILE
    tile_end = jnp.cumsum(tiles)
    tile_start = tile_end - tiles
    off = jnp.zeros((LANES,), jnp.int32).at[:N_CLASSES].set(tile_start * ROW_TILE)
    nused = tile_end[-1]
    step = jnp.arange(n_steps + 2, dtype=jnp.int32)
    tile_cls = jnp.sum(tile_end[None, :] <= jnp.minimum(step, nused - 1)[:, None], axis=1)
    tile_cls = jnp.minimum(tile_cls, N_CLASSES - 1).astype(jnp.int32)
    lo_tab, hi_tab = _pair_tables()
    te1 = jnp.asarray(lo_tab)[tile_cls]
    te2 = jnp.asarray(hi_tab)[tile_cls]
    in_cls = (step - tile_start[tile_cls]) * ROW_TILE
    tile_cnt = jnp.where(step < nused, jnp.clip(counts[tile_cls] - in_cls, 0, ROW_TILE), 0).astype(jnp.int32)

    n_slots = -(-(n_tiles + MOE_SLOTS) * ROW_TILE // SMEM_BLOCK) * SMEM_BLOCK
    inv = _slot_tokens(off, cls.reshape(N), rank.reshape(N), n_slots, chunk)
    return _moe(te1, te2, tile_cnt, inv, hm, *expert_w, layer, n_steps)


def _swap_rope(w):
    half = QK_ROPE // 2
    return jnp.concatenate([-w[..., half:], w[..., :half]], axis=-1)


def _prep_mla_weights(w_in, w_uq, w_ukv):
    D = w_in.shape[0]
    w_kr = w_in[:, Q_LORA + KV_LORA:]
    w_ks = _swap_rope(w_kr)
    zeros = jnp.zeros((D, QK_NOPE), w_in.dtype)
    win = jnp.concatenate([w_in[:, :Q_LORA + KV_LORA], zeros, w_kr, w_kr, zeros, w_ks, w_ks], axis=1)
    uq = w_uq.reshape(Q_LORA, MLA_HEADS, QK_NOPE + QK_ROPE)
    rope = uq[..., QK_NOPE:]
    wuq = jnp.concatenate([uq[..., :QK_NOPE], rope, _swap_rope(rope)], axis=-1).reshape(Q_LORA, -1)
    ukv = w_ukv.reshape(KV_LORA, MLA_HEADS, QK_NOPE + V_HEAD)
    wkn = jnp.concatenate([ukv[..., :QK_NOPE], jnp.zeros((KV_LORA, MLA_HEADS, HEAD_W - QK_NOPE), w_ukv.dtype)],
                          axis=-1).reshape(KV_LORA, -1)
    wv = ukv[..., QK_NOPE:].reshape(KV_LORA, -1)
    return win.astype(bf16), wuq.T.astype(bf16), wkn.astype(bf16), wv.T.astype(bf16)


def _rope_inv_freq():
    inv = 1.0 / (ROPE_THETA ** (np.arange(0, QK_ROPE, 2, dtype=np.float32) / QK_ROPE))
    inv2 = np.concatenate([inv, inv]).astype(np.float32)
    return np.concatenate([np.zeros((QK_NOPE,), np.float32), inv2, inv2]).reshape(1, LANES)


def _prep_router(w_gr, b_gr, w_er, b_er):
    D = w_gr.shape[0]
    n = MOE_GROUPS + MOE_GROUPS * EXPERTS_PER_GROUP
    w = jnp.concatenate([w_gr, w_er, jnp.zeros((D, LANES - n), f32)], axis=1)
    b = jnp.concatenate([b_gr, b_er, jnp.zeros((LANES - n,), f32)]).reshape(1, LANES)
    hi, lo = _split_bf16(w)
    return jnp.concatenate([hi, lo], axis=1), b


def kernel(x, c, positions, w_mod, b_mod, ln_g, ln_b, w_in, q_norm_g, kv_norm_g, w_uq, w_ukv, w_o, w_pool,
           pool_scale, w_group_router, b_group_router, w_expert_router, b_expert_router, w_gate, w_up, w_down):
    B, S, D = x.shape
    N = B * S
    tm = min(512, S)
    tq = min(1024, S)
    tr = min(1024, N)
    chunk = min(2048, N)

    mod = _modulation(c, w_mod, b_mod)
    ln = jnp.stack([ln_g, ln_b], axis=2)
    x2 = x.reshape(N, D)

    win, wuqt, wkn, wvt = _prep_mla_weights(w_in[0], w_uq[0], w_ukv[0])
    qt, k, vt = _mla_proj(x2, positions.reshape(N, 1), mod[0], win, q_norm_g[0].reshape(1, -1),
                          kv_norm_g[0].reshape(1, -1), wuqt, wkn, wvt, jnp.asarray(_rope_inv_freq()),
                          B, S, tm, tq)
    o = _attention(qt, k, vt, tq)
    wr, br = _prep_router(w_group_router[0], b_group_router[0], w_expert_router[0], b_expert_router[0])
    x1, hm, cls = _attn_out(o.reshape(N, D), x2, mod[0], w_o[0].astype(bf16), ln[0, 0], wr, br, S, tm)
    expert_w = (jnp.concatenate([w_gate, w_up], axis=-1).astype(bf16), w_down.astype(bf16))
    y0 = _moe_layer(hm, cls, expert_w, 0, chunk, tr)

    wr, br = _prep_router(w_group_router[1], b_group_router[1], w_expert_router[1], b_expert_router[1])
    x3, hm, cls = _pool_layer(x1, y0, mod[0], ln[0, 1], mod[1], w_pool[0].astype(bf16),
                              pool_scale[0].reshape(1, D), ln[1, 0], wr, br, S, tm)
    y1 = _moe_layer(hm, cls, expert_w, 1, chunk, tr)

    out = _final_merge(x3, y1, mod[1], ln[1, 1], S, tm)
    return out.reshape(B, S, D)
```

```python
import functools

import numpy as np
import jax
import jax.numpy as jnp
from jax import lax
from jax.experimental import pallas as pl
from jax.experimental.pallas import tpu as pltpu

MLA_HEADS = 16
Q_LORA = 384
KV_LORA = 256
QK_NOPE = 64
QK_ROPE = 32
V_HEAD = 64
ROPE_THETA = 10000.0
POOL_WINDOWS = (2, 4, 8, 16)
MOE_GROUPS = 4
EXPERTS_PER_GROUP = 8
DEPTH = 2
DN_ALPHA = (2.0 * DEPTH) ** 0.25
LN_EPS = 1e-5
RMS_EPS = 1e-6
LOG2E = 1.4426950408889634

LANES = 128
SUBLANES = 8
VMEM_LIMIT = 56 * 1024 * 1024

HEAD_W = 128
PAIRS_PER_GROUP = EXPERTS_PER_GROUP * (EXPERTS_PER_GROUP - 1) // 2
N_CLASSES = MOE_GROUPS * PAIRS_PER_GROUP
ROW_TILE = 128
HM_ROWS = 16
Y_ROWS = 8
MOE_SLOTS = 3
SMEM_BLOCK = 1024
HALO = 16

f32 = jnp.float32
bf16 = jnp.bfloat16


def _cparams(sem):
    return pltpu.CompilerParams(dimension_semantics=sem, vmem_limit_bytes=VMEM_LIMIT)


def _split_bf16(a):
    hi = a.astype(bf16)
    lo = (a - hi.astype(f32)).astype(bf16)
    return hi, lo


def _dot(a, b):
    return jnp.dot(a, b, preferred_element_type=f32)


def _dot3(a_hi, a_lo, b_hi, b_lo):
    return _dot(a_hi, b_hi) + (_dot(a_lo, b_hi) + _dot(a_hi, b_lo))


def _rows_load(ref, n, pitch, width):
    return jnp.concatenate([ref[pl.ds(s, n, stride=pitch), :] for s in range(width // LANES)], axis=1)


def _rows_store(ref, val, pitch):
    n, width = val.shape
    for s in range(width // LANES):
        ref[pl.ds(s, n, stride=pitch), :] = val[:, s * LANES:(s + 1) * LANES]


def _layer_norm(x, g, b):
    mu = jnp.mean(x, axis=-1, keepdims=True)
    xc = x - mu
    var = jnp.mean(xc * xc, axis=-1, keepdims=True)
    return xc * lax.rsqrt(var + LN_EPS) * g + b


def _mod_kernel(c_ref, w_ref, b_ref, o_ref):
    c = c_ref[...]
    ca = c * (1.0 / (1.0 + jnp.exp(-c)))
    a_hi, a_lo = _split_bf16(ca)
    w_hi, w_lo = _split_bf16(w_ref[...])
    o_ref[...] = _dot3(a_hi, a_lo, w_hi, w_lo) + b_ref[...]


def _modulation(c, w_mod, b_mod):
    B, D = c.shape
    depth = w_mod.shape[0]
    out = pl.pallas_call(
        _mod_kernel,
        grid=(depth, 6),
        in_specs=[
            pl.BlockSpec((B, D), lambda i, j: (0, 0)),
            pl.BlockSpec((None, D, D), lambda i, j: (i, 0, j)),
            pl.BlockSpec((None, None, 1, D), lambda i, j: (i, j, 0, 0)),
        ],
        out_specs=pl.BlockSpec((None, None, B, D), lambda i, j: (i, j, 0, 0)),
        out_shape=jax.ShapeDtypeStruct((depth, 6, B, D), f32),
        compiler_params=_cparams(("arbitrary", "arbitrary")),
        name="adaln_mod",
    )(c, w_mod, b_mod.reshape(depth, 6, 1, D))
    out = jnp.transpose(out, (0, 2, 1, 3))
    return jnp.pad(out, ((0, 0), (0, 0), (0, 2), (0, 0)))


def _mla_proj_kernel(x_ref, pos_ref, mod_ref, win_ref, qg_ref, kg_ref, wuqt_ref, wkn_ref, wvt_ref,
                     invf_ref, qt_ref, k_ref, vt_ref):
    x = x_ref[...]
    h = x * (1.0 + mod_ref[1:2, :]) + mod_ref[0:1, :]
    z = _dot(h.astype(bf16), win_ref[...])
    zq = z[:, :Q_LORA]
    zkv = z[:, Q_LORA:Q_LORA + KV_LORA]
    za = z[:, Q_LORA + KV_LORA:Q_LORA + KV_LORA + LANES]
    zb = z[:, Q_LORA + KV_LORA + LANES:]
    q_lat = zq * lax.rsqrt(jnp.mean(zq * zq, axis=-1, keepdims=True) + RMS_EPS) * qg_ref[...]
    kv_lat = zkv * lax.rsqrt(jnp.mean(zkv * zkv, axis=-1, keepdims=True) + RMS_EPS) * kg_ref[...]

    ang = pos_ref[...].astype(f32) * invf_ref[...]
    cs = jnp.cos(ang)
    sn = jnp.sin(ang)
    lane = lax.broadcasted_iota(jnp.int32, cs.shape, 1)
    scale = (QK_NOPE + QK_ROPE) ** -0.5 * LOG2E
    tqn = scale * jnp.where(lane < QK_NOPE + QK_ROPE, cs, sn)
    kr = za * cs + zb * sn

    kn = _dot(kv_lat.astype(bf16), wkn_ref[...])
    for hd in range(MLA_HEADS):
        k_ref[hd] = (kn[:, hd * HEAD_W:(hd + 1) * HEAD_W] + kr).astype(bf16)

    tqt = tqn.T
    qt = _dot(wuqt_ref[...], q_lat.T.astype(bf16))
    for hd in range(MLA_HEADS):
        qt_ref[hd] = (qt[hd * HEAD_W:(hd + 1) * HEAD_W, :] * tqt).astype(bf16)
    vt_ref[...] = _dot(wvt_ref[...], kv_lat.T.astype(bf16)).astype(bf16)


def _mla_proj(x2, pos2, mod, win, qg, kg, wuqt, wkn, wvt, invf, B, S, tm, tq):
    N, D = x2.shape
    spb = S // tm
    per_q = tq // tm
    const = lambda shape: pl.BlockSpec(shape, lambda i: (0,) * len(shape))
    nq = S // tq
    return pl.pallas_call(
        _mla_proj_kernel,
        grid=(N // tm,),
        in_specs=[
            pl.BlockSpec((tm, D), lambda i: (i, 0)),
            pl.BlockSpec((tm, 1), lambda i: (i, 0)),
            pl.BlockSpec((None, 8, D), lambda i: (i // spb, 0, 0)),
            const(win.shape), const(qg.shape), const(kg.shape), const(wuqt.shape), const(wkn.shape),
            const(wvt.shape), const(invf.shape),
        ],
        out_specs=[
            pl.BlockSpec((None, MLA_HEADS, None, HEAD_W, tm),
                         lambda i: (i // spb, 0, (i % spb) // per_q, 0, i % per_q)),
            pl.BlockSpec((None, MLA_HEADS, tm, HEAD_W), lambda i: (i // spb, 0, i % spb, 0)),
            pl.BlockSpec((None, None, MLA_HEADS * V_HEAD, tm),
                         lambda i: (i // spb, (i % spb) // per_q, 0, i % per_q)),
        ],
        out_shape=[
            jax.ShapeDtypeStruct((B, MLA_HEADS, nq, HEAD_W, tq), bf16),
            jax.ShapeDtypeStruct((B, MLA_HEADS, S, HEAD_W), bf16),
            jax.ShapeDtypeStruct((B, nq, MLA_HEADS * V_HEAD, tq), bf16),
        ],
        compiler_params=_cparams(("arbitrary",)),
        name="mla_proj",
    )(x2, pos2, mod, win, qg, kg, wuqt, wkn, wvt, invf)


ONES_ROWS = 16


def _attn_kernel(qt_ref, k_ref, vt_ref, o_ref, vx_ref, m_ref, acc_ref, ot_ref, *, tq):
    nq = qt_ref.shape[1]
    for hh in range(2):
        for j in range(nq):
            vx_ref[hh, j, :V_HEAD, :] = vt_ref[j, hh * V_HEAD:(hh + 1) * V_HEAD, :]
            vx_ref[hh, j, V_HEAD:, :] = jnp.ones((ONES_ROWS, tq), bf16)

    def kv_step(qi, j):
        scores = [_dot(k_ref[hh, pl.ds(j * tq, tq), :], qt_ref[hh, qi]) for hh in range(2)]
        for hh in range(2):
            update(hh, 0, tq, [(scores[hh], vx_ref[hh, j])])

    def update(hh, lo, hi, parts):
        m_old = m_ref[hh, :, lo:hi]
        m_new = m_old
        for s, _ in parts:
            m_new = jnp.maximum(m_new, jnp.max(s, axis=0, keepdims=True))
        alpha = jnp.exp2(m_old - m_new)
        pv = sum(_dot(vx, jnp.exp2(s - m_new[0:1, :]).astype(bf16)) for s, vx in parts)
        acc_ref[hh, :, lo:hi] = acc_ref[hh, :, lo:hi] * alpha[0:1, :] + pv
        m_ref[hh, :, lo:hi] = m_new

    def diag_step(qi):
        half = tq // 2
        k0 = qi * tq
        tops = [_dot(k_ref[hh, pl.ds(k0, half), :], qt_ref[hh, qi]) for hh in range(2)]
        bots = [_dot(k_ref[hh, pl.ds(k0 + half, half), :], qt_ref[hh, qi, :, half:]) for hh in range(2)]
        key = lax.broadcasted_iota(jnp.int32, (half, half), 0)
        qry = lax.broadcasted_iota(jnp.int32, (half, half), 1)
        visible = key <= qry
        for hh in range(2):
            vx = vx_ref[hh, qi]
            early = jnp.where(visible, tops[hh][:, :half], -jnp.inf)
            late = jnp.where(visible, bots[hh], -jnp.inf)
            update(hh, 0, half, [(early, vx[:, :half])])
            update(hh, half, tq, [(tops[hh][:, half:], vx[:, :half]), (late, vx[:, half:])])

    def q_body(qi, carry):
        m_ref[...] = jnp.full(m_ref.shape, -jnp.inf, f32)
        acc_ref[...] = jnp.zeros(acc_ref.shape, f32)

        for j in range(qi):
            kv_step(qi, j)
        diag_step(qi)
        for hh in range(2):
            acc = acc_ref[hh]
            ot_ref[hh * V_HEAD:(hh + 1) * V_HEAD, :] = acc[:V_HEAD, :] * (1.0 / acc[V_HEAD:V_HEAD + 1, :])
        o_ref[pl.ds(qi * tq, tq), :] = ot_ref[...].T.astype(bf16)
        return carry

    for qi in range(nq):
        q_body(qi, 0)


def _attention(qt, k, vt, tq):
    B, H, nq, W, _ = qt.shape
    S = nq * tq
    return pl.pallas_call(
        functools.partial(_attn_kernel, tq=tq),
        grid=(B, H // 2),
        in_specs=[
            pl.BlockSpec((None, 2, nq, W, tq), lambda b, p: (b, p, 0, 0, 0)),
            pl.BlockSpec((None, 2, S, W), lambda b, p: (b, p, 0, 0)),
            pl.BlockSpec((None, nq, 2 * V_HEAD, tq), lambda b, p: (b, 0, p, 0)),
        ],
        out_specs=pl.BlockSpec((None, S, LANES), lambda b, p: (b, 0, p)),
        out_shape=jax.ShapeDtypeStruct((B, S, H * V_HEAD), bf16),
        scratch_shapes=[
            pltpu.VMEM((2, nq, V_HEAD + ONES_ROWS, tq), bf16),
            pltpu.VMEM((2, SUBLANES, tq), f32),
            pltpu.VMEM((2, V_HEAD + ONES_ROWS, tq), f32),
            pltpu.VMEM((2 * V_HEAD, tq), f32),
        ],
        compiler_params=_cparams(("arbitrary", "arbitrary")),
        name="mla_attention",
    )(qt, k, vt)


def _post_mixer(x, y, mod_ref, ln_ref, wr_ref, br_ref, xo_ref, hm_ref, cls_ref):
    tm, D = x.shape
    x1 = _layer_norm(DN_ALPHA * x + (1.0 + mod_ref[2:3, :]) * y, ln_ref[0:1, :], ln_ref[1:2, :])
    xo_ref[...] = x1
    h2 = x1 * (1.0 + mod_ref[4:5, :]) + mod_ref[3:4, :]
    h_hi, h_lo = _split_bf16(h2)
    wr = wr_ref[...]
    t_hi = _dot(h_hi, wr)
    t_lo = _dot(h_lo, wr)
    logits = t_hi[:, :LANES] + (t_hi[:, LANES:] + t_lo[:, :LANES]) + br_ref[...]

    lane = lax.broadcasted_iota(jnp.int32, logits.shape, 1).astype(f32)
    neg = -jnp.inf
    far = float(LANES)
    is_g = lane < MOE_GROUPS
    gl = jnp.where(is_g, logits, neg)
    gmax = jnp.max(gl, axis=-1, keepdims=True)
    gidx = jnp.min(jnp.where(gl == gmax, lane, far), axis=-1, keepdims=True)
    g_p = 1.0 / jnp.sum(jnp.where(is_g, jnp.exp(logits - gmax), 0.0), axis=-1, keepdims=True)
    base = MOE_GROUPS + EXPERTS_PER_GROUP * gidx
    el = jnp.where((lane >= base) & (lane < base + EXPERTS_PER_GROUP), logits, neg)
    t1 = jnp.max(el, axis=-1, keepdims=True)
    i1 = jnp.min(jnp.where(el == t1, lane, far), axis=-1, keepdims=True)
    el2 = jnp.where(lane == i1, neg, el)
    t2 = jnp.max(el2, axis=-1, keepdims=True)
    i2 = jnp.min(jnp.where(el2 == t2, lane, far), axis=-1, keepdims=True)
    e = jnp.exp(t2 - t1)
    w1 = g_p / (1.0 + e)
    w2 = g_p * e / (1.0 + e)
    a = i1 - base
    b = i2 - base
    a_first = a < b
    lo = jnp.where(a_first, a, b)
    hi = jnp.where(a_first, b, a)
    w_lo = jnp.where(a_first, w1, w2)
    w_hi = jnp.where(a_first, w2, w1)
    pair = lo * (2 * EXPERTS_PER_GROUP - 1 - lo) * 0.5 + (hi - lo - 1.0)
    cls_ref[...] = (gidx * PAIRS_PER_GROUP + pair).astype(jnp.int32)

    _rows_store(hm_ref, h2, HM_ROWS)
    hm_ref.reshape(tm, HM_ROWS, LANES)[:, D // LANES:, :] = jnp.zeros((tm, HM_ROWS - D // LANES, LANES), f32)
    hm_ref[pl.ds(D // LANES, tm, stride=HM_ROWS), :] = jnp.broadcast_to(w_lo, (tm, LANES))
    hm_ref[pl.ds(D // LANES + 1, tm, stride=HM_ROWS), :] = jnp.broadcast_to(w_hi, (tm, LANES))


def _attn_out_kernel(o_ref, x_ref, mod_ref, wo_ref, ln_ref, wr_ref, br_ref,
                     xo_ref, hm_ref, cls_ref):
    y = _dot(o_ref[...], wo_ref[...])
    _post_mixer(x_ref[...], y, mod_ref, ln_ref, wr_ref, br_ref, xo_ref, hm_ref, cls_ref)


def _post_out_specs(N, D, tm):
    specs = [
        pl.BlockSpec((tm, D), lambda i: (i, 0)),
        pl.BlockSpec((tm * HM_ROWS, LANES), lambda i: (i, 0)),
        pl.BlockSpec((tm, 1), lambda i: (i, 0)),
    ]
    shapes = [
        jax.ShapeDtypeStruct((N, D), f32),
        jax.ShapeDtypeStruct((N * HM_ROWS, LANES), f32),
        jax.ShapeDtypeStruct((N, 1), jnp.int32),
    ]
    return specs, shapes


def _attn_out(o2, x2, mod, wo, ln, wr, br, S, tm):
    N, D = x2.shape
    spb = S // tm
    const = lambda shape: pl.BlockSpec(shape, lambda i: (0,) * len(shape))
    out_specs, out_shapes = _post_out_specs(N, D, tm)
    return pl.pallas_call(
        _attn_out_kernel,
        grid=(N // tm,),
        in_specs=[
            pl.BlockSpec((tm, D), lambda i: (i, 0)),
            pl.BlockSpec((tm, D), lambda i: (i, 0)),
            pl.BlockSpec((None, 8, D), lambda i: (i // spb, 0, 0)),
            const(wo.shape), const(ln.shape), const(wr.shape), const(br.shape),
        ],
        out_specs=out_specs,
        out_shape=out_shapes,
        compiler_params=_cparams(("arbitrary",)),
        name="attn_out_router",
    )(o2, x2, mod, wo, ln, wr, br)


def _pool_kernel(x_ref, y_ref, modp_ref, lnp_ref, mod_ref, wp_ref, ps_ref, ln_ref,
                 wr_ref, br_ref, xo_ref, hm_ref, cls_ref, hb_ref, *, spb):
    tm, D = x_ref.shape
    i = pl.program_id(0)
    t_blk = i % spb
    yprev = _rows_load(y_ref, tm, Y_ROWS, D)
    x2 = _layer_norm(DN_ALPHA * x_ref[...] + (1.0 + modp_ref[5:6, :]) * yprev,
                     lnp_ref[0:1, :], lnp_ref[1:2, :])
    h = x2 * (1.0 + mod_ref[1:2, :]) + mod_ref[0:1, :]

    @pl.when(t_blk == 0)
    def _():
        hb_ref[0:HALO, :] = jnp.zeros((HALO, D), f32)

    hb_ref[HALO:, :] = h
    t_seq = t_blk * tm + lax.broadcasted_iota(jnp.int32, (tm, 1), 0)
    gd = D // len(POOL_WINDOWS)
    ys = []
    for gi, w in enumerate(POOL_WINDOWS):
        c0, c1 = gi * gd, (gi + 1) * gd
        a = hb_ref[:, c0:c1]
        span = 1
        while span < w:
            a = a[span:, :] + a[:-span, :]
            span *= 2
        tsum = a[HALO - (w - 1):, :]
        cnt = jnp.minimum(t_seq + 1, w).astype(f32)
        mixed = tsum / cnt - h[:, c0:c1]
        ys.append(_dot(mixed.astype(bf16), wp_ref[gi]))
    hb_ref[0:HALO, :] = h[tm - HALO:, :]
    y = jnp.concatenate(ys, axis=1) * ps_ref[...]
    _post_mixer(x2, y, mod_ref, ln_ref, wr_ref, br_ref, xo_ref, hm_ref, cls_ref)


def _pool_layer(x2, yprev, modp, lnp, mod, wp, ps, ln, wr, br, S, tm):
    N, D = x2.shape
    spb = S // tm
    const = lambda shape: pl.BlockSpec(shape, lambda i: (0,) * len(shape))
    out_specs, out_shapes = _post_out_specs(N, D, tm)
    return pl.pallas_call(
        functools.partial(_pool_kernel, spb=spb),
        grid=(N // tm,),
        in_specs=[
            pl.BlockSpec((tm, D), lambda i: (i, 0)),
            pl.BlockSpec((tm * Y_ROWS, LANES), lambda i: (i, 0)),
            pl.BlockSpec((None, 8, D), lambda i: (i // spb, 0, 0)),
            const(lnp.shape),
            pl.BlockSpec((None, 8, D), lambda i: (i // spb, 0, 0)),
            const(wp.shape), const(ps.shape), const(ln.shape),
            const(wr.shape), const(br.shape),
        ],
        out_specs=out_specs,
        out_shape=out_shapes,
        scratch_shapes=[pltpu.VMEM((HALO + tm, D), f32)],
        compiler_params=_cparams(("arbitrary",)),
        name="pool_mixer_router",
    )(x2, yprev, modp, lnp, mod, wp, ps, ln, wr, br)


def _final_kernel(x_ref, y_ref, mod_ref, ln_ref, o_ref):
    tm, D = x_ref.shape
    y = _rows_load(y_ref, tm, Y_ROWS, D)
    o_ref[...] = _layer_norm(DN_ALPHA * x_ref[...] + (1.0 + mod_ref[5:6, :]) * y,
                             ln_ref[0:1, :], ln_ref[1:2, :])


def _final_merge(x2, y, mod, ln, S, tm):
    N, D = x2.shape
    spb = S // tm
    return pl.pallas_call(
        _final_kernel,
        grid=(N // tm,),
        in_specs=[
            pl.BlockSpec((tm, D), lambda i: (i, 0)),
            pl.BlockSpec((tm * Y_ROWS, LANES), lambda i: (i, 0)),
            pl.BlockSpec((None, 8, D), lambda i: (i // spb, 0, 0)),
            pl.BlockSpec(ln.shape, lambda i: (0, 0)),
        ],
        out_specs=pl.BlockSpec((tm, D), lambda i: (i, 0)),
        out_shape=jax.ShapeDtypeStruct((N, D), f32),
        compiler_params=_cparams(("arbitrary",)),
        name="final_merge",
    )(x2, y, mod, ln)


def _rank_kernel(cls_ref, pos_ref, cnt_ref, carry_ref, off_ref, earlier_ref):
    tr = cls_ref.shape[0]
    phase, tile = pl.program_id(0), pl.program_id(1)

    @pl.when(tile == 0)
    def _():
        @pl.when(phase == 0)
        def _():
            r = lax.broadcasted_iota(jnp.int32, (tr, tr), 0)
            c = lax.broadcasted_iota(jnp.int32, (tr, tr), 1)
            earlier_ref[...] = (c < r).astype(bf16)

        @pl.when(phase == 1)
        def _():
            tiles = jnp.floor((carry_ref[...] + (ROW_TILE - 1)) * (1.0 / ROW_TILE))
            hi = jnp.floor(tiles * (1.0 / 16.0))
            lo = tiles - 16.0 * hi
            before = (lax.broadcasted_iota(jnp.int32, (LANES, LANES), 0)
                      < lax.broadcasted_iota(jnp.int32, (LANES, LANES), 1)).astype(bf16)
            starts = 16.0 * _dot(hi.astype(bf16), before) + _dot(lo.astype(bf16), before)
            off_ref[...] = starts * ROW_TILE

        carry_ref[...] = jnp.zeros(carry_ref.shape, f32)

    lane = lax.broadcasted_iota(jnp.int32, (tr, LANES), 1)
    onehot = (cls_ref[...] == lane)
    oh = onehot.astype(bf16)
    total = carry_ref[0:1, :] + jnp.sum(oh.astype(f32), axis=0, keepdims=True)

    @pl.when(phase == 0)
    def _():
        cnt_ref[...] = jnp.broadcast_to(total, cnt_ref.shape).astype(jnp.int32)

    @pl.when(phase == 1)
    def _():
        ahead = _dot(earlier_ref[...], oh) + (carry_ref[0:1, :] + off_ref[0:1, :])
        pos_ref[...] = jnp.sum(jnp.where(onehot, ahead, 0.0), axis=-1, keepdims=True).astype(jnp.int32)

    carry_ref[...] = jnp.broadcast_to(total, carry_ref.shape)


def _class_slots(cls, tr):
    N = cls.shape[0]
    return pl.pallas_call(
        _rank_kernel,
        grid=(2, N // tr),
        in_specs=[pl.BlockSpec((tr, 1), lambda p, i: (i, 0))],
        out_specs=[
            pl.BlockSpec((tr, 1), lambda p, i: (i * p, 0)),
            pl.BlockSpec((SUBLANES, LANES), lambda p, i: (0, 0)),
        ],
        out_shape=[
            jax.ShapeDtypeStruct((N, 1), jnp.int32),
            jax.ShapeDtypeStruct((SUBLANES, LANES), jnp.int32),
        ],
        scratch_shapes=[pltpu.VMEM((SUBLANES, LANES), f32), pltpu.VMEM((SUBLANES, LANES), f32),
                        pltpu.VMEM((tr, tr), bf16)],
        compiler_params=_cparams(("arbitrary", "arbitrary")),
        name="class_slots",
    )(cls)


def _slot_token_kernel(pos_ref, inv_ref, *, chunk):
    base = pl.program_id(0) * chunk

    @pl.when(pl.program_id(0) == 0)
    def _():
        def clear(s, c):
            inv_ref[s] = 0
            return c
        lax.fori_loop(0, inv_ref.shape[0], clear, 0, unroll=32)

    def body(t, c):
        inv_ref[pos_ref[t]] = base + t
        return c

    lax.fori_loop(0, chunk, body, 0, unroll=16)


def _slot_tokens(pos, n_slots, chunk):
    N = pos.shape[0]
    return pl.pallas_call(
        functools.partial(_slot_token_kernel, chunk=chunk),
        grid=(N // chunk,),
        in_specs=[pl.BlockSpec((chunk,), lambda i: (i,), memory_space=pltpu.SMEM)],
        out_specs=pl.BlockSpec((n_slots,), lambda i: (0,), memory_space=pltpu.SMEM),
        out_shape=jax.ShapeDtypeStruct((n_slots,), jnp.int32),
        compiler_params=_cparams(("arbitrary",)),
        name="slot_tokens",
    )(pos)


def _moe_kernel(te1_ref, te2_ref, cnt_ref, inv_ref, hm_ref, wgu1_ref, wd1_ref, wgu2_ref, wd2_ref,
                y_ref, *scratch):
    hbufs = scratch[:MOE_SLOTS]
    ybufs = scratch[MOE_SLOTS:2 * MOE_SLOTS]
    gsem, ssem = scratch[2 * MOE_SLOTS:]
    ff, D = wd1_ref.shape
    i = pl.program_id(0)

    def count(tile):
        return jnp.where(tile >= 0, cnt_ref[jnp.maximum(tile, 0)], 0)

    n_cur, n_prev, n_ahead, n_old = count(i), count(i - 1), count(i + 2), count(i - MOE_SLOTS)

    def token_rows(ref, t, pitch):
        return ref.at[pl.ds(pl.multiple_of(t * pitch, pitch), pitch)]

    def start_gather(tok, r, hb, sem):
        pltpu.make_async_copy(token_rows(hm_ref, tok, HM_ROWS), token_rows(hb, r, HM_ROWS), sem).start()

    def start_scatter(tok, r, yb, sem):
        pltpu.make_async_copy(token_rows(yb, r, Y_ROWS), token_rows(y_ref, tok, Y_ROWS), sem).start()

    def wait_tokens(src, dst, sem, n, pitch):
        @pl.when(n > 0)
        def _():
            rows = pl.multiple_of(n * pitch, pitch)
            pltpu.make_async_copy(src.at[pl.ds(0, rows)], dst.at[pl.ds(0, rows)], sem).wait()

    @pl.when(i == 0)
    def _():
        for t in range(MOE_SLOTS - 1):
            def first(r, c):
                start_gather(inv_ref[t * ROW_TILE + r], r, hbufs[t], gsem.at[t])
                return c
            lax.fori_loop(0, cnt_ref[t], first, 0)

    def step(cur):
        far = (cur + MOE_SLOTS - 1) % MOE_SLOTS
        hb, yb = hbufs[cur], ybufs[cur]
        wait_tokens(hm_ref, hb, gsem.at[cur], n_cur, HM_ROWS)
        wait_tokens(yb, y_ref, ssem.at[cur], n_old, Y_ROWS)

        @pl.when((n_cur > 0) | (n_prev > 0))
        def _():
            ahead = (i + 2) * ROW_TILE
            prev = jnp.maximum(i - 1, 0) * ROW_TILE
            for r in range(ROW_TILE):
                tok_ahead = inv_ref[ahead + r]
                tok_prev = inv_ref[prev + r]

                @pl.when(r < n_ahead)
                def _():
                    start_gather(tok_ahead, r, hbufs[far], gsem.at[far])

                @pl.when(r < n_prev)
                def _():
                    start_scatter(tok_prev, r, ybufs[far], ssem.at[far])

            h = _rows_load(hb, ROW_TILE, HM_ROWS, D).astype(bf16)

            def hidden(gu, gate_row):
                w = hb[pl.ds(gate_row, ROW_TILE, stride=HM_ROWS), :]
                g, u = gu[:, :ff], gu[:, ff:]
                he = g * (1.0 / (1.0 + jnp.exp(-g))) * u
                return (he * jnp.concatenate([w] * (ff // LANES), axis=1)).astype(bf16)

            gu1 = _dot(h, wgu1_ref[...])
            gu2 = _dot(h, wgu2_ref[...])
            y = _dot(hidden(gu1, D // LANES), wd1_ref[...]) + _dot(hidden(gu2, D // LANES + 1), wd2_ref[...])
            _rows_store(yb, y, Y_ROWS)

    for cur in range(MOE_SLOTS):
        @pl.when(i % MOE_SLOTS == cur)
        def _():
            step(cur)


def _moe(te1, te2, cnt, inv, hm, wgu, wd, layer, n_steps):
    N = hm.shape[0] // HM_ROWS
    _, E, D, ff2 = wgu.shape
    ff = ff2 // 2

    def wmap(which):
        def f(i, te1, te2, cnt, inv):
            return (layer, (te1, te2)[which][i], 0, 0)
        return f

    return pl.pallas_call(
        _moe_kernel,
        grid_spec=pltpu.PrefetchScalarGridSpec(
            num_scalar_prefetch=4,
            grid=(n_steps,),
            in_specs=[
                pl.BlockSpec(memory_space=pl.ANY),
                pl.BlockSpec((None, None, D, ff2), wmap(0)),
                pl.BlockSpec((None, None, ff, D), wmap(0)),
                pl.BlockSpec((None, None, D, ff2), wmap(1)),
                pl.BlockSpec((None, None, ff, D), wmap(1)),
            ],
            out_specs=pl.BlockSpec(memory_space=pl.ANY),
            scratch_shapes=(
                [pltpu.VMEM((ROW_TILE * HM_ROWS, LANES), f32)] * MOE_SLOTS
                + [pltpu.VMEM((ROW_TILE * Y_ROWS, LANES), f32)] * MOE_SLOTS
                + [pltpu.SemaphoreType.DMA((MOE_SLOTS,)), pltpu.SemaphoreType.DMA((MOE_SLOTS,))]
            ),
        ),
        out_shape=jax.ShapeDtypeStruct((N * Y_ROWS, LANES), f32),
        compiler_params=_cparams(("arbitrary",)),
        name="pair_moe",
    )(te1, te2, cnt, inv, hm, wgu, wd, wgu, wd)


def _pair_tables():
    lo = np.zeros((N_CLASSES,), np.int32)
    hi = np.zeros((N_CLASSES,), np.int32)
    for g in range(MOE_GROUPS):
        p = 0
        for a in range(EXPERTS_PER_GROUP):
            for b in range(a + 1, EXPERTS_PER_GROUP):
                lo[g * PAIRS_PER_GROUP + p] = g * EXPERTS_PER_GROUP + a
                hi[g * PAIRS_PER_GROUP + p] = g * EXPERTS_PER_GROUP + b
                p += 1
    return lo, hi


def _moe_layer(hm, cls, expert_w, layer, chunk, tr):
    N = cls.shape[0]
    n_tiles = N // ROW_TILE + N_CLASSES
    n_steps = n_tiles + MOE_SLOTS
    pos, cnt = _class_slots(cls, tr)
    counts = cnt[0, :N_CLASSES]
    tiles = (counts + ROW_TILE - 1) // ROW_TILE
    tile_end = jnp.cumsum(tiles)
    tile_start = tile_end - tiles
    nused = tile_end[-1]
    step = jnp.arange(n_steps + 2, dtype=jnp.int32)
    tile_cls = jnp.sum(tile_end[None, :] <= jnp.minimum(step, nused - 1)[:, None], axis=1)
    tile_cls = jnp.minimum(tile_cls, N_CLASSES - 1).astype(jnp.int32)
    lo_tab, hi_tab = _pair_tables()
    te1 = jnp.asarray(lo_tab)[tile_cls]
    te2 = jnp.asarray(hi_tab)[tile_cls]
    in_cls = (step - tile_start[tile_cls]) * ROW_TILE
    tile_cnt = jnp.where(step < nused, jnp.clip(counts[tile_cls] - in_cls, 0, ROW_TILE), 0).astype(jnp.int32)

    n_slots = -(-(n_tiles + MOE_SLOTS) * ROW_TILE // SMEM_BLOCK) * SMEM_BLOCK
    inv = _slot_tokens(pos.reshape(N), n_slots, chunk)
    return _moe(te1, te2, tile_cnt, inv, hm, *expert_w, layer, n_steps)


def _swap_rope(w):
    half = QK_ROPE // 2
    return jnp.concatenate([-w[..., half:], w[..., :half]], axis=-1)


def _prep_mla_weights(w_in, w_uq, w_ukv):
    D = w_in.shape[0]
    w_kr = w_in[:, Q_LORA + KV_LORA:]
    w_ks = _swap_rope(w_kr)
    zeros = jnp.zeros((D, QK_NOPE), w_in.dtype)
    win = jnp.concatenate([w_in[:, :Q_LORA + KV_LORA], zeros, w_kr, w_kr, zeros, w_ks, w_ks], axis=1)
    uq = w_uq.reshape(Q_LORA, MLA_HEADS, QK_NOPE + QK_ROPE)
    rope = uq[..., QK_NOPE:]
    wuq = jnp.concatenate([uq[..., :QK_NOPE], rope, _swap_rope(rope)], axis=-1).reshape(Q_LORA, -1)
    ukv = w_ukv.reshape(KV_LORA, MLA_HEADS, QK_NOPE + V_HEAD)
    wkn = jnp.concatenate([ukv[..., :QK_NOPE], jnp.zeros((KV_LORA, MLA_HEADS, HEAD_W - QK_NOPE), w_ukv.dtype)],
                          axis=-1).reshape(KV_LORA, -1)
    wv = ukv[..., QK_NOPE:].reshape(KV_LORA, -1)
    return win.astype(bf16), wuq.T.astype(bf16), wkn.astype(bf16), wv.T.astype(bf16)


def _rope_inv_freq():
    inv = 1.0 / (ROPE_THETA ** (np.arange(0, QK_ROPE, 2, dtype=np.float32) / QK_ROPE))
    inv2 = np.concatenate([inv, inv]).astype(np.float32)
    return np.concatenate([np.zeros((QK_NOPE,), np.float32), inv2, inv2]).reshape(1, LANES)


def _prep_router(w_gr, b_gr, w_er, b_er):
    D = w_gr.shape[0]
    n = MOE_GROUPS + MOE_GROUPS * EXPERTS_PER_GROUP
    w = jnp.concatenate([w_gr, w_er, jnp.zeros((D, LANES - n), f32)], axis=1)
    b = jnp.concatenate([b_gr, b_er, jnp.zeros((LANES - n,), f32)]).reshape(1, LANES)
    hi, lo = _split_bf16(w)
    return jnp.concatenate([hi, lo], axis=1), b


def kernel(x, c, positions, w_mod, b_mod, ln_g, ln_b, w_in, q_norm_g, kv_norm_g, w_uq, w_ukv, w_o, w_pool,
           pool_scale, w_group_router, b_group_router, w_expert_router, b_expert_router, w_gate, w_up, w_down):
    B, S, D = x.shape
    N = B * S
    tm = min(512, S)
    tq = min(1024, S)
    tr = min(1024, N)
    chunk = min(2048, N)

    mod = _modulation(c, w_mod, b_mod)
    ln = jnp.stack([ln_g, ln_b], axis=2)
    x2 = x.reshape(N, D)

    win, wuqt, wkn, wvt = _prep_mla_weights(w_in[0], w_uq[0], w_ukv[0])
    qt, k, vt = _mla_proj(x2, positions.reshape(N, 1), mod[0], win, q_norm_g[0].reshape(1, -1),
                          kv_norm_g[0].reshape(1, -1), wuqt, wkn, wvt, jnp.asarray(_rope_inv_freq()),
                          B, S, tm, tq)
    o = _attention(qt, k, vt, tq)
    wr, br = _prep_router(w_group_router[0], b_group_router[0], w_expert_router[0], b_expert_router[0])
    x1, hm, cls = _attn_out(o.reshape(N, D), x2, mod[0], w_o[0].astype(bf16), ln[0, 0], wr, br, S, tm)
    expert_w = (jnp.concatenate([w_gate, w_up], axis=-1).astype(bf16), w_down.astype(bf16))
    y0 = _moe_layer(hm, cls, expert_w, 0, chunk, tr)

    wr, br = _prep_router(w_group_router[1], b_group_router[1], w_expert_router[1], b_expert_router[1])
    x3, hm, cls = _pool_layer(x1, y0, mod[0], ln[0, 1], mod[1], w_pool[0].astype(bf16),
                              pool_scale[0].reshape(1, D), ln[1, 0], wr, br, S, tm)
    y1 = _moe_layer(hm, cls, expert_w, 1, chunk, tr)

    out = _final_merge(x3, y1, mod[1], ln[1, 1], S, tm)
    return out.reshape(B, S, D)
```

```python
import functools

import numpy as np
import jax
import jax.numpy as jnp
from jax import lax
from jax.experimental import pallas as pl
from jax.experimental.pallas import tpu as pltpu

MLA_HEADS = 16
Q_LORA = 384
KV_LORA = 256
QK_NOPE = 64
QK_ROPE = 32
V_HEAD = 64
ROPE_THETA = 10000.0
POOL_WINDOWS = (2, 4, 8, 16)
MOE_GROUPS = 4
EXPERTS_PER_GROUP = 8
DEPTH = 2
DN_ALPHA = (2.0 * DEPTH) ** 0.25
LN_EPS = 1e-5
RMS_EPS = 1e-6
LOG2E = 1.4426950408889634

LANES = 128
SUBLANES = 8
VMEM_LIMIT = 56 * 1024 * 1024

HEAD_W = 128
PAIRS_PER_GROUP = EXPERTS_PER_GROUP * (EXPERTS_PER_GROUP - 1) // 2
N_CLASSES = MOE_GROUPS * PAIRS_PER_GROUP
ROW_TILE = 128
HM_ROWS = 16
Y_ROWS = 8
MOE_SLOTS = 3
SMEM_BLOCK = 1024
HALO = 16

f32 = jnp.float32
bf16 = jnp.bfloat16


def _cparams(sem):
    return pltpu.CompilerParams(dimension_semantics=sem, vmem_limit_bytes=VMEM_LIMIT)


def _split_bf16(a):
    hi = a.astype(bf16)
    lo = (a - hi.astype(f32)).astype(bf16)
    return hi, lo


def _dot(a, b):
    return jnp.dot(a, b, preferred_element_type=f32)


def _dot3(a_hi, a_lo, b_hi, b_lo):
    return _dot(a_hi, b_hi) + (_dot(a_lo, b_hi) + _dot(a_hi, b_lo))


def _rows_load(ref, n, pitch, width):
    return jnp.concatenate([ref[pl.ds(s, n, stride=pitch), :] for s in range(width // LANES)], axis=1)


def _rows_store(ref, val, pitch):
    n, width = val.shape
    for s in range(width // LANES):
        ref[pl.ds(s, n, stride=pitch), :] = val[:, s * LANES:(s + 1) * LANES]


def _layer_norm(x, g, b):
    mu = jnp.mean(x, axis=-1, keepdims=True)
    xc = x - mu
    var = jnp.mean(xc * xc, axis=-1, keepdims=True)
    return xc * lax.rsqrt(var + LN_EPS) * g + b


def _mod_kernel(c_ref, w_ref, b_ref, o_ref):
    c = c_ref[...]
    ca = c * (1.0 / (1.0 + jnp.exp(-c)))
    a_hi, a_lo = _split_bf16(ca)
    w_hi, w_lo = _split_bf16(w_ref[...])
    o_ref[...] = _dot3(a_hi, a_lo, w_hi, w_lo) + b_ref[...]


def _modulation(c, w_mod, b_mod):
    B, D = c.shape
    depth = w_mod.shape[0]
    out = pl.pallas_call(
        _mod_kernel,
        grid=(depth, 6),
        in_specs=[
            pl.BlockSpec((B, D), lambda i, j: (0, 0)),
            pl.BlockSpec((None, D, D), lambda i, j: (i, 0, j)),
            pl.BlockSpec((None, None, 1, D), lambda i, j: (i, j, 0, 0)),
        ],
        out_specs=pl.BlockSpec((None, None, B, D), lambda i, j: (i, j, 0, 0)),
        out_shape=jax.ShapeDtypeStruct((depth, 6, B, D), f32),
        compiler_params=_cparams(("arbitrary", "arbitrary")),
        name="adaln_mod",
    )(c, w_mod, b_mod.reshape(depth, 6, 1, D))
    out = jnp.transpose(out, (0, 2, 1, 3))
    return jnp.pad(out, ((0, 0), (0, 0), (0, 2), (0, 0)))


def _mla_proj_kernel(x_ref, pos_ref, mod_ref, win_ref, qg_ref, kg_ref, wuqt_ref, wkn_ref, wvt_ref,
                     invf_ref, qt_ref, k_ref, vt_ref):
    x = x_ref[...]
    h = x * (1.0 + mod_ref[1:2, :]) + mod_ref[0:1, :]
    z = _dot(h.astype(bf16), win_ref[...])
    zq = z[:, :Q_LORA]
    zkv = z[:, Q_LORA:Q_LORA + KV_LORA]
    za = z[:, Q_LORA + KV_LORA:Q_LORA + KV_LORA + LANES]
    zb = z[:, Q_LORA + KV_LORA + LANES:]
    q_lat = zq * lax.rsqrt(jnp.mean(zq * zq, axis=-1, keepdims=True) + RMS_EPS) * qg_ref[...]
    kv_lat = zkv * lax.rsqrt(jnp.mean(zkv * zkv, axis=-1, keepdims=True) + RMS_EPS) * kg_ref[...]

    ang = pos_ref[...].astype(f32) * invf_ref[...]
    cs = jnp.cos(ang)
    sn = jnp.sin(ang)
    lane = lax.broadcasted_iota(jnp.int32, cs.shape, 1)
    scale = (QK_NOPE + QK_ROPE) ** -0.5 * LOG2E
    tqn = scale * jnp.where(lane < QK_NOPE + QK_ROPE, cs, sn)
    kr = za * cs + zb * sn

    kn = _dot(kv_lat.astype(bf16), wkn_ref[...])
    for hd in range(MLA_HEADS):
        k_ref[hd] = (kn[:, hd * HEAD_W:(hd + 1) * HEAD_W] + kr).astype(bf16)

    tqt = tqn.T
    qt = _dot(wuqt_ref[...], q_lat.T.astype(bf16))
    for hd in range(MLA_HEADS):
        qt_ref[hd] = (qt[hd * HEAD_W:(hd + 1) * HEAD_W, :] * tqt).astype(bf16)
    vt_ref[...] = _dot(wvt_ref[...], kv_lat.T.astype(bf16)).astype(bf16)


def _mla_proj(x2, pos2, mod, win, qg, kg, wuqt, wkn, wvt, invf, B, S, tm, tq):
    N, D = x2.shape
    spb = S // tm
    per_q = tq // tm
    const = lambda shape: pl.BlockSpec(shape, lambda i: (0,) * len(shape))
    nq = S // tq
    return pl.pallas_call(
        _mla_proj_kernel,
        grid=(N // tm,),
        in_specs=[
            pl.BlockSpec((tm, D), lambda i: (i, 0)),
            pl.BlockSpec((tm, 1), lambda i: (i, 0)),
            pl.BlockSpec((None, 8, D), lambda i: (i // spb, 0, 0)),
            const(win.shape), const(qg.shape), const(kg.shape), const(wuqt.shape), const(wkn.shape),
            const(wvt.shape), const(invf.shape),
        ],
        out_specs=[
            pl.BlockSpec((None, MLA_HEADS, None, HEAD_W, tm),
                         lambda i: (i // spb, 0, (i % spb) // per_q, 0, i % per_q)),
            pl.BlockSpec((None, MLA_HEADS, tm, HEAD_W), lambda i: (i // spb, 0, i % spb, 0)),
            pl.BlockSpec((None, None, MLA_HEADS * V_HEAD, tm),
                         lambda i: (i // spb, (i % spb) // per_q, 0, i % per_q)),
        ],
        out_shape=[
            jax.ShapeDtypeStruct((B, MLA_HEADS, nq, HEAD_W, tq), bf16),
            jax.ShapeDtypeStruct((B, MLA_HEADS, S, HEAD_W), bf16),
            jax.ShapeDtypeStruct((B, nq, MLA_HEADS * V_HEAD, tq), bf16),
        ],
        compiler_params=_cparams(("arbitrary",)),
        name="mla_proj",
    )(x2, pos2, mod, win, qg, kg, wuqt, wkn, wvt, invf)


ONES_ROWS = 16


def _attn_kernel(qt_ref, k_ref, vt_ref, o_ref, vx_ref, m_ref, acc_ref, ot_ref, *, tq):
    nq = qt_ref.shape[1]
    for hh in range(2):
        for j in range(nq):
            vx_ref[hh, j, :V_HEAD, :] = vt_ref[j, hh * V_HEAD:(hh + 1) * V_HEAD, :]
            vx_ref[hh, j, V_HEAD:, :] = jnp.ones((ONES_ROWS, tq), bf16)

    def kv_step(qi, j):
        scores = [_dot(k_ref[hh, pl.ds(j * tq, tq), :], qt_ref[hh, qi]) for hh in range(2)]
        for hh in range(2):
            update(hh, 0, tq, [(scores[hh], vx_ref[hh, j])])

    def update(hh, lo, hi, parts):
        m_old = m_ref[hh, :, lo:hi]
        m_new = m_old
        for s, _ in parts:
            m_new = jnp.maximum(m_new, jnp.max(s, axis=0, keepdims=True))
        alpha = jnp.exp2(m_old - m_new)
        pv = sum(_dot(vx, jnp.exp2(s - m_new[0:1, :]).astype(bf16)) for s, vx in parts)
        acc_ref[hh, :, lo:hi] = acc_ref[hh, :, lo:hi] * alpha[0:1, :] + pv
        m_ref[hh, :, lo:hi] = m_new

    def diag_step(qi):
        half = tq // 2
        k0 = qi * tq
        tops = [_dot(k_ref[hh, pl.ds(k0, half), :], qt_ref[hh, qi]) for hh in range(2)]
        bots = [_dot(k_ref[hh, pl.ds(k0 + half, half), :], qt_ref[hh, qi, :, half:]) for hh in range(2)]
        key = lax.broadcasted_iota(jnp.int32, (half, half), 0)
        qry = lax.broadcasted_iota(jnp.int32, (half, half), 1)
        visible = key <= qry
        for hh in range(2):
            vx = vx_ref[hh, qi]
            early = jnp.where(visible, tops[hh][:, :half], -jnp.inf)
            late = jnp.where(visible, bots[hh], -jnp.inf)
            update(hh, 0, half, [(early, vx[:, :half])])
            update(hh, half, tq, [(tops[hh][:, half:], vx[:, :half]), (late, vx[:, half:])])

    def q_body(qi, carry):
        m_ref[...] = jnp.full(m_ref.shape, -jnp.inf, f32)
        acc_ref[...] = jnp.zeros(acc_ref.shape, f32)

        for j in range(qi):
            kv_step(qi, j)
        diag_step(qi)
        for hh in range(2):
            acc = acc_ref[hh]
            ot_ref[hh * V_HEAD:(hh + 1) * V_HEAD, :] = acc[:V_HEAD, :] * (1.0 / acc[V_HEAD:V_HEAD + 1, :])
        o_ref[pl.ds(qi * tq, tq), :] = ot_ref[...].T.astype(bf16)
        return carry

    for qi in range(nq):
        q_body(qi, 0)


def _attention(qt, k, vt, tq):
    B, H, nq, W, _ = qt.shape
    S = nq * tq
    return pl.pallas_call(
        functools.partial(_attn_kernel, tq=tq),
        grid=(B, H // 2),
        in_specs=[
            pl.BlockSpec((None, 2, nq, W, tq), lambda b, p: (b, p, 0, 0, 0)),
            pl.BlockSpec((None, 2, S, W), lambda b, p: (b, p, 0, 0)),
            pl.BlockSpec((None, nq, 2 * V_HEAD, tq), lambda b, p: (b, 0, p, 0)),
        ],
        out_specs=pl.BlockSpec((None, S, LANES), lambda b, p: (b, 0, p)),
        out_shape=jax.ShapeDtypeStruct((B, S, H * V_HEAD), bf16),
        scratch_shapes=[
            pltpu.VMEM((2, nq, V_HEAD + ONES_ROWS, tq), bf16),
            pltpu.VMEM((2, SUBLANES, tq), f32),
            pltpu.VMEM((2, V_HEAD + ONES_ROWS, tq), f32),
            pltpu.VMEM((2 * V_HEAD, tq), f32),
        ],
        compiler_params=_cparams(("arbitrary", "arbitrary")),
        name="mla_attention",
    )(qt, k, vt)


def _post_mixer(x, y, mod_ref, ln_ref, wr_ref, br_ref, xo_ref, hm_ref, cls_ref):
    tm, D = x.shape
    x1 = _layer_norm(DN_ALPHA * x + (1.0 + mod_ref[2:3, :]) * y, ln_ref[0:1, :], ln_ref[1:2, :])
    xo_ref[...] = x1
    h2 = x1 * (1.0 + mod_ref[4:5, :]) + mod_ref[3:4, :]
    h_hi, h_lo = _split_bf16(h2)
    wr = wr_ref[...]
    t_hi = _dot(h_hi, wr)
    t_lo = _dot(h_lo, wr)
    logits = t_hi[:, :LANES] + (t_hi[:, LANES:] + t_lo[:, :LANES]) + br_ref[...]

    lane = lax.broadcasted_iota(jnp.int32, logits.shape, 1).astype(f32)
    neg = -jnp.inf
    far = float(LANES)
    is_g = lane < MOE_GROUPS
    gl = jnp.where(is_g, logits, neg)
    gmax = jnp.max(gl, axis=-1, keepdims=True)
    gidx = jnp.min(jnp.where(gl == gmax, lane, far), axis=-1, keepdims=True)
    g_p = 1.0 / jnp.sum(jnp.where(is_g, jnp.exp(logits - gmax), 0.0), axis=-1, keepdims=True)
    base = MOE_GROUPS + EXPERTS_PER_GROUP * gidx
    el = jnp.where((lane >= base) & (lane < base + EXPERTS_PER_GROUP), logits, neg)
    t1 = jnp.max(el, axis=-1, keepdims=True)
    i1 = jnp.min(jnp.where(el == t1, lane, far), axis=-1, keepdims=True)
    el2 = jnp.where(lane == i1, neg, el)
    t2 = jnp.max(el2, axis=-1, keepdims=True)
    i2 = jnp.min(jnp.where(el2 == t2, lane, far), axis=-1, keepdims=True)
    e = jnp.exp(t2 - t1)
    w1 = g_p / (1.0 + e)
    w2 = g_p * e / (1.0 + e)
    a = i1 - base
    b = i2 - base
    a_first = a < b
    lo = jnp.where(a_first, a, b)
    hi = jnp.where(a_first, b, a)
    w_lo = jnp.where(a_first, w1, w2)
    w_hi = jnp.where(a_first, w2, w1)
    pair = lo * (2 * EXPERTS_PER_GROUP - 1 - lo) * 0.5 + (hi - lo - 1.0)
    cls_ref[...] = (gidx * PAIRS_PER_GROUP + pair).astype(jnp.int32)

    _rows_store(hm_ref, h2, HM_ROWS)
    hm_ref.reshape(tm, HM_ROWS, LANES)[:, D // LANES:, :] = jnp.zeros((tm, HM_ROWS - D // LANES, LANES), f32)
    hm_ref[pl.ds(D // LANES, tm, stride=HM_ROWS), :] = jnp.broadcast_to(w_lo, (tm, LANES))
    hm_ref[pl.ds(D // LANES + 1, tm, stride=HM_ROWS), :] = jnp.broadcast_to(w_hi, (tm, LANES))


def _attn_out_kernel(o_ref, x_ref, mod_ref, wo_ref, ln_ref, wr_ref, br_ref,
                     xo_ref, hm_ref, cls_ref):
    y = _dot(o_ref[...], wo_ref[...])
    _post_mixer(x_ref[...], y, mod_ref, ln_ref, wr_ref, br_ref, xo_ref, hm_ref, cls_ref)


def _post_out_specs(N, D, tm):
    specs = [
        pl.BlockSpec((tm, D), lambda i: (i, 0)),
        pl.BlockSpec((tm * HM_ROWS, LANES), lambda i: (i, 0)),
        pl.BlockSpec((tm, 1), lambda i: (i, 0)),
    ]
    shapes = [
        jax.ShapeDtypeStruct((N, D), f32),
        jax.ShapeDtypeStruct((N * HM_ROWS, LANES), f32),
        jax.ShapeDtypeStruct((N, 1), jnp.int32),
    ]
    return specs, shapes


def _attn_out(o2, x2, mod, wo, ln, wr, br, S, tm):
    N, D = x2.shape
    spb = S // tm
    const = lambda shape: pl.BlockSpec(shape, lambda i: (0,) * len(shape))
    out_specs, out_shapes = _post_out_specs(N, D, tm)
    return pl.pallas_call(
        _attn_out_kernel,
        grid=(N // tm,),
        in_specs=[
            pl.BlockSpec((tm, D), lambda i: (i, 0)),
            pl.BlockSpec((tm, D), lambda i: (i, 0)),
            pl.BlockSpec((None, 8, D), lambda i: (i // spb, 0, 0)),
            const(wo.shape), const(ln.shape), const(wr.shape), const(br.shape),
        ],
        out_specs=out_specs,
        out_shape=out_shapes,
        compiler_params=_cparams(("arbitrary",)),
        name="attn_out_router",
    )(o2, x2, mod, wo, ln, wr, br)


def _pool_kernel(x_ref, y_ref, modp_ref, lnp_ref, mod_ref, wp_ref, ps_ref, ln_ref,
                 wr_ref, br_ref, xo_ref, hm_ref, cls_ref, hb_ref, *, spb):
    tm, D = x_ref.shape
    i = pl.program_id(0)
    t_blk = i % spb
    yprev = _rows_load(y_ref, tm, Y_ROWS, D)
    x2 = _layer_norm(DN_ALPHA * x_ref[...] + (1.0 + modp_ref[5:6, :]) * yprev,
                     lnp_ref[0:1, :], lnp_ref[1:2, :])
    h = x2 * (1.0 + mod_ref[1:2, :]) + mod_ref[0:1, :]

    @pl.when(t_blk == 0)
    def _():
        hb_ref[0:HALO, :] = jnp.zeros((HALO, D), f32)

    hb_ref[HALO:, :] = h
    t_seq = t_blk * tm + lax.broadcasted_iota(jnp.int32, (tm, 1), 0)
    gd = D // len(POOL_WINDOWS)
    ys = []
    for gi, w in enumerate(POOL_WINDOWS):
        c0, c1 = gi * gd, (gi + 1) * gd
        a = hb_ref[:, c0:c1]
        span = 1
        while span < w:
            a = a[span:, :] + a[:-span, :]
            span *= 2
        tsum = a[HALO - (w - 1):, :]
        cnt = jnp.minimum(t_seq + 1, w).astype(f32)
        mixed = tsum / cnt - h[:, c0:c1]
        ys.append(_dot(mixed.astype(bf16), wp_ref[gi]))
    hb_ref[0:HALO, :] = h[tm - HALO:, :]
    y = jnp.concatenate(ys, axis=1) * ps_ref[...]
    _post_mixer(x2, y, mod_ref, ln_ref, wr_ref, br_ref, xo_ref, hm_ref, cls_ref)


def _pool_layer(x2, yprev, modp, lnp, mod, wp, ps, ln, wr, br, S, tm):
    N, D = x2.shape
    spb = S // tm
    const = lambda shape: pl.BlockSpec(shape, lambda i: (0,) * len(shape))
    out_specs, out_shapes = _post_out_specs(N, D, tm)
    return pl.pallas_call(
        functools.partial(_pool_kernel, spb=spb),
        grid=(N // tm,),
        in_specs=[
            pl.BlockSpec((tm, D), lambda i: (i, 0)),
            pl.BlockSpec((tm * Y_ROWS, LANES), lambda i: (i, 0)),
            pl.BlockSpec((None, 8, D), lambda i: (i // spb, 0, 0)),
            const(lnp.shape),
            pl.BlockSpec((None, 8, D), lambda i: (i // spb, 0, 0)),
            const(wp.shape), const(ps.shape), const(ln.shape),
            const(wr.shape), const(br.shape),
        ],
        out_specs=out_specs,
        out_shape=out_shapes,
        scratch_shapes=[pltpu.VMEM((HALO + tm, D), f32)],
        compiler_params=_cparams(("arbitrary",)),
        name="pool_mixer_router",
    )(x2, yprev, modp, lnp, mod, wp, ps, ln, wr, br)


def _final_kernel(x_ref, y_ref, mod_ref, ln_ref, o_ref):
    tm, D = x_ref.shape
    y = _rows_load(y_ref, tm, Y_ROWS, D)
    o_ref[...] = _layer_norm(DN_ALPHA * x_ref[...] + (1.0 + mod_ref[5:6, :]) * y,
                             ln_ref[0:1, :], ln_ref[1:2, :])


def _final_merge(x2, y, mod, ln, S, tm):
    N, D = x2.shape
    spb = S // tm
    return pl.pallas_call(
        _final_kernel,
        grid=(N // tm,),
        in_specs=[
            pl.BlockSpec((tm, D), lambda i: (i, 0)),
            pl.BlockSpec((tm * Y_ROWS, LANES), lambda i: (i, 0)),
            pl.BlockSpec((None, 8, D), lambda i: (i // spb, 0, 0)),
            pl.BlockSpec(ln.shape, lambda i: (0, 0)),
        ],
        out_specs=pl.BlockSpec((tm, D), lambda i: (i, 0)),
        out_shape=jax.ShapeDtypeStruct((N, D), f32),
        compiler_params=_cparams(("arbitrary",)),
        name="final_merge",
    )(x2, y, mod, ln)


def _rank_kernel(cls_ref, pos_ref, cnt_ref, carry_ref, off_ref, earlier_ref):
    tr = cls_ref.shape[0]
    phase, tile = pl.program_id(0), pl.program_id(1)

    @pl.when(tile == 0)
    def _():
        @pl.when(phase == 0)
        def _():
            r = lax.broadcasted_iota(jnp.int32, (tr, tr), 0)
            c = lax.broadcasted_iota(jnp.int32, (tr, tr), 1)
            earlier_ref[...] = (c < r).astype(bf16)

        @pl.when(phase == 1)
        def _():
            tiles = jnp.floor((carry_ref[...] + (ROW_TILE - 1)) * (1.0 / ROW_TILE))
            hi = jnp.floor(tiles * (1.0 / 16.0))
            lo = tiles - 16.0 * hi
            before = (lax.broadcasted_iota(jnp.int32, (LANES, LANES), 0)
                      < lax.broadcasted_iota(jnp.int32, (LANES, LANES), 1)).astype(bf16)
            starts = 16.0 * _dot(hi.astype(bf16), before) + _dot(lo.astype(bf16), before)
            off_ref[...] = starts * ROW_TILE

        carry_ref[...] = jnp.zeros(carry_ref.shape, f32)

    lane = lax.broadcasted_iota(jnp.int32, (tr, LANES), 1)
    onehot = (cls_ref[...] == lane)
    oh = onehot.astype(bf16)
    total = carry_ref[0:1, :] + jnp.sum(oh.astype(f32), axis=0, keepdims=True)

    @pl.when(phase == 0)
    def _():
        cnt_ref[...] = jnp.broadcast_to(total, cnt_ref.shape).astype(jnp.int32)

    @pl.when(phase == 1)
    def _():
        ahead = _dot(earlier_ref[...], oh) + (carry_ref[0:1, :] + off_ref[0:1, :])
        pos_ref[...] = jnp.sum(jnp.where(onehot, ahead, 0.0), axis=-1, keepdims=True).astype(jnp.int32)

    carry_ref[...] = jnp.broadcast_to(total, carry_ref.shape)


def _class_slots(cls, tr):
    N = cls.shape[0]
    return pl.pallas_call(
        _rank_kernel,
        grid=(2, N // tr),
        in_specs=[pl.BlockSpec((tr, 1), lambda p, i: (i, 0))],
        out_specs=[
            pl.BlockSpec((tr, 1), lambda p, i: (i * p, 0)),
            pl.BlockSpec((SUBLANES, LANES), lambda p, i: (0, 0)),
        ],
        out_shape=[
            jax.ShapeDtypeStruct((N, 1), jnp.int32),
            jax.ShapeDtypeStruct((SUBLANES, LANES), jnp.int32),
        ],
        scratch_shapes=[pltpu.VMEM((SUBLANES, LANES), f32), pltpu.VMEM((SUBLANES, LANES), f32),
                        pltpu.VMEM((tr, tr), bf16)],
        compiler_params=_cparams(("arbitrary", "arbitrary")),
        name="class_slots",
    )(cls)


def _slot_token_kernel(pos_ref, inv_ref, *, chunk):
    base = pl.program_id(0) * chunk

    @pl.when(pl.program_id(0) == 0)
    def _():
        def clear(s, c):
            inv_ref[s] = 0
            return c
        lax.fori_loop(0, inv_ref.shape[0], clear, 0, unroll=32)

    def body(t, c):
        inv_ref[pos_ref[t]] = base + t
        return c

    lax.fori_loop(0, chunk, body, 0, unroll=16)


def _slot_tokens(pos, n_slots, chunk):
    N = pos.shape[0]
    return pl.pallas_call(
        functools.partial(_slot_token_kernel, chunk=chunk),
        grid=(N // chunk,),
        in_specs=[pl.BlockSpec((chunk,), lambda i: (i,), memory_space=pltpu.SMEM)],
        out_specs=pl.BlockSpec((n_slots,), lambda i: (0,), memory_space=pltpu.SMEM),
        out_shape=jax.ShapeDtypeStruct((n_slots,), jnp.int32),
        compiler_params=_cparams(("arbitrary",)),
        name="slot_tokens",
    )(pos)


def _moe_kernel(te1_ref, te2_ref, cnt_ref, inv_ref, hm_ref, wgu1_ref, wd1_ref, wgu2_ref, wd2_ref,
                y_ref, *scratch):
    hbufs = scratch[:MOE_SLOTS]
    ybufs = scratch[MOE_SLOTS:2 * MOE_SLOTS]
    gsem, ssem = scratch[2 * MOE_SLOTS:]
    ff, D = wd1_ref.shape
    i = pl.program_id(0)

    def count(tile):
        return jnp.where(tile >= 0, cnt_ref[jnp.maximum(tile, 0)], 0)

    n_cur, n_prev, n_ahead, n_old = count(i), count(i - 1), count(i + 2), count(i - MOE_SLOTS)

    def token_rows(ref, t, pitch):
        return ref.at[pl.ds(pl.multiple_of(t * pitch, pitch), pitch)]

    def start_gather(tok, r, hb, sem):
        pltpu.make_async_copy(token_rows(hm_ref, tok, HM_ROWS), token_rows(hb, r, HM_ROWS), sem).start()

    def start_scatter(tok, r, yb, sem):
        pltpu.make_async_copy(token_rows(yb, r, Y_ROWS), token_rows(y_ref, tok, Y_ROWS), sem).start()

    def wait_tokens(src, dst, sem, n, pitch):
        @pl.when(n > 0)
        def _():
            rows = pl.multiple_of(n * pitch, pitch)
            pltpu.make_async_copy(src.at[pl.ds(0, rows)], dst.at[pl.ds(0, rows)], sem).wait()

    @pl.when(i == 0)
    def _():
        for t in range(MOE_SLOTS - 1):
            def first(r, c):
                start_gather(inv_ref[t * ROW_TILE + r], r, hbufs[t], gsem.at[t])
                return c
            lax.fori_loop(0, cnt_ref[t], first, 0)

    def step(cur):
        far = (cur + MOE_SLOTS - 1) % MOE_SLOTS
        hb, yb = hbufs[cur], ybufs[cur]
        wait_tokens(hm_ref, hb, gsem.at[cur], n_cur, HM_ROWS)
        wait_tokens(yb, y_ref, ssem.at[cur], n_old, Y_ROWS)

        @pl.when((n_cur > 0) | (n_prev > 0))
        def _():
            ahead = (i + 2) * ROW_TILE
            prev = jnp.maximum(i - 1, 0) * ROW_TILE
            for r in range(ROW_TILE):
                tok_ahead = inv_ref[ahead + r]
                tok_prev = inv_ref[prev + r]

                @pl.when(r < n_ahead)
                def _():
                    start_gather(tok_ahead, r, hbufs[far], gsem.at[far])

                @pl.when(r < n_prev)
                def _():
                    start_scatter(tok_prev, r, ybufs[far], ssem.at[far])

            h = _rows_load(hb, ROW_TILE, HM_ROWS, D).astype(bf16)

            def hidden(gu, gate_row):
                w = hb[pl.ds(gate_row, ROW_TILE, stride=HM_ROWS), :]
                g, u = gu[:, :ff], gu[:, ff:]
                he = g * (1.0 / (1.0 + jnp.exp(-g))) * u
                return (he * jnp.concatenate([w] * (ff // LANES), axis=1)).astype(bf16)

            gu1 = _dot(h, wgu1_ref[...])
            gu2 = _dot(h, wgu2_ref[...])
            y = _dot(hidden(gu1, D // LANES), wd1_ref[...]) + _dot(hidden(gu2, D // LANES + 1), wd2_ref[...])
            _rows_store(yb, y, Y_ROWS)

    for cur in range(MOE_SLOTS):
        @pl.when(i % MOE_SLOTS == cur)
        def _():
            step(cur)


def _moe(te1, te2, cnt, inv, hm, wgu, wd, layer, n_steps):
    N = hm.shape[0] // HM_ROWS
    _, E, D, ff2 = wgu.shape
    ff = ff2 // 2

    def wmap(which):
        def f(i, te1, te2, cnt, inv):
            return (layer, (te1, te2)[which][i], 0, 0)
        return f

    return pl.pallas_call(
        _moe_kernel,
        grid_spec=pltpu.PrefetchScalarGridSpec(
            num_scalar_prefetch=4,
            grid=(n_steps,),
            in_specs=[
                pl.BlockSpec(memory_space=pl.ANY),
                pl.BlockSpec((None, None, D, ff2), wmap(0)),
                pl.BlockSpec((None, None, ff, D), wmap(0)),
                pl.BlockSpec((None, None, D, ff2), wmap(1)),
                pl.BlockSpec((None, None, ff, D), wmap(1)),
            ],
            out_specs=pl.BlockSpec(memory_space=pl.ANY),
            scratch_shapes=(
                [pltpu.VMEM((ROW_TILE * HM_ROWS, LANES), f32)] * MOE_SLOTS
                + [pltpu.VMEM((ROW_TILE * Y_ROWS, LANES), f32)] * MOE_SLOTS
                + [pltpu.SemaphoreType.DMA((MOE_SLOTS,)), pltpu.SemaphoreType.DMA((MOE_SLOTS,))]
            ),
        ),
        out_shape=jax.ShapeDtypeStruct((N * Y_ROWS, LANES), f32),
        compiler_params=_cparams(("arbitrary",)),
        name="pair_moe",
    )(te1, te2, cnt, inv, hm, wgu, wd, wgu, wd)


def _pair_tables():
    lo = np.zeros((N_CLASSES,), np.int32)
    hi = np.zeros((N_CLASSES,), np.int32)
    for g in range(MOE_GROUPS):
        p = 0
        for a in range(EXPERTS_PER_GROUP):
            for b in range(a + 1, EXPERTS_PER_GROUP):
                lo[g * PAIRS_PER_GROUP + p] = g * EXPERTS_PER_GROUP + a
                hi[g * PAIRS_PER_GROUP + p] = g * EXPERTS_PER_GROUP + b
                p += 1
    return lo, hi


def _moe_layer(hm, cls, expert_w, layer, chunk, tr):
    N = cls.shape[0]
    n_tiles = N // ROW_TILE + N_CLASSES
    n_steps = n_tiles + MOE_SLOTS
    pos, cnt = _class_slots(cls, tr)
    counts = cnt[0, :N_CLASSES]
    tiles = (counts + ROW_TILE - 1) // ROW_TILE
    tile_end = jnp.cumsum(tiles)
    tile_start = tile_end - tiles
    nused = tile_end[-1]
    step = jnp.arange(n_steps + 2, dtype=jnp.int32)
    tile_cls = jnp.sum(tile_end[None, :] <= jnp.minimum(step, nused - 1)[:, None], axis=1)
    tile_cls = jnp.minimum(tile_cls, N_CLASSES - 1).astype(jnp.int32)
    lo_tab, hi_tab = _pair_tables()
    te1 = jnp.asarray(lo_tab)[tile_cls]
    te2 = jnp.asarray(hi_tab)[tile_cls]
    in_cls = (step - tile_start[tile_cls]) * ROW_TILE
    tile_cnt = jnp.where(step < nused, jnp.clip(counts[tile_cls] - in_cls, 0, ROW_TILE), 0).astype(jnp.int32)

    n_slots = -(-(n_tiles + MOE_SLOTS) * ROW_TILE // SMEM_BLOCK) * SMEM_BLOCK
    inv = _slot_tokens(pos.reshape(N), n_slots, chunk)
    return _moe(te1, te2, tile_cnt, inv, hm, *expert_w, layer, n_steps)


def _swap_rope(w):
    half = QK_ROPE // 2
    return jnp.concatenate([-w[..., half:], w[..., :half]], axis=-1)


def _prep_mla_weights(w_in, w_uq, w_ukv):
    D = w_in.shape[0]
    w_kr = w_in[:, Q_LORA + KV_LORA:]
    w_ks = _swap_rope(w_kr)
    zeros = jnp.zeros((D, QK_NOPE), w_in.dtype)
    win = jnp.concatenate([w_in[:, :Q_LORA + KV_LORA], zeros, w_kr, w_kr, zeros, w_ks, w_ks], axis=1)
    uq = w_uq.reshape(Q_LORA, MLA_HEADS, QK_NOPE + QK_ROPE)
    rope = uq[..., QK_NOPE:]
    wuq = jnp.concatenate([uq[..., :QK_NOPE], rope, _swap_rope(rope)], axis=-1).reshape(Q_LORA, -1)
    ukv = w_ukv.reshape(KV_LORA, MLA_HEADS, QK_NOPE + V_HEAD)
    wkn = jnp.concatenate([ukv[..., :QK_NOPE], jnp.zeros((KV_LORA, MLA_HEADS, HEAD_W - QK_NOPE), w_ukv.dtype)],
                          axis=-1).reshape(KV_LORA, -1)
    wv = ukv[..., QK_NOPE:].reshape(KV_LORA, -1)
    return win.astype(bf16), wuq.T.astype(bf16), wkn.astype(bf16), wv.T.astype(bf16)


def _rope_inv_freq():
    inv = 1.0 / (ROPE_THETA ** (np.arange(0, QK_ROPE, 2, dtype=np.float32) / QK_ROPE))
    inv2 = np.concatenate([inv, inv]).astype(np.float32)
    return np.concatenate([np.zeros((QK_NOPE,), np.float32), inv2, inv2]).reshape(1, LANES)


def _prep_router(w_gr, b_gr, w_er, b_er):
    D = w_gr.shape[0]
    n = MOE_GROUPS + MOE_GROUPS * EXPERTS_PER_GROUP
    w = jnp.concatenate([w_gr, w_er, jnp.zeros((D, LANES - n), f32)], axis=1)
    b = jnp.concatenate([b_gr, b_er, jnp.zeros((LANES - n,), f32)]).reshape(1, LANES)
    hi, lo = _split_bf16(w)
    return jnp.concatenate([hi, lo], axis=1), b


def kernel(x, c, positions, w_mod, b_mod, ln_g, ln_b, w_in, q_norm_g, kv_norm_g, w_uq, w_ukv, w_o, w_pool,
           pool_scale, w_group_router, b_group_router, w_expert_router, b_expert_router, w_gate, w_up, w_down):
    B, S, D = x.shape
    N = B * S
    tm = min(512, S)
    tq = min(1024, S)
    tr = min(1024, N)
    chunk = min(2048, N)

    mod = _modulation(c, w_mod, b_mod)
    ln = jnp.stack([ln_g, ln_b], axis=2)
    x2 = x.reshape(N, D)

    win, wuqt, wkn, wvt = _prep_mla_weights(w_in[0], w_uq[0], w_ukv[0])
    qt, k, vt = _mla_proj(x2, positions.reshape(N, 1), mod[0], win, q_norm_g[0].reshape(1, -1),
                          kv_norm_g[0].reshape(1, -1), wuqt, wkn, wvt, jnp.asarray(_rope_inv_freq()),
                          B, S, tm, tq)
    o = _attention(qt, k, vt, tq)
    wr, br = _prep_router(w_group_router[0], b_group_router[0], w_expert_router[0], b_expert_router[0])
    x1, hm, cls = _attn_out(o.reshape(N, D), x2, mod[0], w_o[0].astype(bf16), ln[0, 0], wr, br, S, tm)
    wgu = jnp.stack([w_gate.astype(bf16), w_up.astype(bf16)], axis=-2)
    expert_w = (wgu.reshape(*w_gate.shape[:-1], 2 * w_gate.shape[-1]), w_down.astype(bf16))
    y0 = _moe_layer(hm, cls, expert_w, 0, chunk, tr)

    wr, br = _prep_router(w_group_router[1], b_group_router[1], w_expert_router[1], b_expert_router[1])
    x3, hm, cls = _pool_layer(x1, y0, mod[0], ln[0, 1], mod[1], w_pool[0].astype(bf16),
                              pool_scale[0].reshape(1, D), ln[1, 0], wr, br, S, tm)
    y1 = _moe_layer(hm, cls, expert_w, 1, chunk, tr)

    out = _final_merge(x3, y1, mod[1], ln[1, 1], S, tm)
    return out.reshape(B, S, D)
```

```python
import functools

import numpy as np
import jax
import jax.numpy as jnp
from jax import lax
from jax.experimental import pallas as pl
from jax.experimental.pallas import tpu as pltpu

MLA_HEADS = 16
Q_LORA = 384
KV_LORA = 256
QK_NOPE = 64
QK_ROPE = 32
V_HEAD = 64
ROPE_THETA = 10000.0
POOL_WINDOWS = (2, 4, 8, 16)
MOE_GROUPS = 4
EXPERTS_PER_GROUP = 8
DEPTH = 2
DN_ALPHA = (2.0 * DEPTH) ** 0.25
LN_EPS = 1e-5
RMS_EPS = 1e-6
LOG2E = 1.4426950408889634

LANES = 128
SUBLANES = 8
VMEM_LIMIT = 56 * 1024 * 1024

HEAD_W = 128
PAIRS_PER_GROUP = EXPERTS_PER_GROUP * (EXPERTS_PER_GROUP - 1) // 2
N_CLASSES = MOE_GROUPS * PAIRS_PER_GROUP
ROW_TILE = 128
HM_ROWS = 16
Y_ROWS = 8
MOE_SLOTS = 3
SMEM_BLOCK = 1024
HALO = 16

f32 = jnp.float32
bf16 = jnp.bfloat16


def _cparams(sem):
    return pltpu.CompilerParams(dimension_semantics=sem, vmem_limit_bytes=VMEM_LIMIT)


def _split_bf16(a):
    hi = a.astype(bf16)
    lo = (a - hi.astype(f32)).astype(bf16)
    return hi, lo


def _dot(a, b):
    return jnp.dot(a, b, preferred_element_type=f32)


def _dot3(a_hi, a_lo, b_hi, b_lo):
    return _dot(a_hi, b_hi) + (_dot(a_lo, b_hi) + _dot(a_hi, b_lo))


def _rows_load(ref, n, pitch, width):
    return jnp.concatenate([ref[pl.ds(s, n, stride=pitch), :] for s in range(width // LANES)], axis=1)


def _rows_store(ref, val, pitch):
    n, width = val.shape
    for s in range(width // LANES):
        ref[pl.ds(s, n, stride=pitch), :] = val[:, s * LANES:(s + 1) * LANES]


def _layer_norm(x, g, b):
    mu = jnp.mean(x, axis=-1, keepdims=True)
    xc = x - mu
    var = jnp.mean(xc * xc, axis=-1, keepdims=True)
    return xc * lax.rsqrt(var + LN_EPS) * g + b


def _mod_kernel(c_ref, w_ref, b_ref, o_ref):
    c = c_ref[...]
    ca = c * (1.0 / (1.0 + jnp.exp(-c)))
    a_hi, a_lo = _split_bf16(ca)
    w_hi, w_lo = _split_bf16(w_ref[...])
    o_ref[...] = _dot3(a_hi, a_lo, w_hi, w_lo) + b_ref[...]


def _modulation(c, w_mod, b_mod):
    B, D = c.shape
    depth = w_mod.shape[0]
    out = pl.pallas_call(
        _mod_kernel,
        grid=(depth, 6),
        in_specs=[
            pl.BlockSpec((B, D), lambda i, j: (0, 0)),
            pl.BlockSpec((None, D, D), lambda i, j: (i, 0, j)),
            pl.BlockSpec((None, None, 1, D), lambda i, j: (i, j, 0, 0)),
        ],
        out_specs=pl.BlockSpec((None, None, B, D), lambda i, j: (i, j, 0, 0)),
        out_shape=jax.ShapeDtypeStruct((depth, 6, B, D), f32),
        compiler_params=_cparams(("arbitrary", "arbitrary")),
        name="adaln_mod",
    )(c, w_mod, b_mod.reshape(depth, 6, 1, D))
    out = jnp.transpose(out, (0, 2, 1, 3))
    return jnp.pad(out, ((0, 0), (0, 0), (0, 2), (0, 0)))


def _mla_proj_kernel(x_ref, pos_ref, mod_ref, win_ref, qg_ref, kg_ref, wuqt_ref, wkn_ref, wvt_ref,
                     invf_ref, qt_ref, k_ref, vt_ref):
    x = x_ref[...]
    h = x * (1.0 + mod_ref[1:2, :]) + mod_ref[0:1, :]
    z = _dot(h.astype(bf16), win_ref[...])
    zq = z[:, :Q_LORA]
    zkv = z[:, Q_LORA:Q_LORA + KV_LORA]
    za = z[:, Q_LORA + KV_LORA:Q_LORA + KV_LORA + LANES]
    zb = z[:, Q_LORA + KV_LORA + LANES:]
    q_lat = zq * lax.rsqrt(jnp.mean(zq * zq, axis=-1, keepdims=True) + RMS_EPS) * qg_ref[...]
    kv_lat = zkv * lax.rsqrt(jnp.mean(zkv * zkv, axis=-1, keepdims=True) + RMS_EPS) * kg_ref[...]

    ang = pos_ref[...].astype(f32) * invf_ref[...]
    cs = jnp.cos(ang)
    sn = jnp.sin(ang)
    lane = lax.broadcasted_iota(jnp.int32, cs.shape, 1)
    scale = (QK_NOPE + QK_ROPE) ** -0.5 * LOG2E
    tqn = scale * jnp.where(lane < QK_NOPE + QK_ROPE, cs, sn)
    kr = za * cs + zb * sn

    kn = _dot(kv_lat.astype(bf16), wkn_ref[...])
    for hd in range(MLA_HEADS):
        k_ref[hd] = (kn[:, hd * HEAD_W:(hd + 1) * HEAD_W] + kr).astype(bf16)

    tqt = tqn.T
    qt = _dot(wuqt_ref[...], q_lat.T.astype(bf16))
    for hd in range(MLA_HEADS):
        qt_ref[hd] = (qt[hd * HEAD_W:(hd + 1) * HEAD_W, :] * tqt).astype(bf16)
    vt_ref[...] = _dot(wvt_ref[...], kv_lat.T.astype(bf16)).astype(bf16)


def _mla_proj(x2, pos2, mod, win, qg, kg, wuqt, wkn, wvt, invf, B, S, tm, tq):
    N, D = x2.shape
    spb = S // tm
    per_q = tq // tm
    const = lambda shape: pl.BlockSpec(shape, lambda i: (0,) * len(shape))
    nq = S // tq
    return pl.pallas_call(
        _mla_proj_kernel,
        grid=(N // tm,),
        in_specs=[
            pl.BlockSpec((tm, D), lambda i: (i, 0)),
            pl.BlockSpec((tm, 1), lambda i: (i, 0)),
            pl.BlockSpec((None, 8, D), lambda i: (i // spb, 0, 0)),
            const(win.shape), const(qg.shape), const(kg.shape), const(wuqt.shape), const(wkn.shape),
            const(wvt.shape), const(invf.shape),
        ],
        out_specs=[
            pl.BlockSpec((None, MLA_HEADS, None, HEAD_W, tm),
                         lambda i: (i // spb, 0, (i % spb) // per_q, 0, i % per_q)),
            pl.BlockSpec((None, MLA_HEADS, tm, HEAD_W), lambda i: (i // spb, 0, i % spb, 0)),
            pl.BlockSpec((None, None, MLA_HEADS * V_HEAD, tm),
                         lambda i: (i // spb, (i % spb) // per_q, 0, i % per_q)),
        ],
        out_shape=[
            jax.ShapeDtypeStruct((B, MLA_HEADS, nq, HEAD_W, tq), bf16),
            jax.ShapeDtypeStruct((B, MLA_HEADS, S, HEAD_W), bf16),
            jax.ShapeDtypeStruct((B, nq, MLA_HEADS * V_HEAD, tq), bf16),
        ],
        compiler_params=_cparams(("arbitrary",)),
        name="mla_proj",
    )(x2, pos2, mod, win, qg, kg, wuqt, wkn, wvt, invf)


ONES_ROWS = 16


def _attn_kernel(qt_ref, k_ref, vt_ref, o_ref, vx_ref, m_ref, acc_ref, ot_ref, *, tq):
    nq = qt_ref.shape[1]
    for hh in range(2):
        for j in range(nq):
            vx_ref[hh, j, :V_HEAD, :] = vt_ref[j, hh * V_HEAD:(hh + 1) * V_HEAD, :]
            vx_ref[hh, j, V_HEAD:, :] = jnp.ones((ONES_ROWS, tq), bf16)

    def kv_step(qi, j):
        scores = [_dot(k_ref[hh, pl.ds(j * tq, tq), :], qt_ref[hh, qi]) for hh in range(2)]
        for hh in range(2):
            update(hh, 0, tq, [(scores[hh], vx_ref[hh, j])])

    def update(hh, lo, hi, parts):
        m_old = m_ref[hh, :, lo:hi]
        m_new = m_old
        for s, _ in parts:
            m_new = jnp.maximum(m_new, jnp.max(s, axis=0, keepdims=True))
        alpha = jnp.exp2(m_old - m_new)
        pv = sum(_dot(vx, jnp.exp2(s - m_new[0:1, :]).astype(bf16)) for s, vx in parts)
        acc_ref[hh, :, lo:hi] = acc_ref[hh, :, lo:hi] * alpha[0:1, :] + pv
        m_ref[hh, :, lo:hi] = m_new

    def diag_step(qi):
        half = tq // 2
        k0 = qi * tq
        tops = [_dot(k_ref[hh, pl.ds(k0, half), :], qt_ref[hh, qi]) for hh in range(2)]
        bots = [_dot(k_ref[hh, pl.ds(k0 + half, half), :], qt_ref[hh, qi, :, half:]) for hh in range(2)]
        key = lax.broadcasted_iota(jnp.int32, (half, half), 0)
        qry = lax.broadcasted_iota(jnp.int32, (half, half), 1)
        visible = key <= qry
        for hh in range(2):
            vx = vx_ref[hh, qi]
            early = jnp.where(visible, tops[hh][:, :half], -jnp.inf)
            late = jnp.where(visible, bots[hh], -jnp.inf)
            update(hh, 0, half, [(early, vx[:, :half])])
            update(hh, half, tq, [(tops[hh][:, half:], vx[:, :half]), (late, vx[:, half:])])

    def q_body(qi, carry):
        m_ref[...] = jnp.full(m_ref.shape, -jnp.inf, f32)
        acc_ref[...] = jnp.zeros(acc_ref.shape, f32)

        for j in range(qi):
            kv_step(qi, j)
        diag_step(qi)
        for hh in range(2):
            acc = acc_ref[hh]
            ot_ref[hh * V_HEAD:(hh + 1) * V_HEAD, :] = acc[:V_HEAD, :] * (1.0 / acc[V_HEAD:V_HEAD + 1, :])
        o_ref[pl.ds(qi * tq, tq), :] = ot_ref[...].T.astype(bf16)
        return carry

    for qi in range(nq):
        q_body(qi, 0)


def _attention(qt, k, vt, tq):
    B, H, nq, W, _ = qt.shape
    S = nq * tq
    return pl.pallas_call(
        functools.partial(_attn_kernel, tq=tq),
        grid=(B, H // 2),
        in_specs=[
            pl.BlockSpec((None, 2, nq, W, tq), lambda b, p: (b, p, 0, 0, 0)),
            pl.BlockSpec((None, 2, S, W), lambda b, p: (b, p, 0, 0)),
            pl.BlockSpec((None, nq, 2 * V_HEAD, tq), lambda b, p: (b, 0, p, 0)),
        ],
        out_specs=pl.BlockSpec((None, S, LANES), lambda b, p: (b, 0, p)),
        out_shape=jax.ShapeDtypeStruct((B, S, H * V_HEAD), bf16),
        scratch_shapes=[
            pltpu.VMEM((2, nq, V_HEAD + ONES_ROWS, tq), bf16),
            pltpu.VMEM((2, SUBLANES, tq), f32),
            pltpu.VMEM((2, V_HEAD + ONES_ROWS, tq), f32),
            pltpu.VMEM((2 * V_HEAD, tq), f32),
        ],
        compiler_params=_cparams(("arbitrary", "arbitrary")),
        name="mla_attention",
    )(qt, k, vt)


def _post_mixer(x, y, mod_ref, ln_ref, wr_ref, br_ref, xo_ref, hm_ref, cls_ref, hist_ref):
    tm, D = x.shape
    x1 = _layer_norm(DN_ALPHA * x + (1.0 + mod_ref[2:3, :]) * y, ln_ref[0:1, :], ln_ref[1:2, :])
    xo_ref[...] = x1
    h2 = x1 * (1.0 + mod_ref[4:5, :]) + mod_ref[3:4, :]
    h_hi, h_lo = _split_bf16(h2)
    wr = wr_ref[...]
    t_hi = _dot(h_hi, wr)
    t_lo = _dot(h_lo, wr)
    logits = t_hi[:, :LANES] + (t_hi[:, LANES:] + t_lo[:, :LANES]) + br_ref[...]

    lane = lax.broadcasted_iota(jnp.int32, logits.shape, 1).astype(f32)
    neg = -jnp.inf
    far = float(LANES)
    is_g = lane < MOE_GROUPS
    gl = jnp.where(is_g, logits, neg)
    gmax = jnp.max(gl, axis=-1, keepdims=True)
    gidx = jnp.min(jnp.where(gl == gmax, lane, far), axis=-1, keepdims=True)
    g_p = 1.0 / jnp.sum(jnp.where(is_g, jnp.exp(logits - gmax), 0.0), axis=-1, keepdims=True)
    base = MOE_GROUPS + EXPERTS_PER_GROUP * gidx
    el = jnp.where((lane >= base) & (lane < base + EXPERTS_PER_GROUP), logits, neg)
    t1 = jnp.max(el, axis=-1, keepdims=True)
    i1 = jnp.min(jnp.where(el == t1, lane, far), axis=-1, keepdims=True)
    el2 = jnp.where(lane == i1, neg, el)
    t2 = jnp.max(el2, axis=-1, keepdims=True)
    i2 = jnp.min(jnp.where(el2 == t2, lane, far), axis=-1, keepdims=True)
    e = jnp.exp(t2 - t1)
    w1 = g_p / (1.0 + e)
    w2 = g_p * e / (1.0 + e)
    a = i1 - base
    b = i2 - base
    a_first = a < b
    lo = jnp.where(a_first, a, b)
    hi = jnp.where(a_first, b, a)
    w_lo = jnp.where(a_first, w1, w2)
    w_hi = jnp.where(a_first, w2, w1)
    pair = lo * (2 * EXPERTS_PER_GROUP - 1 - lo) * 0.5 + (hi - lo - 1.0)
    cls = gidx * PAIRS_PER_GROUP + pair
    cls_ref[...] = cls.astype(jnp.int32)

    @pl.when(pl.program_id(0) == 0)
    def _():
        hist_ref[...] = jnp.zeros(hist_ref.shape, f32)

    hist_ref[...] += jnp.sum(jnp.where(lane == cls, 1.0, 0.0), axis=0, keepdims=True)

    _rows_store(hm_ref, h2, HM_ROWS)
    hm_ref.reshape(tm, HM_ROWS, LANES)[:, D // LANES:, :] = jnp.zeros((tm, HM_ROWS - D // LANES, LANES), f32)
    hm_ref[pl.ds(D // LANES, tm, stride=HM_ROWS), :] = jnp.broadcast_to(w_lo, (tm, LANES))
    hm_ref[pl.ds(D // LANES + 1, tm, stride=HM_ROWS), :] = jnp.broadcast_to(w_hi, (tm, LANES))


def _attn_out_kernel(o_ref, x_ref, mod_ref, wo_ref, ln_ref, wr_ref, br_ref,
                     xo_ref, hm_ref, cls_ref, hist_ref):
    y = _dot(o_ref[...], wo_ref[...])
    _post_mixer(x_ref[...], y, mod_ref, ln_ref, wr_ref, br_ref, xo_ref, hm_ref, cls_ref, hist_ref)


def _post_out_specs(N, D, tm):
    specs = [
        pl.BlockSpec((tm, D), lambda i: (i, 0)),
        pl.BlockSpec((tm * HM_ROWS, LANES), lambda i: (i, 0)),
        pl.BlockSpec((tm, 1), lambda i: (i, 0)),
        pl.BlockSpec((SUBLANES, LANES), lambda i: (0, 0)),
    ]
    shapes = [
        jax.ShapeDtypeStruct((N, D), f32),
        jax.ShapeDtypeStruct((N * HM_ROWS, LANES), f32),
        jax.ShapeDtypeStruct((N, 1), jnp.int32),
        jax.ShapeDtypeStruct((SUBLANES, LANES), f32),
    ]
    return specs, shapes


def _attn_out(o2, x2, mod, wo, ln, wr, br, S, tm):
    N, D = x2.shape
    spb = S // tm
    const = lambda shape: pl.BlockSpec(shape, lambda i: (0,) * len(shape))
    out_specs, out_shapes = _post_out_specs(N, D, tm)
    return pl.pallas_call(
        _attn_out_kernel,
        grid=(N // tm,),
        in_specs=[
            pl.BlockSpec((tm, D), lambda i: (i, 0)),
            pl.BlockSpec((tm, D), lambda i: (i, 0)),
            pl.BlockSpec((None, 8, D), lambda i: (i // spb, 0, 0)),
            const(wo.shape), const(ln.shape), const(wr.shape), const(br.shape),
        ],
        out_specs=out_specs,
        out_shape=out_shapes,
        compiler_params=_cparams(("arbitrary",)),
        name="attn_out_router",
    )(o2, x2, mod, wo, ln, wr, br)


def _pool_kernel(x_ref, y_ref, modp_ref, lnp_ref, mod_ref, wp_ref, ps_ref, ln_ref,
                 wr_ref, br_ref, xo_ref, hm_ref, cls_ref, hist_ref, hb_ref, *, spb):
    tm, D = x_ref.shape
    i = pl.program_id(0)
    t_blk = i % spb
    yprev = _rows_load(y_ref, tm, Y_ROWS, D)
    x2 = _layer_norm(DN_ALPHA * x_ref[...] + (1.0 + modp_ref[5:6, :]) * yprev,
                     lnp_ref[0:1, :], lnp_ref[1:2, :])
    h = x2 * (1.0 + mod_ref[1:2, :]) + mod_ref[0:1, :]

    @pl.when(t_blk == 0)
    def _():
        hb_ref[0:HALO, :] = jnp.zeros((HALO, D), f32)

    hb_ref[HALO:, :] = h
    t_seq = t_blk * tm + lax.broadcasted_iota(jnp.int32, (tm, 1), 0)
    gd = D // len(POOL_WINDOWS)
    ys = []
    for gi, w in enumerate(POOL_WINDOWS):
        c0, c1 = gi * gd, (gi + 1) * gd
        a = hb_ref[:, c0:c1]
        span = 1
        while span < w:
            a = a[span:, :] + a[:-span, :]
            span *= 2
        tsum = a[HALO - (w - 1):, :]
        cnt = jnp.minimum(t_seq + 1, w).astype(f32)
        mixed = tsum / cnt - h[:, c0:c1]
        ys.append(_dot(mixed.astype(bf16), wp_ref[gi]))
    hb_ref[0:HALO, :] = h[tm - HALO:, :]
    y = jnp.concatenate(ys, axis=1) * ps_ref[...]
    _post_mixer(x2, y, mod_ref, ln_ref, wr_ref, br_ref, xo_ref, hm_ref, cls_ref, hist_ref)


def _pool_layer(x2, yprev, modp, lnp, mod, wp, ps, ln, wr, br, S, tm):
    N, D = x2.shape
    spb = S // tm
    const = lambda shape: pl.BlockSpec(shape, lambda i: (0,) * len(shape))
    out_specs, out_shapes = _post_out_specs(N, D, tm)
    return pl.pallas_call(
        functools.partial(_pool_kernel, spb=spb),
        grid=(N // tm,),
        in_specs=[
            pl.BlockSpec((tm, D), lambda i: (i, 0)),
            pl.BlockSpec((tm * Y_ROWS, LANES), lambda i: (i, 0)),
            pl.BlockSpec((None, 8, D), lambda i: (i // spb, 0, 0)),
            const(lnp.shape),
            pl.BlockSpec((None, 8, D), lambda i: (i // spb, 0, 0)),
            const(wp.shape), const(ps.shape), const(ln.shape),
            const(wr.shape), const(br.shape),
        ],
        out_specs=out_specs,
        out_shape=out_shapes,
        scratch_shapes=[pltpu.VMEM((HALO + tm, D), f32)],
        compiler_params=_cparams(("arbitrary",)),
        name="pool_mixer_router",
    )(x2, yprev, modp, lnp, mod, wp, ps, ln, wr, br)


def _final_kernel(x_ref, y_ref, mod_ref, ln_ref, o_ref):
    tm, D = x_ref.shape
    y = _rows_load(y_ref, tm, Y_ROWS, D)
    o_ref[...] = _layer_norm(DN_ALPHA * x_ref[...] + (1.0 + mod_ref[5:6, :]) * y,
                             ln_ref[0:1, :], ln_ref[1:2, :])


def _final_merge(x2, y, mod, ln, S, tm):
    N, D = x2.shape
    spb = S // tm
    return pl.pallas_call(
        _final_kernel,
        grid=(N // tm,),
        in_specs=[
            pl.BlockSpec((tm, D), lambda i: (i, 0)),
            pl.BlockSpec((tm * Y_ROWS, LANES), lambda i: (i, 0)),
            pl.BlockSpec((None, 8, D), lambda i: (i // spb, 0, 0)),
            pl.BlockSpec(ln.shape, lambda i: (0, 0)),
        ],
        out_specs=pl.BlockSpec((tm, D), lambda i: (i, 0)),
        out_shape=jax.ShapeDtypeStruct((N, D), f32),
        compiler_params=_cparams(("arbitrary",)),
        name="final_merge",
    )(x2, y, mod, ln)


def _rank_kernel(cls_ref, hist_ref, pos_ref, carry_ref, off_ref, earlier_ref):
    tr = cls_ref.shape[0]

    @pl.when(pl.program_id(0) == 0)
    def _():
        r = lax.broadcasted_iota(jnp.int32, (tr, tr), 0)
        c = lax.broadcasted_iota(jnp.int32, (tr, tr), 1)
        earlier_ref[...] = (c < r).astype(bf16)
        tiles = jnp.floor((hist_ref[...] + (ROW_TILE - 1)) * (1.0 / ROW_TILE))
        hi = jnp.floor(tiles * (1.0 / 16.0))
        lo = tiles - 16.0 * hi
        before = (lax.broadcasted_iota(jnp.int32, (LANES, LANES), 0)
                  < lax.broadcasted_iota(jnp.int32, (LANES, LANES), 1)).astype(bf16)
        starts = 16.0 * _dot(hi.astype(bf16), before) + _dot(lo.astype(bf16), before)
        off_ref[...] = starts * ROW_TILE
        carry_ref[...] = jnp.zeros(carry_ref.shape, f32)

    lane = lax.broadcasted_iota(jnp.int32, (tr, LANES), 1)
    onehot = (cls_ref[...] == lane)
    oh = onehot.astype(bf16)
    ahead = _dot(earlier_ref[...], oh) + (carry_ref[0:1, :] + off_ref[0:1, :])
    pos_ref[...] = jnp.sum(jnp.where(onehot, ahead, 0.0), axis=-1, keepdims=True).astype(jnp.int32)
    total = carry_ref[0:1, :] + jnp.sum(oh.astype(f32), axis=0, keepdims=True)
    carry_ref[...] = jnp.broadcast_to(total, carry_ref.shape)


def _class_slots(cls, hist, tr):
    N = cls.shape[0]
    return pl.pallas_call(
        _rank_kernel,
        grid=(N // tr,),
        in_specs=[pl.BlockSpec((tr, 1), lambda i: (i, 0)), pl.BlockSpec((SUBLANES, LANES), lambda i: (0, 0))],
        out_specs=pl.BlockSpec((tr, 1), lambda i: (i, 0)),
        out_shape=jax.ShapeDtypeStruct((N, 1), jnp.int32),
        scratch_shapes=[pltpu.VMEM((SUBLANES, LANES), f32), pltpu.VMEM((SUBLANES, LANES), f32),
                        pltpu.VMEM((tr, tr), bf16)],
        compiler_params=_cparams(("arbitrary",)),
        name="class_slots",
    )(cls, hist)


def _slot_token_kernel(pos_ref, inv_ref, *, chunk):
    base = pl.program_id(0) * chunk

    @pl.when(pl.program_id(0) == 0)
    def _():
        def clear(s, c):
            inv_ref[s] = 0
            return c
        lax.fori_loop(0, inv_ref.shape[0], clear, 0, unroll=32)

    def body(t, c):
        inv_ref[pos_ref[t]] = base + t
        return c

    lax.fori_loop(0, chunk, body, 0, unroll=16)


def _slot_tokens(pos, n_slots, chunk):
    N = pos.shape[0]
    return pl.pallas_call(
        functools.partial(_slot_token_kernel, chunk=chunk),
        grid=(N // chunk,),
        in_specs=[pl.BlockSpec((chunk,), lambda i: (i,), memory_space=pltpu.SMEM)],
        out_specs=pl.BlockSpec((n_slots,), lambda i: (0,), memory_space=pltpu.SMEM),
        out_shape=jax.ShapeDtypeStruct((n_slots,), jnp.int32),
        compiler_params=_cparams(("arbitrary",)),
        name="slot_tokens",
    )(pos)


def _moe_kernel(te1_ref, te2_ref, cnt_ref, inv_ref, hm_ref, wgu1_ref, wd1_ref, wgu2_ref, wd2_ref,
                y_ref, *scratch):
    hbufs = scratch[:MOE_SLOTS]
    ybufs = scratch[MOE_SLOTS:2 * MOE_SLOTS]
    gsem, ssem = scratch[2 * MOE_SLOTS:]
    ff, D = wd1_ref.shape
    i = pl.program_id(0)

    def count(tile):
        return jnp.where(tile >= 0, cnt_ref[jnp.maximum(tile, 0)], 0)

    n_cur, n_prev, n_ahead, n_old = count(i), count(i - 1), count(i + 2), count(i - MOE_SLOTS)

    def token_rows(ref, t, pitch):
        return ref.at[pl.ds(pl.multiple_of(t * pitch, pitch), pitch)]

    def start_gather(tok, r, hb, sem):
        pltpu.make_async_copy(token_rows(hm_ref, tok, HM_ROWS), token_rows(hb, r, HM_ROWS), sem).start()

    def start_scatter(tok, r, yb, sem):
        pltpu.make_async_copy(token_rows(yb, r, Y_ROWS), token_rows(y_ref, tok, Y_ROWS), sem).start()

    def wait_tokens(src, dst, sem, n, pitch):
        @pl.when(n > 0)
        def _():
            rows = pl.multiple_of(n * pitch, pitch)
            pltpu.make_async_copy(src.at[pl.ds(0, rows)], dst.at[pl.ds(0, rows)], sem).wait()

    @pl.when(i == 0)
    def _():
        for t in range(MOE_SLOTS - 1):
            def first(r, c):
                start_gather(inv_ref[t * ROW_TILE + r], r, hbufs[t], gsem.at[t])
                return c
            lax.fori_loop(0, cnt_ref[t], first, 0)

    def step(cur):
        far = (cur + MOE_SLOTS - 1) % MOE_SLOTS
        hb, yb = hbufs[cur], ybufs[cur]
        wait_tokens(hm_ref, hb, gsem.at[cur], n_cur, HM_ROWS)
        wait_tokens(yb, y_ref, ssem.at[cur], n_old, Y_ROWS)

        @pl.when((n_cur > 0) | (n_prev > 0))
        def _():
            ahead = (i + 2) * ROW_TILE
            prev = jnp.maximum(i - 1, 0) * ROW_TILE
            for r in range(ROW_TILE):
                tok_ahead = inv_ref[ahead + r]
                tok_prev = inv_ref[prev + r]

                @pl.when(r < n_ahead)
                def _():
                    start_gather(tok_ahead, r, hbufs[far], gsem.at[far])

                @pl.when(r < n_prev)
                def _():
                    start_scatter(tok_prev, r, ybufs[far], ssem.at[far])

            h = _rows_load(hb, ROW_TILE, HM_ROWS, D).astype(bf16)

            def hidden(gu, gate_row):
                w = hb[pl.ds(gate_row, ROW_TILE, stride=HM_ROWS), :]
                g, u = gu[:, :ff], gu[:, ff:]
                he = g * (1.0 / (1.0 + jnp.exp(-g))) * u
                return (he * jnp.concatenate([w] * (ff // LANES), axis=1)).astype(bf16)

            gu1 = _dot(h, wgu1_ref[...])
            gu2 = _dot(h, wgu2_ref[...])
            y = _dot(hidden(gu1, D // LANES), wd1_ref[...]) + _dot(hidden(gu2, D // LANES + 1), wd2_ref[...])
            _rows_store(yb, y, Y_ROWS)

    for cur in range(MOE_SLOTS):
        @pl.when(i % MOE_SLOTS == cur)
        def _():
            step(cur)


def _moe(te1, te2, cnt, inv, hm, wgu, wd, layer, n_steps):
    N = hm.shape[0] // HM_ROWS
    _, E, D, ff2 = wgu.shape
    ff = ff2 // 2

    def wmap(which):
        def f(i, te1, te2, cnt, inv):
            return (layer, (te1, te2)[which][i], 0, 0)
        return f

    return pl.pallas_call(
        _moe_kernel,
        grid_spec=pltpu.PrefetchScalarGridSpec(
            num_scalar_prefetch=4,
            grid=(n_steps,),
            in_specs=[
                pl.BlockSpec(memory_space=pl.ANY),
                pl.BlockSpec((None, None, D, ff2), wmap(0)),
                pl.BlockSpec((None, None, ff, D), wmap(0)),
                pl.BlockSpec((None, None, D, ff2), wmap(1)),
                pl.BlockSpec((None, None, ff, D), wmap(1)),
            ],
            out_specs=pl.BlockSpec(memory_space=pl.ANY),
            scratch_shapes=(
                [pltpu.VMEM((ROW_TILE * HM_ROWS, LANES), f32)] * MOE_SLOTS
                + [pltpu.VMEM((ROW_TILE * Y_ROWS, LANES), f32)] * MOE_SLOTS
                + [pltpu.SemaphoreType.DMA((MOE_SLOTS,)), pltpu.SemaphoreType.DMA((MOE_SLOTS,))]
            ),
        ),
        out_shape=jax.ShapeDtypeStruct((N * Y_ROWS, LANES), f32),
        compiler_params=_cparams(("arbitrary",)),
        name="pair_moe",
    )(te1, te2, cnt, inv, hm, wgu, wd, wgu, wd)


def _pair_tables():
    lo = np.zeros((N_CLASSES,), np.int32)
    hi = np.zeros((N_CLASSES,), np.int32)
    for g in range(MOE_GROUPS):
        p = 0
        for a in range(EXPERTS_PER_GROUP):
            for b in range(a + 1, EXPERTS_PER_GROUP):
                lo[g * PAIRS_PER_GROUP + p] = g * EXPERTS_PER_GROUP + a
                hi[g * PAIRS_PER_GROUP + p] = g * EXPERTS_PER_GROUP + b
                p += 1
    return lo, hi


def _moe_layer(hm, cls, hist, expert_w, layer, chunk, tr):
    N = cls.shape[0]
    n_tiles = N // ROW_TILE + N_CLASSES
    n_steps = n_tiles + MOE_SLOTS
    pos = _class_slots(cls, hist, tr)
    counts = hist[0, :N_CLASSES].astype(jnp.int32)
    tiles = (counts + ROW_TILE - 1) // ROW_TILE
    tile_end = jnp.cumsum(tiles)
    tile_start = tile_end - tiles
    nused = tile_end[-1]
    step = jnp.arange(n_steps + 2, dtype=jnp.int32)
    tile_cls = jnp.sum(tile_end[None, :] <= jnp.minimum(step, nused - 1)[:, None], axis=1)
    tile_cls = jnp.minimum(tile_cls, N_CLASSES - 1).astype(jnp.int32)
    lo_tab, hi_tab = _pair_tables()
    te1 = jnp.asarray(lo_tab)[tile_cls]
    te2 = jnp.asarray(hi_tab)[tile_cls]
    in_cls = (step - tile_start[tile_cls]) * ROW_TILE
    tile_cnt = jnp.where(step < nused, jnp.clip(counts[tile_cls] - in_cls, 0, ROW_TILE), 0).astype(jnp.int32)

    n_slots = -(-(n_tiles + MOE_SLOTS) * ROW_TILE // SMEM_BLOCK) * SMEM_BLOCK
    inv = _slot_tokens(pos.reshape(N), n_slots, chunk)
    return _moe(te1, te2, tile_cnt, inv, hm, *expert_w, layer, n_steps)


def _swap_rope(w):
    half = QK_ROPE // 2
    return jnp.concatenate([-w[..., half:], w[..., :half]], axis=-1)


def _prep_mla_weights(w_in, w_uq, w_ukv):
    D = w_in.shape[0]
    w_kr = w_in[:, Q_LORA + KV_LORA:]
    w_ks = _swap_rope(w_kr)
    zeros = jnp.zeros((D, QK_NOPE), w_in.dtype)
    win = jnp.concatenate([w_in[:, :Q_LORA + KV_LORA], zeros, w_kr, w_kr, zeros, w_ks, w_ks], axis=1)
    uq = w_uq.reshape(Q_LORA, MLA_HEADS, QK_NOPE + QK_ROPE)
    rope = uq[..., QK_NOPE:]
    wuq = jnp.concatenate([uq[..., :QK_NOPE], rope, _swap_rope(rope)], axis=-1).reshape(Q_LORA, -1)
    ukv = w_ukv.reshape(KV_LORA, MLA_HEADS, QK_NOPE + V_HEAD)
    wkn = jnp.concatenate([ukv[..., :QK_NOPE], jnp.zeros((KV_LORA, MLA_HEADS, HEAD_W - QK_NOPE), w_ukv.dtype)],
                          axis=-1).reshape(KV_LORA, -1)
    wv = ukv[..., QK_NOPE:].reshape(KV_LORA, -1)
    return win.astype(bf16), wuq.T.astype(bf16), wkn.astype(bf16), wv.T.astype(bf16)


def _rope_inv_freq():
    inv = 1.0 / (ROPE_THETA ** (np.arange(0, QK_ROPE, 2, dtype=np.float32) / QK_ROPE))
    inv2 = np.concatenate([inv, inv]).astype(np.float32)
    return np.concatenate([np.zeros((QK_NOPE,), np.float32), inv2, inv2]).reshape(1, LANES)


def _prep_router(w_gr, b_gr, w_er, b_er):
    D = w_gr.shape[0]
    n = MOE_GROUPS + MOE_GROUPS * EXPERTS_PER_GROUP
    w = jnp.concatenate([w_gr, w_er, jnp.zeros((D, LANES - n), f32)], axis=1)
    b = jnp.concatenate([b_gr, b_er, jnp.zeros((LANES - n,), f32)]).reshape(1, LANES)
    hi, lo = _split_bf16(w)
    return jnp.concatenate([hi, lo], axis=1), b


def kernel(x, c, positions, w_mod, b_mod, ln_g, ln_b, w_in, q_norm_g, kv_norm_g, w_uq, w_ukv, w_o, w_pool,
           pool_scale, w_group_router, b_group_router, w_expert_router, b_expert_router, w_gate, w_up, w_down):
    B, S, D = x.shape
    N = B * S
    tm = min(512, S)
    tq = min(1024, S)
    tr = min(1024, N)
    chunk = min(2048, N)

    mod = _modulation(c, w_mod, b_mod)
    ln = jnp.stack([ln_g, ln_b], axis=2)
    x2 = x.reshape(N, D)

    win, wuqt, wkn, wvt = _prep_mla_weights(w_in[0], w_uq[0], w_ukv[0])
    qt, k, vt = _mla_proj(x2, positions.reshape(N, 1), mod[0], win, q_norm_g[0].reshape(1, -1),
                          kv_norm_g[0].reshape(1, -1), wuqt, wkn, wvt, jnp.asarray(_rope_inv_freq()),
                          B, S, tm, tq)
    o = _attention(qt, k, vt, tq)
    wr, br = _prep_router(w_group_router[0], b_group_router[0], w_expert_router[0], b_expert_router[0])
    x1, hm, cls, hist = _attn_out(o.reshape(N, D), x2, mod[0], w_o[0].astype(bf16), ln[0, 0], wr, br, S, tm)
    expert_w = (jnp.concatenate([w_gate, w_up], axis=-1).astype(bf16), w_down.astype(bf16))
    y0 = _moe_layer(hm, cls, hist, expert_w, 0, chunk, tr)

    wr, br = _prep_router(w_group_router[1], b_group_router[1], w_expert_router[1], b_expert_router[1])
    x3, hm, cls, hist = _pool_layer(x1, y0, mod[0], ln[0, 1], mod[1], w_pool[0].astype(bf16),
                              pool_scale[0].reshape(1, D), ln[1, 0], wr, br, S, tm)
    y1 = _moe_layer(hm, cls, hist, expert_w, 1, chunk, tr)

    out = _final_merge(x3, y1, mod[1], ln[1, 1], S, tm)
    return out.reshape(B, S, D)
```

```python
import functools

import numpy as np
import jax
import jax.numpy as jnp
from jax import lax
from jax.experimental import pallas as pl
from jax.experimental.pallas import tpu as pltpu

MLA_HEADS = 16
Q_LORA = 384
KV_LORA = 256
QK_NOPE = 64
QK_ROPE = 32
V_HEAD = 64
ROPE_THETA = 10000.0
POOL_WINDOWS = (2, 4, 8, 16)
MOE_GROUPS = 4
EXPERTS_PER_GROUP = 8
DEPTH = 2
DN_ALPHA = (2.0 * DEPTH) ** 0.25
LN_EPS = 1e-5
RMS_EPS = 1e-6
LOG2E = 1.4426950408889634

LANES = 128
SUBLANES = 8
VMEM_LIMIT = 56 * 1024 * 1024

HEAD_W = 128
PAIRS_PER_GROUP = EXPERTS_PER_GROUP * (EXPERTS_PER_GROUP - 1) // 2
N_CLASSES = MOE_GROUPS * PAIRS_PER_GROUP
ROW_TILE = 128
HM_ROWS = 16
Y_ROWS = 8
MOE_SLOTS = 3
SMEM_BLOCK = 1024
HALO = 16

f32 = jnp.float32
bf16 = jnp.bfloat16


def _cparams(sem):
    return pltpu.CompilerParams(dimension_semantics=sem, vmem_limit_bytes=VMEM_LIMIT)


def _split_bf16(a):
    hi = a.astype(bf16)
    lo = (a - hi.astype(f32)).astype(bf16)
    return hi, lo


def _dot(a, b):
    return jnp.dot(a, b, preferred_element_type=f32)


def _dot3(a_hi, a_lo, b_hi, b_lo):
    return _dot(a_hi, b_hi) + (_dot(a_lo, b_hi) + _dot(a_hi, b_lo))


def _rows_load(ref, n, pitch, width):
    return jnp.concatenate([ref[pl.ds(s, n, stride=pitch), :] for s in range(width // LANES)], axis=1)


def _rows_store(ref, val, pitch):
    n, width = val.shape
    for s in range(width // LANES):
        ref[pl.ds(s, n, stride=pitch), :] = val[:, s * LANES:(s + 1) * LANES]


def _layer_norm(x, g, b):
    mu = jnp.mean(x, axis=-1, keepdims=True)
    xc = x - mu
    var = jnp.mean(xc * xc, axis=-1, keepdims=True)
    return xc * lax.rsqrt(var + LN_EPS) * g + b


def _mod_kernel(c_ref, w_ref, b_ref, o_ref):
    c = c_ref[...]
    ca = c * (1.0 / (1.0 + jnp.exp(-c)))
    a_hi, a_lo = _split_bf16(ca)
    w_hi, w_lo = _split_bf16(w_ref[...])
    o_ref[...] = _dot3(a_hi, a_lo, w_hi, w_lo) + b_ref[...]


def _modulation(c, w_mod, b_mod):
    B, D = c.shape
    depth = w_mod.shape[0]
    out = pl.pallas_call(
        _mod_kernel,
        grid=(depth, 6),
        in_specs=[
            pl.BlockSpec((B, D), lambda i, j: (0, 0)),
            pl.BlockSpec((None, D, D), lambda i, j: (i, 0, j)),
            pl.BlockSpec((None, None, 1, D), lambda i, j: (i, j, 0, 0)),
        ],
        out_specs=pl.BlockSpec((None, None, B, D), lambda i, j: (i, j, 0, 0)),
        out_shape=jax.ShapeDtypeStruct((depth, 6, B, D), f32),
        compiler_params=_cparams(("arbitrary", "arbitrary")),
        name="adaln_mod",
    )(c, w_mod, b_mod.reshape(depth, 6, 1, D))
    out = jnp.transpose(out, (0, 2, 1, 3))
    return jnp.pad(out, ((0, 0), (0, 0), (0, 2), (0, 0)))


def _mla_proj_kernel(x_ref, pos_ref, mod_ref, win_ref, qg_ref, kg_ref, wuqt_ref, wkn_ref, wvt_ref,
                     invf_ref, qt_ref, k_ref, vt_ref):
    x = x_ref[...]
    h = x * (1.0 + mod_ref[1:2, :]) + mod_ref[0:1, :]
    z = _dot(h.astype(bf16), win_ref[...])
    zq = z[:, :Q_LORA]
    zkv = z[:, Q_LORA:Q_LORA + KV_LORA]
    za = z[:, Q_LORA + KV_LORA:Q_LORA + KV_LORA + LANES]
    zb = z[:, Q_LORA + KV_LORA + LANES:]
    q_lat = zq * lax.rsqrt(jnp.mean(zq * zq, axis=-1, keepdims=True) + RMS_EPS) * qg_ref[...]
    kv_lat = zkv * lax.rsqrt(jnp.mean(zkv * zkv, axis=-1, keepdims=True) + RMS_EPS) * kg_ref[...]

    ang = pos_ref[...].astype(f32) * invf_ref[...]
    cs = jnp.cos(ang)
    sn = jnp.sin(ang)
    lane = lax.broadcasted_iota(jnp.int32, cs.shape, 1)
    scale = (QK_NOPE + QK_ROPE) ** -0.5 * LOG2E
    tqn = scale * jnp.where(lane < QK_NOPE + QK_ROPE, cs, sn)
    kr = za * cs + zb * sn

    kn = _dot(kv_lat.astype(bf16), wkn_ref[...])
    for hd in range(MLA_HEADS):
        k_ref[hd] = (kn[:, hd * HEAD_W:(hd + 1) * HEAD_W] + kr).astype(bf16)

    tqt = tqn.T
    qt = _dot(wuqt_ref[...], q_lat.T.astype(bf16))
    for hd in range(MLA_HEADS):
        qt_ref[hd] = (qt[hd * HEAD_W:(hd + 1) * HEAD_W, :] * tqt).astype(bf16)
    vt_ref[...] = _dot(wvt_ref[...], kv_lat.T.astype(bf16)).astype(bf16)


def _mla_proj(x2, pos2, mod, win, qg, kg, wuqt, wkn, wvt, invf, B, S, tm, tq):
    N, D = x2.shape
    spb = S // tm
    per_q = tq // tm
    const = lambda shape: pl.BlockSpec(shape, lambda i: (0,) * len(shape))
    nq = S // tq
    return pl.pallas_call(
        _mla_proj_kernel,
        grid=(N // tm,),
        in_specs=[
            pl.BlockSpec((tm, D), lambda i: (i, 0)),
            pl.BlockSpec((tm, 1), lambda i: (i, 0)),
            pl.BlockSpec((None, 8, D), lambda i: (i // spb, 0, 0)),
            const(win.shape), const(qg.shape), const(kg.shape), const(wuqt.shape), const(wkn.shape),
            const(wvt.shape), const(invf.shape),
        ],
        out_specs=[
            pl.BlockSpec((None, MLA_HEADS, None, HEAD_W, tm),
                         lambda i: (i // spb, 0, (i % spb) // per_q, 0, i % per_q)),
            pl.BlockSpec((None, MLA_HEADS, tm, HEAD_W), lambda i: (i // spb, 0, i % spb, 0)),
            pl.BlockSpec((None, None, MLA_HEADS * V_HEAD, tm),
                         lambda i: (i // spb, (i % spb) // per_q, 0, i % per_q)),
        ],
        out_shape=[
            jax.ShapeDtypeStruct((B, MLA_HEADS, nq, HEAD_W, tq), bf16),
            jax.ShapeDtypeStruct((B, MLA_HEADS, S, HEAD_W), bf16),
            jax.ShapeDtypeStruct((B, nq, MLA_HEADS * V_HEAD, tq), bf16),
        ],
        compiler_params=_cparams(("arbitrary",)),
        name="mla_proj",
    )(x2, pos2, mod, win, qg, kg, wuqt, wkn, wvt, invf)


ONES_ROWS = 16


def _attn_kernel(qt_ref, k_ref, vt_ref, o_ref, vx_ref, m_ref, acc_ref, ot_ref, *, tq):
    nq = qt_ref.shape[1]
    for hh in range(2):
        for j in range(nq):
            vx_ref[hh, j, :V_HEAD, :] = vt_ref[j, hh * V_HEAD:(hh + 1) * V_HEAD, :]
            vx_ref[hh, j, V_HEAD:, :] = jnp.ones((ONES_ROWS, tq), bf16)

    def kv_step(qi, j):
        scores = [_dot(k_ref[hh, pl.ds(j * tq, tq), :], qt_ref[hh, qi]) for hh in range(2)]
        for hh in range(2):
            update(hh, 0, tq, [(scores[hh], vx_ref[hh, j])])

    def update(hh, lo, hi, parts):
        m_old = m_ref[hh, :, lo:hi]
        m_new = m_old
        for s, _ in parts:
            m_new = jnp.maximum(m_new, jnp.max(s, axis=0, keepdims=True))
        alpha = jnp.exp2(m_old - m_new)
        pv = sum(_dot(vx, jnp.exp2(s - m_new[0:1, :]).astype(bf16)) for s, vx in parts)
        acc_ref[hh, :, lo:hi] = acc_ref[hh, :, lo:hi] * alpha[0:1, :] + pv
        m_ref[hh, :, lo:hi] = m_new

    def diag_step(qi):
        half = tq // 2
        k0 = qi * tq
        tops = [_dot(k_ref[hh, pl.ds(k0, half), :], qt_ref[hh, qi]) for hh in range(2)]
        bots = [_dot(k_ref[hh, pl.ds(k0 + half, half), :], qt_ref[hh, qi, :, half:]) for hh in range(2)]
        key = lax.broadcasted_iota(jnp.int32, (half, half), 0)
        qry = lax.broadcasted_iota(jnp.int32, (half, half), 1)
        visible = key <= qry
        for hh in range(2):
            vx = vx_ref[hh, qi]
            early = jnp.where(visible, tops[hh][:, :half], -jnp.inf)
            late = jnp.where(visible, bots[hh], -jnp.inf)
            update(hh, 0, half, [(early, vx[:, :half])])
            update(hh, half, tq, [(tops[hh][:, half:], vx[:, :half]), (late, vx[:, half:])])

    def q_body(qi, carry):
        m_ref[...] = jnp.full(m_ref.shape, -jnp.inf, f32)
        acc_ref[...] = jnp.zeros(acc_ref.shape, f32)

        for j in range(qi):
            kv_step(qi, j)
        diag_step(qi)
        for hh in range(2):
            acc = acc_ref[hh]
            ot_ref[hh * V_HEAD:(hh + 1) * V_HEAD, :] = acc[:V_HEAD, :] * (1.0 / acc[V_HEAD:V_HEAD + 1, :])
        o_ref[pl.ds(qi * tq, tq), :] = ot_ref[...].T.astype(bf16)
        return carry

    for qi in range(nq):
        q_body(qi, 0)


def _attention(qt, k, vt, tq):
    B, H, nq, W, _ = qt.shape
    S = nq * tq
    return pl.pallas_call(
        functools.partial(_attn_kernel, tq=tq),
        grid=(B, H // 2),
        in_specs=[
            pl.BlockSpec((None, 2, nq, W, tq), lambda b, p: (b, p, 0, 0, 0)),
            pl.BlockSpec((None, 2, S, W), lambda b, p: (b, p, 0, 0)),
            pl.BlockSpec((None, nq, 2 * V_HEAD, tq), lambda b, p: (b, 0, p, 0)),
        ],
        out_specs=pl.BlockSpec((None, S, LANES), lambda b, p: (b, 0, p)),
        out_shape=jax.ShapeDtypeStruct((B, S, H * V_HEAD), bf16),
        scratch_shapes=[
            pltpu.VMEM((2, nq, V_HEAD + ONES_ROWS, tq), bf16),
            pltpu.VMEM((2, SUBLANES, tq), f32),
            pltpu.VMEM((2, V_HEAD + ONES_ROWS, tq), f32),
            pltpu.VMEM((2 * V_HEAD, tq), f32),
        ],
        compiler_params=_cparams(("arbitrary", "arbitrary")),
        name="mla_attention",
    )(qt, k, vt)


def _post_mixer(x, y, mod_ref, ln_ref, wr_ref, br_ref, xo_ref, hm_ref, cls_ref):
    tm, D = x.shape
    x1 = _layer_norm(DN_ALPHA * x + (1.0 + mod_ref[2:3, :]) * y, ln_ref[0:1, :], ln_ref[1:2, :])
    xo_ref[...] = x1
    h2 = x1 * (1.0 + mod_ref[4:5, :]) + mod_ref[3:4, :]
    h_hi, h_lo = _split_bf16(h2)
    wr = wr_ref[...]
    t_hi = _dot(h_hi, wr)
    t_lo = _dot(h_lo, wr)
    logits = t_hi[:, :LANES] + (t_hi[:, LANES:] + t_lo[:, :LANES]) + br_ref[...]

    lane = lax.broadcasted_iota(jnp.int32, logits.shape, 1).astype(f32)
    neg = -jnp.inf
    far = float(LANES)
    is_g = lane < MOE_GROUPS
    gl = jnp.where(is_g, logits, neg)
    gmax = jnp.max(gl, axis=-1, keepdims=True)
    gidx = jnp.min(jnp.where(gl == gmax, lane, far), axis=-1, keepdims=True)
    g_p = 1.0 / jnp.sum(jnp.where(is_g, jnp.exp(logits - gmax), 0.0), axis=-1, keepdims=True)
    base = MOE_GROUPS + EXPERTS_PER_GROUP * gidx
    el = jnp.where((lane >= base) & (lane < base + EXPERTS_PER_GROUP), logits, neg)
    t1 = jnp.max(el, axis=-1, keepdims=True)
    i1 = jnp.min(jnp.where(el == t1, lane, far), axis=-1, keepdims=True)
    el2 = jnp.where(lane == i1, neg, el)
    t2 = jnp.max(el2, axis=-1, keepdims=True)
    i2 = jnp.min(jnp.where(el2 == t2, lane, far), axis=-1, keepdims=True)
    e = jnp.exp(t2 - t1)
    w1 = g_p / (1.0 + e)
    w2 = g_p * e / (1.0 + e)
    a = i1 - base
    b = i2 - base
    a_first = a < b
    lo = jnp.where(a_first, a, b)
    hi = jnp.where(a_first, b, a)
    w_lo = jnp.where(a_first, w1, w2)
    w_hi = jnp.where(a_first, w2, w1)
    pair = lo * (2 * EXPERTS_PER_GROUP - 1 - lo) * 0.5 + (hi - lo - 1.0)
    cls_ref[...] = (gidx * PAIRS_PER_GROUP + pair).astype(jnp.int32)

    _rows_store(hm_ref, h2, HM_ROWS)
    hm_ref.reshape(tm, HM_ROWS, LANES)[:, D // LANES:, :] = jnp.zeros((tm, HM_ROWS - D // LANES, LANES), f32)
    hm_ref[pl.ds(D // LANES, tm, stride=HM_ROWS), :] = jnp.broadcast_to(w_lo, (tm, LANES))
    hm_ref[pl.ds(D // LANES + 1, tm, stride=HM_ROWS), :] = jnp.broadcast_to(w_hi, (tm, LANES))


def _attn_out_kernel(o_ref, x_ref, mod_ref, wo_ref, ln_ref, wr_ref, br_ref,
                     xo_ref, hm_ref, cls_ref):
    y = _dot(o_ref[...], wo_ref[...])
    _post_mixer(x_ref[...], y, mod_ref, ln_ref, wr_ref, br_ref, xo_ref, hm_ref, cls_ref)


def _post_out_specs(N, D, tm):
    specs = [
        pl.BlockSpec((tm, D), lambda i: (i, 0)),
        pl.BlockSpec((tm * HM_ROWS, LANES), lambda i: (i, 0)),
        pl.BlockSpec((tm, 1), lambda i: (i, 0)),
    ]
    shapes = [
        jax.ShapeDtypeStruct((N, D), f32),
        jax.ShapeDtypeStruct((N * HM_ROWS, LANES), f32),
        jax.ShapeDtypeStruct((N, 1), jnp.int32),
    ]
    return specs, shapes


def _attn_out(o2, x2, mod, wo, ln, wr, br, S, tm):
    N, D = x2.shape
    spb = S // tm
    const = lambda shape: pl.BlockSpec(shape, lambda i: (0,) * len(shape))
    out_specs, out_shapes = _post_out_specs(N, D, tm)
    return pl.pallas_call(
        _attn_out_kernel,
        grid=(N // tm,),
        in_specs=[
            pl.BlockSpec((tm, D), lambda i: (i, 0)),
            pl.BlockSpec((tm, D), lambda i: (i, 0)),
            pl.BlockSpec((None, 8, D), lambda i: (i // spb, 0, 0)),
            const(wo.shape), const(ln.shape), const(wr.shape), const(br.shape),
        ],
        out_specs=out_specs,
        out_shape=out_shapes,
        compiler_params=_cparams(("arbitrary",)),
        name="attn_out_router",
    )(o2, x2, mod, wo, ln, wr, br)


def _pool_kernel(x_ref, y_ref, modp_ref, lnp_ref, mod_ref, wp_ref, ps_ref, ln_ref,
                 wr_ref, br_ref, xo_ref, hm_ref, cls_ref, hb_ref, *, spb):
    tm, D = x_ref.shape
    i = pl.program_id(0)
    t_blk = i % spb
    yprev = _rows_load(y_ref, tm, Y_ROWS, D)
    x2 = _layer_norm(DN_ALPHA * x_ref[...] + (1.0 + modp_ref[5:6, :]) * yprev,
                     lnp_ref[0:1, :], lnp_ref[1:2, :])
    h = x2 * (1.0 + mod_ref[1:2, :]) + mod_ref[0:1, :]

    @pl.when(t_blk == 0)
    def _():
        hb_ref[0:HALO, :] = jnp.zeros((HALO, D), f32)

    hb_ref[HALO:, :] = h
    t_seq = t_blk * tm + lax.broadcasted_iota(jnp.int32, (tm, 1), 0)
    gd = D // len(POOL_WINDOWS)
    ys = []
    for gi, w in enumerate(POOL_WINDOWS):
        c0, c1 = gi * gd, (gi + 1) * gd
        a = hb_ref[:, c0:c1]
        span = 1
        while span < w:
            a = a[span:, :] + a[:-span, :]
            span *= 2
        tsum = a[HALO - (w - 1):, :]
        cnt = jnp.minimum(t_seq + 1, w).astype(f32)
        mixed = tsum / cnt - h[:, c0:c1]
        ys.append(_dot(mixed.astype(bf16), wp_ref[gi]))
    hb_ref[0:HALO, :] = h[tm - HALO:, :]
    y = jnp.concatenate(ys, axis=1) * ps_ref[...]
    _post_mixer(x2, y, mod_ref, ln_ref, wr_ref, br_ref, xo_ref, hm_ref, cls_ref)


def _pool_layer(x2, yprev, modp, lnp, mod, wp, ps, ln, wr, br, S, tm):
    N, D = x2.shape
    spb = S // tm
    const = lambda shape: pl.BlockSpec(shape, lambda i: (0,) * len(shape))
    out_specs, out_shapes = _post_out_specs(N, D, tm)
    return pl.pallas_call(
        functools.partial(_pool_kernel, spb=spb),
        grid=(N // tm,),
        in_specs=[
            pl.BlockSpec((tm, D), lambda i: (i, 0)),
            pl.BlockSpec((tm * Y_ROWS, LANES), lambda i: (i, 0)),
            pl.BlockSpec((None, 8, D), lambda i: (i // spb, 0, 0)),
            const(lnp.shape),
            pl.BlockSpec((None, 8, D), lambda i: (i // spb, 0, 0)),
            const(wp.shape), const(ps.shape), const(ln.shape),
            const(wr.shape), const(br.shape),
        ],
        out_specs=out_specs,
        out_shape=out_shapes,
        scratch_shapes=[pltpu.VMEM((HALO + tm, D), f32)],
        compiler_params=_cparams(("arbitrary",)),
        name="pool_mixer_router",
    )(x2, yprev, modp, lnp, mod, wp, ps, ln, wr, br)


def _final_kernel(x_ref, y_ref, mod_ref, ln_ref, o_ref):
    tm, D = x_ref.shape
    y = _rows_load(y_ref, tm, Y_ROWS, D)
    o_ref[...] = _layer_norm(DN_ALPHA * x_ref[...] + (1.0 + mod_ref[5:6, :]) * y,
                             ln_ref[0:1, :], ln_ref[1:2, :])


def _final_merge(x2, y, mod, ln, S, tm):
    N, D = x2.shape
    spb = S // tm
    return pl.pallas_call(
        _final_kernel,
        grid=(N // tm,),
        in_specs=[
            pl.BlockSpec((tm, D), lambda i: (i, 0)),
            pl.BlockSpec((tm * Y_ROWS, LANES), lambda i: (i, 0)),
            pl.BlockSpec((None, 8, D), lambda i: (i // spb, 0, 0)),
            pl.BlockSpec(ln.shape, lambda i: (0, 0)),
        ],
        out_specs=pl.BlockSpec((tm, D), lambda i: (i, 0)),
        out_shape=jax.ShapeDtypeStruct((N, D), f32),
        compiler_params=_cparams(("arbitrary",)),
        name="final_merge",
    )(x2, y, mod, ln)


def _rank_kernel(cls_ref, pos_ref, cnt_ref, carry_ref, off_ref, earlier_ref):
    tr = cls_ref.shape[0]
    phase, tile = pl.program_id(0), pl.program_id(1)

    @pl.when(tile == 0)
    def _():
        @pl.when(phase == 0)
        def _():
            r = lax.broadcasted_iota(jnp.int32, (tr, tr), 0)
            c = lax.broadcasted_iota(jnp.int32, (tr, tr), 1)
            earlier_ref[...] = (c < r).astype(bf16)

        @pl.when(phase == 1)
        def _():
            tiles = jnp.floor((carry_ref[...] + (ROW_TILE - 1)) * (1.0 / ROW_TILE))
            hi = jnp.floor(tiles * (1.0 / 16.0))
            lo = tiles - 16.0 * hi
            before = (lax.broadcasted_iota(jnp.int32, (LANES, LANES), 0)
                      < lax.broadcasted_iota(jnp.int32, (LANES, LANES), 1)).astype(bf16)
            starts = 16.0 * _dot(hi.astype(bf16), before) + _dot(lo.astype(bf16), before)
            off_ref[...] = starts * ROW_TILE

        carry_ref[...] = jnp.zeros(carry_ref.shape, f32)

    lane = lax.broadcasted_iota(jnp.int32, (tr, LANES), 1)
    onehot = (cls_ref[...] == lane)
    oh = onehot.astype(bf16)
    total = carry_ref[0:1, :] + jnp.sum(oh.astype(f32), axis=0, keepdims=True)

    @pl.when(phase == 0)
    def _():
        cnt_ref[...] = jnp.broadcast_to(total, cnt_ref.shape).astype(jnp.int32)

    @pl.when(phase == 1)
    def _():
        ahead = _dot(earlier_ref[...], oh) + (carry_ref[0:1, :] + off_ref[0:1, :])
        pos_ref[...] = jnp.sum(jnp.where(onehot, ahead, 0.0), axis=-1, keepdims=True).astype(jnp.int32)

    carry_ref[...] = jnp.broadcast_to(total, carry_ref.shape)


def _class_slots(cls, tr):
    N = cls.shape[0]
    return pl.pallas_call(
        _rank_kernel,
        grid=(2, N // tr),
        in_specs=[pl.BlockSpec((tr, 1), lambda p, i: (i, 0))],
        out_specs=[
            pl.BlockSpec((tr, 1), lambda p, i: (i * p, 0)),
            pl.BlockSpec((SUBLANES, LANES), lambda p, i: (0, 0)),
        ],
        out_shape=[
            jax.ShapeDtypeStruct((N, 1), jnp.int32),
            jax.ShapeDtypeStruct((SUBLANES, LANES), jnp.int32),
        ],
        scratch_shapes=[pltpu.VMEM((SUBLANES, LANES), f32), pltpu.VMEM((SUBLANES, LANES), f32),
                        pltpu.VMEM((tr, tr), bf16)],
        compiler_params=_cparams(("arbitrary", "arbitrary")),
        name="class_slots",
    )(cls)


def _slot_token_kernel(pos_ref, inv_ref, *, chunk):
    base = pl.program_id(0) * chunk

    @pl.when(pl.program_id(0) == 0)
    def _():
        def clear(s, c):
            inv_ref[s] = 0
            return c
        lax.fori_loop(0, inv_ref.shape[0], clear, 0, unroll=32)

    def body(t, c):
        inv_ref[pos_ref[t]] = base + t
        return c

    lax.fori_loop(0, chunk, body, 0, unroll=16)


def _slot_tokens(pos, n_slots, chunk):
    N = pos.shape[0]
    return pl.pallas_call(
        functools.partial(_slot_token_kernel, chunk=chunk),
        grid=(N // chunk,),
        in_specs=[pl.BlockSpec((chunk,), lambda i: (i,), memory_space=pltpu.SMEM)],
        out_specs=pl.BlockSpec((n_slots,), lambda i: (0,), memory_space=pltpu.SMEM),
        out_shape=jax.ShapeDtypeStruct((n_slots,), jnp.int32),
        compiler_params=_cparams(("arbitrary",)),
        name="slot_tokens",
    )(pos)


def _moe_kernel(te1_ref, te2_ref, cnt_ref, inv_ref, hm_ref, wgu1_ref, wd1_ref, wgu2_ref, wd2_ref,
                y_ref, *scratch):
    hbufs = scratch[:MOE_SLOTS]
    ybufs = scratch[MOE_SLOTS:2 * MOE_SLOTS]
    gsem, ssem = scratch[2 * MOE_SLOTS:]
    ff, D = wd1_ref.shape
    i = pl.program_id(0)

    def count(tile):
        return jnp.where(tile >= 0, cnt_ref[jnp.maximum(tile, 0)], 0)

    n_cur, n_prev, n_ahead, n_old = count(i), count(i - 1), count(i + 2), count(i - MOE_SLOTS)

    def token_rows(ref, t, pitch):
        return ref.at[pl.ds(pl.multiple_of(t * pitch, pitch), pitch)]

    def start_gather(tok, r, hb, sem):
        pltpu.make_async_copy(token_rows(hm_ref, tok, HM_ROWS), token_rows(hb, r, HM_ROWS), sem).start()

    def start_scatter(tok, r, yb, sem):
        pltpu.make_async_copy(token_rows(yb, r, Y_ROWS), token_rows(y_ref, tok, Y_ROWS), sem).start(priority=r % 2)

    def wait_tokens(src, dst, sem, n, pitch):
        @pl.when(n > 0)
        def _():
            rows = pl.multiple_of(n * pitch, pitch)
            pltpu.make_async_copy(src.at[pl.ds(0, rows)], dst.at[pl.ds(0, rows)], sem).wait()

    @pl.when(i == 0)
    def _():
        for t in range(MOE_SLOTS - 1):
            def first(r, c):
                start_gather(inv_ref[t * ROW_TILE + r], r, hbufs[t], gsem.at[t])
                return c
            lax.fori_loop(0, cnt_ref[t], first, 0)

    def step(cur):
        far = (cur + MOE_SLOTS - 1) % MOE_SLOTS
        hb, yb = hbufs[cur], ybufs[cur]
        wait_tokens(hm_ref, hb, gsem.at[cur], n_cur, HM_ROWS)
        wait_tokens(yb, y_ref, ssem.at[cur], n_old, Y_ROWS)

        @pl.when((n_cur > 0) | (n_prev > 0))
        def _():
            ahead = (i + 2) * ROW_TILE
            prev = jnp.maximum(i - 1, 0) * ROW_TILE
            for r in range(ROW_TILE):
                tok_ahead = inv_ref[ahead + r]
                tok_prev = inv_ref[prev + r]

                @pl.when(r < n_ahead)
                def _():
                    start_gather(tok_ahead, r, hbufs[far], gsem.at[far])

                @pl.when(r < n_prev)
                def _():
                    start_scatter(tok_prev, r, ybufs[far], ssem.at[far])

            h = _rows_load(hb, ROW_TILE, HM_ROWS, D).astype(bf16)

            def hidden(gu, gate_row):
                w = hb[pl.ds(gate_row, ROW_TILE, stride=HM_ROWS), :]
                g, u = gu[:, :ff], gu[:, ff:]
                he = g * (1.0 / (1.0 + jnp.exp(-g))) * u
                return (he * jnp.concatenate([w] * (ff // LANES), axis=1)).astype(bf16)

            gu1 = _dot(h, wgu1_ref[...])
            gu2 = _dot(h, wgu2_ref[...])
            y = _dot(hidden(gu1, D // LANES), wd1_ref[...]) + _dot(hidden(gu2, D // LANES + 1), wd2_ref[...])
            _rows_store(yb, y, Y_ROWS)

    for cur in range(MOE_SLOTS):
        @pl.when(i % MOE_SLOTS == cur)
        def _():
            step(cur)


def _moe(te1, te2, cnt, inv, hm, wgu, wd, layer, n_steps):
    N = hm.shape[0] // HM_ROWS
    _, E, D, ff2 = wgu.shape
    ff = ff2 // 2

    def wmap(which):
        def f(i, te1, te2, cnt, inv):
            return (layer, (te1, te2)[which][i], 0, 0)
        return f

    return pl.pallas_call(
        _moe_kernel,
        grid_spec=pltpu.PrefetchScalarGridSpec(
            num_scalar_prefetch=4,
            grid=(n_steps,),
            in_specs=[
                pl.BlockSpec(memory_space=pl.ANY),
                pl.BlockSpec((None, None, D, ff2), wmap(0)),
                pl.BlockSpec((None, None, ff, D), wmap(0)),
                pl.BlockSpec((None, None, D, ff2), wmap(1)),
                pl.BlockSpec((None, None, ff, D), wmap(1)),
            ],
            out_specs=pl.BlockSpec(memory_space=pl.ANY),
            scratch_shapes=(
                [pltpu.VMEM((ROW_TILE * HM_ROWS, LANES), f32)] * MOE_SLOTS
                + [pltpu.VMEM((ROW_TILE * Y_ROWS, LANES), f32)] * MOE_SLOTS
                + [pltpu.SemaphoreType.DMA((MOE_SLOTS,)), pltpu.SemaphoreType.DMA((MOE_SLOTS,))]
            ),
        ),
        out_shape=jax.ShapeDtypeStruct((N * Y_ROWS, LANES), f32),
        compiler_params=_cparams(("arbitrary",)),
        name="pair_moe",
    )(te1, te2, cnt, inv, hm, wgu, wd, wgu, wd)


def _pair_tables():
    lo = np.zeros((N_CLASSES,), np.int32)
    hi = np.zeros((N_CLASSES,), np.int32)
    for g in range(MOE_GROUPS):
        p = 0
        for a in range(EXPERTS_PER_GROUP):
            for b in range(a + 1, EXPERTS_PER_GROUP):
                lo[g * PAIRS_PER_GROUP + p] = g * EXPERTS_PER_GROUP + a
                hi[g * PAIRS_PER_GROUP + p] = g * EXPERTS_PER_GROUP + b
                p += 1
    return lo, hi


def _moe_layer(hm, cls, expert_w, layer, chunk, tr):
    N = cls.shape[0]
    n_tiles = N // ROW_TILE + N_CLASSES
    n_steps = n_tiles + MOE_SLOTS
    pos, cnt = _class_slots(cls, tr)
    counts = cnt[0, :N_CLASSES]
    tiles = (counts + ROW_TILE - 1) // ROW_TILE
    tile_end = jnp.cumsum(tiles)
    tile_start = tile_end - tiles
    nused = tile_end[-1]
    step = jnp.arange(n_steps + 2, dtype=jnp.int32)
    tile_cls = jnp.sum(tile_end[None, :] <= jnp.minimum(step, nused - 1)[:, None], axis=1)
    tile_cls = jnp.minimum(tile_cls, N_CLASSES - 1).astype(jnp.int32)
    lo_tab, hi_tab = _pair_tables()
    te1 = jnp.asarray(lo_tab)[tile_cls]
    te2 = jnp.asarray(hi_tab)[tile_cls]
    in_cls = (step - tile_start[tile_cls]) * ROW_TILE
    tile_cnt = jnp.where(step < nused, jnp.clip(counts[tile_cls] - in_cls, 0, ROW_TILE), 0).astype(jnp.int32)

    n_slots = -(-(n_tiles + MOE_SLOTS) * ROW_TILE // SMEM_BLOCK) * SMEM_BLOCK
    inv = _slot_tokens(pos.reshape(N), n_slots, chunk)
    return _moe(te1, te2, tile_cnt, inv, hm, *expert_w, layer, n_steps)


def _swap_rope(w):
    half = QK_ROPE // 2
    return jnp.concatenate([-w[..., half:], w[..., :half]], axis=-1)


def _prep_mla_weights(w_in, w_uq, w_ukv):
    D = w_in.shape[0]
    w_kr = w_in[:, Q_LORA + KV_LORA:]
    w_ks = _swap_rope(w_kr)
    zeros = jnp.zeros((D, QK_NOPE), w_in.dtype)
    win = jnp.concatenate([w_in[:, :Q_LORA + KV_LORA], zeros, w_kr, w_kr, zeros, w_ks, w_ks], axis=1)
    uq = w_uq.reshape(Q_LORA, MLA_HEADS, QK_NOPE + QK_ROPE)
    rope = uq[..., QK_NOPE:]
    wuq = jnp.concatenate([uq[..., :QK_NOPE], rope, _swap_rope(rope)], axis=-1).reshape(Q_LORA, -1)
    ukv = w_ukv.reshape(KV_LORA, MLA_HEADS, QK_NOPE + V_HEAD)
    wkn = jnp.concatenate([ukv[..., :QK_NOPE], jnp.zeros((KV_LORA, MLA_HEADS, HEAD_W - QK_NOPE), w_ukv.dtype)],
                          axis=-1).reshape(KV_LORA, -1)
    wv = ukv[..., QK_NOPE:].reshape(KV_LORA, -1)
    return win.astype(bf16), wuq.T.astype(bf16), wkn.astype(bf16), wv.T.astype(bf16)


def _rope_inv_freq():
    inv = 1.0 / (ROPE_THETA ** (np.arange(0, QK_ROPE, 2, dtype=np.float32) / QK_ROPE))
    inv2 = np.concatenate([inv, inv]).astype(np.float32)
    return np.concatenate([np.zeros((QK_NOPE,), np.float32), inv2, inv2]).reshape(1, LANES)


def _prep_router(w_gr, b_gr, w_er, b_er):
    D = w_gr.shape[0]
    n = MOE_GROUPS + MOE_GROUPS * EXPERTS_PER_GROUP
    w = jnp.concatenate([w_gr, w_er, jnp.zeros((D, LANES - n), f32)], axis=1)
    b = jnp.concatenate([b_gr, b_er, jnp.zeros((LANES - n,), f32)]).reshape(1, LANES)
    hi, lo = _split_bf16(w)
    return jnp.concatenate([hi, lo], axis=1), b


def kernel(x, c, positions, w_mod, b_mod, ln_g, ln_b, w_in, q_norm_g, kv_norm_g, w_uq, w_ukv, w_o, w_pool,
           pool_scale, w_group_router, b_group_router, w_expert_router, b_expert_router, w_gate, w_up, w_down):
    B, S, D = x.shape
    N = B * S
    tm = min(512, S)
    tq = min(1024, S)
    tr = min(1024, N)
    chunk = min(2048, N)

    mod = _modulation(c, w_mod, b_mod)
    ln = jnp.stack([ln_g, ln_b], axis=2)
    x2 = x.reshape(N, D)

    win, wuqt, wkn, wvt = _prep_mla_weights(w_in[0], w_uq[0], w_ukv[0])
    qt, k, vt = _mla_proj(x2, positions.reshape(N, 1), mod[0], win, q_norm_g[0].reshape(1, -1),
                          kv_norm_g[0].reshape(1, -1), wuqt, wkn, wvt, jnp.asarray(_rope_inv_freq()),
                          B, S, tm, tq)
    o = _attention(qt, k, vt, tq)
    wr, br = _prep_router(w_group_router[0], b_group_router[0], w_expert_router[0], b_expert_router[0])
    x1, hm, cls = _attn_out(o.reshape(N, D), x2, mod[0], w_o[0].astype(bf16), ln[0, 0], wr, br, S, tm)
    expert_w = (jnp.concatenate([w_gate, w_up], axis=-1).astype(bf16), w_down.astype(bf16))
    y0 = _moe_layer(hm, cls, expert_w, 0, chunk, tr)

    wr, br = _prep_router(w_group_router[1], b_group_router[1], w_expert_router[1], b_expert_router[1])
    x3, hm, cls = _pool_layer(x1, y0, mod[0], ln[0, 1], mod[1], w_pool[0].astype(bf16),
                              pool_scale[0].reshape(1, D), ln[1, 0], wr, br, S, tm)
    y1 = _moe_layer(hm, cls, expert_w, 1, chunk, tr)

    out = _final_merge(x3, y1, mod[1], ln[1, 1], S, tm)
    return out.reshape(B, S, D)
```
